```python
import jax, jax.numpy as jnp
from jax import lax
import numpy as np

D_MODEL = 1024
BATCH = 8
SEQ = 8192
DEPTH = 4

D_MIX = D_MODEL
D_POOL = D_MIX // 4
POOL_WINDOWS = (2, 4, 8, 16)
N_POOL_GROUPS = len(POOL_WINDOWS)
POOL_GROUP = D_POOL // N_POOL_GROUPS
D_CONF = 3 * D_MIX // 8
D_SCONV = D_MIX - D_POOL - D_CONF
CONF_KERNEL = 31
SCONV_KERNEL = 3
D_IN = D_POOL + 2 * D_CONF + 3 * D_SCONV
N_MEM = 256
XATTN_HEADS = 4
XATTN_HEAD_DIM = D_MODEL // XATTN_HEADS
D_FF = 2816
FFN_CONV_KERNEL = 3
EPS = 1e-6

kernel_name = "hybrid_pool_conformer_shortconv_trunk"


def rms_norm(x, g):
    x32 = x.astype(jnp.float32)
    y = x32 * lax.rsqrt(jnp.mean(x32 * x32, axis=-1, keepdims=True) + EPS)
    return (y * g.astype(jnp.float32)).astype(x.dtype)


def layer_norm(x, g, b):
    x32 = x.astype(jnp.float32)
    mu = jnp.mean(x32, axis=-1, keepdims=True)
    xc = x32 - mu
    y = xc * lax.rsqrt(jnp.mean(xc * xc, axis=-1, keepdims=True) + EPS)
    return (y * g.astype(jnp.float32) + b.astype(jnp.float32)).astype(x.dtype)


def causal_dwconv(u, w):
    k, c = w.shape
    return lax.conv_general_dilated(
        u, w[:, None, :].astype(u.dtype), window_strides=(1,), padding=[(k - 1, 0)],
        dimension_numbers=("NWC", "WIO", "NWC"), feature_group_count=c)


def pool_mixer(u, maps, scale):
    t_len = u.shape[1]
    u32 = u.astype(jnp.float32)
    cs = jnp.cumsum(u32, axis=1)
    pos1 = jnp.arange(1, t_len + 1, dtype=jnp.int32)
    outs = []
    for g, w in enumerate(POOL_WINDOWS):
        sl = slice(g * POOL_GROUP, (g + 1) * POOL_GROUP)
        cs_g = cs[..., sl]
        prev = jnp.pad(cs_g, ((0, 0), (w, 0), (0, 0)))[:, :t_len]
        count = jnp.minimum(pos1, w).astype(jnp.float32)[None, :, None]
        pooled = ((cs_g - prev) / count - u32[..., sl]).astype(u.dtype)
        outs.append(jnp.einsum("btc,cd->btd", pooled, maps[g]))
    return jnp.concatenate(outs, axis=-1) * scale


def conformer_conv(a, gate, w_dw, b_dw, ln_g, ln_b):
    v = a * jax.nn.sigmoid(gate)
    v = causal_dwconv(v, w_dw) + b_dw
    v = layer_norm(v, ln_g, ln_b)
    return jax.nn.silu(v)


def short_gated_conv(bg, cg, xv, w):
    return bg * causal_dwconv(cg * xv, w)


def cross_attention(h, mem_n, wq, wk, wv, wo):
    b, t, _ = h.shape
    m = mem_n.shape[1]
    q = (h @ wq).reshape(b, t, XATTN_HEADS, XATTN_HEAD_DIM)
    k = (mem_n @ wk).reshape(b, m, XATTN_HEADS, XATTN_HEAD_DIM)
    v = (mem_n @ wv).reshape(b, m, XATTN_HEADS, XATTN_HEAD_DIM)
    s = jnp.einsum("bthd,bmhd->bhtm", q, k).astype(jnp.float32) * (XATTN_HEAD_DIM ** -0.5)
    p = jax.nn.softmax(s, axis=-1).astype(h.dtype)
    o = jnp.einsum("bhtm,bmhd->bthd", p, v).reshape(b, t, XATTN_HEADS * XATTN_HEAD_DIM)
    return o @ wo


def conv_ffn(h, w_up, w_conv, w_down):
    u = causal_dwconv(h @ w_up, w_conv)
    gate, val = jnp.split(u, 2, axis=-1)
    return (jax.nn.silu(gate) * val) @ w_down


def _fwd_setup_inputs(seed: int = 0) -> dict:
    key = jax.random.key(seed)
    ks = jax.random.split(key, 26)
    f32 = jnp.float32

    def nrm(k, shape, scale):
        return jax.random.normal(k, shape, f32) * scale

    def gain(k, shape):
        return 1.0 + 0.05 * jax.random.normal(k, shape, f32)

    L, D = DEPTH, D_MODEL
    return {
        "x": nrm(ks[0], (BATCH, SEQ, D), 1.0),
        "mem": nrm(ks[1], (BATCH, N_MEM, D), 1.0),
        "mem_norm": gain(ks[2], (D,)),
        "mix_pre_norm": gain(ks[3], (L, D)),
        "mix_post_norm": gain(ks[4], (L, D)),
        "w_in": nrm(ks[5], (L, D, D_IN), D ** -0.5),
        "pool_maps": nrm(ks[6], (L, N_POOL_GROUPS, POOL_GROUP, POOL_GROUP), POOL_GROUP ** -0.5),
        "pool_scale": gain(ks[7], (L, D_POOL)),
        "conf_dw_w": nrm(ks[8], (L, CONF_KERNEL, D_CONF), CONF_KERNEL ** -0.5),
        "conf_dw_b": nrm(ks[9], (L, D_CONF), 0.01),
        "conf_ln_g": gain(ks[10], (L, D_CONF)),
        "conf_ln_b": nrm(ks[11], (L, D_CONF), 0.01),
        "sconv_w": nrm(ks[12], (L, SCONV_KERNEL, D_SCONV), SCONV_KERNEL ** -0.5),
        "w_out": nrm(ks[13], (L, D_MIX, D), D_MIX ** -0.5),
        "xattn_pre_norm": gain(ks[14], (L, D)),
        "xattn_post_norm": gain(ks[15], (L, D)),
        "xattn_wq": nrm(ks[16], (L, D, D), D ** -0.5),
        "xattn_wk": nrm(ks[17], (L, D, D), D ** -0.5),
        "xattn_wv": nrm(ks[18], (L, D, D), D ** -0.5),
        "xattn_wo": nrm(ks[19], (L, D, D), D ** -0.5),
        "ffn_pre_norm": gain(ks[20], (L, D)),
        "ffn_post_norm": gain(ks[21], (L, D)),
        "ffn_w_up": nrm(ks[22], (L, D, 2 * D_FF), D ** -0.5),
        "ffn_conv_w": nrm(ks[23], (L, FFN_CONV_KERNEL, 2 * D_FF), FFN_CONV_KERNEL ** -0.5),
        "ffn_w_down": nrm(ks[24], (L, D_FF, D), D_FF ** -0.5),
    }


def _fwd_reference(x, mem, mem_norm, mix_pre_norm, mix_post_norm, w_in, pool_maps, pool_scale,
              conf_dw_w, conf_dw_b, conf_ln_g, conf_ln_b, sconv_w, w_out,
              xattn_pre_norm, xattn_post_norm, xattn_wq, xattn_wk, xattn_wv, xattn_wo,
              ffn_pre_norm, ffn_post_norm, ffn_w_up, ffn_conv_w, ffn_w_down):
    split_at = [D_POOL, D_POOL + D_CONF, D_POOL + 2 * D_CONF,
                D_POOL + 2 * D_CONF + D_SCONV, D_POOL + 2 * D_CONF + 2 * D_SCONV]
    mem_n = rms_norm(mem, mem_norm)
    for l in range(DEPTH):
        h = rms_norm(x, mix_pre_norm[l])
        z = h @ w_in[l]
        zp, za, zg, zb, zc, zx = jnp.split(z, split_at, axis=-1)
        ya = pool_mixer(zp, pool_maps[l], pool_scale[l])
        yb = conformer_conv(za, zg, conf_dw_w[l], conf_dw_b[l], conf_ln_g[l], conf_ln_b[l])
        yc = short_gated_conv(zb, zc, zx, sconv_w[l])
        y = jnp.concatenate([ya, yb, yc], axis=-1) @ w_out[l]
        x = x + rms_norm(y, mix_post_norm[l])
        h = rms_norm(x, xattn_pre_norm[l])
        y = cross_attention(h, mem_n, xattn_wq[l], xattn_wk[l], xattn_wv[l], xattn_wo[l])
        x = x + rms_norm(y, xattn_post_norm[l])
        h = rms_norm(x, ffn_pre_norm[l])
        y = conv_ffn(h, ffn_w_up[l], ffn_conv_w[l], ffn_w_down[l])
        x = x + rms_norm(y, ffn_post_norm[l])
    return x


import jax as _jax
import jax.numpy as _jnp

TWIN_FORMAT = 'train_step'
FWD_PARAMS = ['x', 'mem', 'mem_norm', 'mix_pre_norm', 'mix_post_norm', 'w_in', 'pool_maps', 'pool_scale', 'conf_dw_w', 'conf_dw_b', 'conf_ln_g', 'conf_ln_b', 'sconv_w', 'w_out', 'xattn_pre_norm', 'xattn_post_norm', 'xattn_wq', 'xattn_wk', 'xattn_wv', 'xattn_wo', 'ffn_pre_norm', 'ffn_post_norm', 'ffn_w_up', 'ffn_conv_w', 'ffn_w_down']
TWIN_WEIGHTS = ['mem_norm', 'mix_pre_norm', 'mix_post_norm', 'w_in', 'pool_maps', 'pool_scale', 'conf_dw_w', 'conf_dw_b', 'conf_ln_g', 'conf_ln_b', 'sconv_w', 'w_out', 'xattn_pre_norm', 'xattn_post_norm', 'xattn_wq', 'xattn_wk', 'xattn_wv', 'xattn_wo', 'ffn_pre_norm', 'ffn_post_norm', 'ffn_w_up', 'ffn_conv_w', 'ffn_w_down']
TWIN_DIFF_INPUT = 'x'
TWIN_INPUTS = ['x', 'mem', 'mem_norm', 'mix_pre_norm', 'mix_post_norm', 'w_in', 'pool_maps', 'pool_scale', 'conf_dw_w', 'conf_dw_b', 'conf_ln_g', 'conf_ln_b', 'sconv_w', 'w_out', 'xattn_pre_norm', 'xattn_post_norm', 'xattn_wq', 'xattn_wk', 'xattn_wv', 'xattn_wo', 'ffn_pre_norm', 'ffn_post_norm', 'ffn_w_up', 'ffn_conv_w', 'ffn_w_down', 'loss_target', 'm_mem_norm', 'm_mix_pre_norm', 'm_mix_post_norm', 'm_w_in', 'm_pool_maps', 'm_pool_scale', 'm_conf_dw_w', 'm_conf_dw_b', 'm_conf_ln_g', 'm_conf_ln_b', 'm_sconv_w', 'm_w_out', 'm_xattn_pre_norm', 'm_xattn_post_norm', 'm_xattn_wq', 'm_xattn_wk', 'm_xattn_wv', 'm_xattn_wo', 'm_ffn_pre_norm', 'm_ffn_post_norm', 'm_ffn_w_up', 'm_ffn_conv_w', 'm_ffn_w_down', 'v_mem_norm', 'v_mix_pre_norm', 'v_mix_post_norm', 'v_w_in', 'v_pool_maps', 'v_pool_scale', 'v_conf_dw_w', 'v_conf_dw_b', 'v_conf_ln_g', 'v_conf_ln_b', 'v_sconv_w', 'v_w_out', 'v_xattn_pre_norm', 'v_xattn_post_norm', 'v_xattn_wq', 'v_xattn_wk', 'v_xattn_wv', 'v_xattn_wo', 'v_ffn_pre_norm', 'v_ffn_post_norm', 'v_ffn_w_up', 'v_ffn_conv_w', 'v_ffn_w_down']
TWIN_OUTPUTS = ['loss', 'grad_x', 'grad_mem_norm', 'grad_mix_pre_norm', 'grad_mix_post_norm', 'grad_w_in', 'grad_pool_maps', 'grad_pool_scale', 'grad_conf_dw_w', 'grad_conf_dw_b', 'grad_conf_ln_g', 'grad_conf_ln_b', 'grad_sconv_w', 'grad_w_out', 'grad_xattn_pre_norm', 'grad_xattn_post_norm', 'grad_xattn_wq', 'grad_xattn_wk', 'grad_xattn_wv', 'grad_xattn_wo', 'grad_ffn_pre_norm', 'grad_ffn_post_norm', 'grad_ffn_w_up', 'grad_ffn_conv_w', 'grad_ffn_w_down', 'delta_mem_norm', 'delta_mix_pre_norm', 'delta_mix_post_norm', 'delta_w_in', 'delta_pool_maps', 'delta_pool_scale', 'delta_conf_dw_w', 'delta_conf_dw_b', 'delta_conf_ln_g', 'delta_conf_ln_b', 'delta_sconv_w', 'delta_w_out', 'delta_xattn_pre_norm', 'delta_xattn_post_norm', 'delta_xattn_wq', 'delta_xattn_wk', 'delta_xattn_wv', 'delta_xattn_wo', 'delta_ffn_pre_norm', 'delta_ffn_post_norm', 'delta_ffn_w_up', 'delta_ffn_conv_w', 'delta_ffn_w_down', 'new_m_mem_norm', 'new_m_mix_pre_norm', 'new_m_mix_post_norm', 'new_m_w_in', 'new_m_pool_maps', 'new_m_pool_scale', 'new_m_conf_dw_w', 'new_m_conf_dw_b', 'new_m_conf_ln_g', 'new_m_conf_ln_b', 'new_m_sconv_w', 'new_m_w_out', 'new_m_xattn_pre_norm', 'new_m_xattn_post_norm', 'new_m_xattn_wq', 'new_m_xattn_wk', 'new_m_xattn_wv', 'new_m_xattn_wo', 'new_m_ffn_pre_norm', 'new_m_ffn_post_norm', 'new_m_ffn_w_up', 'new_m_ffn_conv_w', 'new_m_ffn_w_down', 'new_v_mem_norm', 'new_v_mix_pre_norm', 'new_v_mix_post_norm', 'new_v_w_in', 'new_v_pool_maps', 'new_v_pool_scale', 'new_v_conf_dw_w', 'new_v_conf_dw_b', 'new_v_conf_ln_g', 'new_v_conf_ln_b', 'new_v_sconv_w', 'new_v_w_out', 'new_v_xattn_pre_norm', 'new_v_xattn_post_norm', 'new_v_xattn_wq', 'new_v_xattn_wk', 'new_v_xattn_wv', 'new_v_xattn_wo', 'new_v_ffn_pre_norm', 'new_v_ffn_post_norm', 'new_v_ffn_w_up', 'new_v_ffn_conv_w', 'new_v_ffn_w_down']
TWIN_LEAF_KINDS = {'loss': 'loss', 'grad_x': 'grad_x', 'grad_mem_norm': 'grad_w', 'grad_mix_pre_norm': 'grad_w', 'grad_mix_post_norm': 'grad_w', 'grad_w_in': 'grad_w', 'grad_pool_maps': 'grad_w', 'grad_pool_scale': 'grad_w', 'grad_conf_dw_w': 'grad_w', 'grad_conf_dw_b': 'grad_w', 'grad_conf_ln_g': 'grad_w', 'grad_conf_ln_b': 'grad_w', 'grad_sconv_w': 'grad_w', 'grad_w_out': 'grad_w', 'grad_xattn_pre_norm': 'grad_w', 'grad_xattn_post_norm': 'grad_w', 'grad_xattn_wq': 'grad_w', 'grad_xattn_wk': 'grad_w', 'grad_xattn_wv': 'grad_w', 'grad_xattn_wo': 'grad_w', 'grad_ffn_pre_norm': 'grad_w', 'grad_ffn_post_norm': 'grad_w', 'grad_ffn_w_up': 'grad_w', 'grad_ffn_conv_w': 'grad_w', 'grad_ffn_w_down': 'grad_w', 'delta_mem_norm': 'delta_w', 'delta_mix_pre_norm': 'delta_w', 'delta_mix_post_norm': 'delta_w', 'delta_w_in': 'delta_w', 'delta_pool_maps': 'delta_w', 'delta_pool_scale': 'delta_w', 'delta_conf_dw_w': 'delta_w', 'delta_conf_dw_b': 'delta_w', 'delta_conf_ln_g': 'delta_w', 'delta_conf_ln_b': 'delta_w', 'delta_sconv_w': 'delta_w', 'delta_w_out': 'delta_w', 'delta_xattn_pre_norm': 'delta_w', 'delta_xattn_post_norm': 'delta_w', 'delta_xattn_wq': 'delta_w', 'delta_xattn_wk': 'delta_w', 'delta_xattn_wv': 'delta_w', 'delta_xattn_wo': 'delta_w', 'delta_ffn_pre_norm': 'delta_w', 'delta_ffn_post_norm': 'delta_w', 'delta_ffn_w_up': 'delta_w', 'delta_ffn_conv_w': 'delta_w', 'delta_ffn_w_down': 'delta_w', 'new_m_mem_norm': 'new_m', 'new_m_mix_pre_norm': 'new_m', 'new_m_mix_post_norm': 'new_m', 'new_m_w_in': 'new_m', 'new_m_pool_maps': 'new_m', 'new_m_pool_scale': 'new_m', 'new_m_conf_dw_w': 'new_m', 'new_m_conf_dw_b': 'new_m', 'new_m_conf_ln_g': 'new_m', 'new_m_conf_ln_b': 'new_m', 'new_m_sconv_w': 'new_m', 'new_m_w_out': 'new_m', 'new_m_xattn_pre_norm': 'new_m', 'new_m_xattn_post_norm': 'new_m', 'new_m_xattn_wq': 'new_m', 'new_m_xattn_wk': 'new_m', 'new_m_xattn_wv': 'new_m', 'new_m_xattn_wo': 'new_m', 'new_m_ffn_pre_norm': 'new_m', 'new_m_ffn_post_norm': 'new_m', 'new_m_ffn_w_up': 'new_m', 'new_m_ffn_conv_w': 'new_m', 'new_m_ffn_w_down': 'new_m', 'new_v_mem_norm': 'new_v', 'new_v_mix_pre_norm': 'new_v', 'new_v_mix_post_norm': 'new_v', 'new_v_w_in': 'new_v', 'new_v_pool_maps': 'new_v', 'new_v_pool_scale': 'new_v', 'new_v_conf_dw_w': 'new_v', 'new_v_conf_dw_b': 'new_v', 'new_v_conf_ln_g': 'new_v', 'new_v_conf_ln_b': 'new_v', 'new_v_sconv_w': 'new_v', 'new_v_w_out': 'new_v', 'new_v_xattn_pre_norm': 'new_v', 'new_v_xattn_post_norm': 'new_v', 'new_v_xattn_wq': 'new_v', 'new_v_xattn_wk': 'new_v', 'new_v_xattn_wv': 'new_v', 'new_v_xattn_wo': 'new_v', 'new_v_ffn_pre_norm': 'new_v', 'new_v_ffn_post_norm': 'new_v', 'new_v_ffn_w_up': 'new_v', 'new_v_ffn_conv_w': 'new_v', 'new_v_ffn_w_down': 'new_v'}


def _forward(args):
    return _fwd_reference(*[args[k] for k in FWD_PARAMS])


def _output_shape():
    def fwd():
        inp = _fwd_setup_inputs(0)
        return _fwd_reference(*[inp[k] for k in FWD_PARAMS])
    out = _jax.eval_shape(fwd)
    return out.shape, out.dtype

N_MICROBATCH = 1
ADAM_LR = 0.001
ADAM_B1 = 0.9
ADAM_B2 = 0.999
ADAM_EPS = 1e-08
ADAM_WD = 0.01
ADAM_STEP = 10
PER_EXAMPLE_BATCH_AXIS = {'x': 0, 'mem': 0, 'loss_target': 0}
SHARED_INPUTS = []
_WEIGHT_DTYPES = {'mem_norm': _jnp.float32, 'mix_pre_norm': _jnp.float32, 'mix_post_norm': _jnp.float32, 'w_in': _jnp.float32, 'pool_maps': _jnp.float32, 'pool_scale': _jnp.float32, 'conf_dw_w': _jnp.float32, 'conf_dw_b': _jnp.float32, 'conf_ln_g': _jnp.float32, 'conf_ln_b': _jnp.float32, 'sconv_w': _jnp.float32, 'w_out': _jnp.float32, 'xattn_pre_norm': _jnp.float32, 'xattn_post_norm': _jnp.float32, 'xattn_wq': _jnp.float32, 'xattn_wk': _jnp.float32, 'xattn_wv': _jnp.float32, 'xattn_wo': _jnp.float32, 'ffn_pre_norm': _jnp.float32, 'ffn_post_norm': _jnp.float32, 'ffn_w_up': _jnp.float32, 'ffn_conv_w': _jnp.float32, 'ffn_w_down': _jnp.float32}
MOMENT_SCALE = {'mem_norm': 3.622759e+01, 'mix_pre_norm': 4.319395e+00, 'mix_post_norm': 6.354227e+01, 'w_in': 2.997994e+00, 'pool_maps': 3.915906e+00, 'pool_scale': 4.064516e+00, 'conf_dw_w': 4.848691e+00, 'conf_dw_b': 3.675924e+01, 'conf_ln_g': 1.642849e+01, 'conf_ln_b': 2.329390e+01, 'sconv_w': 3.007367e+00, 'w_out': 5.821469e+00, 'xattn_pre_norm': 4.498125e+00, 'xattn_post_norm': 6.685732e+01, 'xattn_wq': 4.579075e+00, 'xattn_wk': 4.662170e+00, 'xattn_wv': 1.746757e+01, 'xattn_wo': 1.761472e+01, 'ffn_pre_norm': 5.609537e+00, 'ffn_post_norm': 6.415316e+01, 'ffn_w_up': 2.287079e+00, 'ffn_conv_w': 2.623628e+00, 'ffn_w_down': 4.731051e+00}


def _to_microbatches(a, axis):
    t = _jnp.moveaxis(a, axis, 0)
    t = t.reshape((N_MICROBATCH, t.shape[0] // N_MICROBATCH) + t.shape[1:])
    return _jnp.moveaxis(t, 1, axis + 1)


def setup_inputs(seed: int = 0) -> dict:
    inp = _fwd_setup_inputs(seed)
    key = _jax.random.fold_in(_jax.random.key(seed), 7919)
    shape, _ = _output_shape()
    out = dict(inp)
    out["loss_target"] = _jax.random.normal(_jax.random.fold_in(key, 0), shape, _jnp.float32)
    for i, name in enumerate(TWIN_WEIGHTS):
        w = inp[name].astype(_jnp.float32)
        if MOMENT_SCALE is None:
            s = _jnp.sqrt(_jnp.mean(_jnp.square(w)) + 1e-30)
        else:
            s = MOMENT_SCALE[name]
        km, kv = _jax.random.split(_jax.random.fold_in(key, i + 1))
        out[name] = w
        out["m_" + name] = s * _jax.random.normal(km, w.shape, _jnp.float32)
        out["v_" + name] = (s * s) * _jax.random.uniform(kv, w.shape, _jnp.float32, 0.5, 1.5)
    if N_MICROBATCH > 1:
        for name, axis in PER_EXAMPLE_BATCH_AXIS.items():
            out[name] = _to_microbatches(out[name], axis)
    return {'x': out['x'], 'mem': out['mem'], 'mem_norm': out['mem_norm'], 'mix_pre_norm': out['mix_pre_norm'], 'mix_post_norm': out['mix_post_norm'], 'w_in': out['w_in'], 'pool_maps': out['pool_maps'], 'pool_scale': out['pool_scale'], 'conf_dw_w': out['conf_dw_w'], 'conf_dw_b': out['conf_dw_b'], 'conf_ln_g': out['conf_ln_g'], 'conf_ln_b': out['conf_ln_b'], 'sconv_w': out['sconv_w'], 'w_out': out['w_out'], 'xattn_pre_norm': out['xattn_pre_norm'], 'xattn_post_norm': out['xattn_post_norm'], 'xattn_wq': out['xattn_wq'], 'xattn_wk': out['xattn_wk'], 'xattn_wv': out['xattn_wv'], 'xattn_wo': out['xattn_wo'], 'ffn_pre_norm': out['ffn_pre_norm'], 'ffn_post_norm': out['ffn_post_norm'], 'ffn_w_up': out['ffn_w_up'], 'ffn_conv_w': out['ffn_conv_w'], 'ffn_w_down': out['ffn_w_down'], 'loss_target': out['loss_target'], 'm_mem_norm': out['m_mem_norm'], 'm_mix_pre_norm': out['m_mix_pre_norm'], 'm_mix_post_norm': out['m_mix_post_norm'], 'm_w_in': out['m_w_in'], 'm_pool_maps': out['m_pool_maps'], 'm_pool_scale': out['m_pool_scale'], 'm_conf_dw_w': out['m_conf_dw_w'], 'm_conf_dw_b': out['m_conf_dw_b'], 'm_conf_ln_g': out['m_conf_ln_g'], 'm_conf_ln_b': out['m_conf_ln_b'], 'm_sconv_w': out['m_sconv_w'], 'm_w_out': out['m_w_out'], 'm_xattn_pre_norm': out['m_xattn_pre_norm'], 'm_xattn_post_norm': out['m_xattn_post_norm'], 'm_xattn_wq': out['m_xattn_wq'], 'm_xattn_wk': out['m_xattn_wk'], 'm_xattn_wv': out['m_xattn_wv'], 'm_xattn_wo': out['m_xattn_wo'], 'm_ffn_pre_norm': out['m_ffn_pre_norm'], 'm_ffn_post_norm': out['m_ffn_post_norm'], 'm_ffn_w_up': out['m_ffn_w_up'], 'm_ffn_conv_w': out['m_ffn_conv_w'], 'm_ffn_w_down': out['m_ffn_w_down'], 'v_mem_norm': out['v_mem_norm'], 'v_mix_pre_norm': out['v_mix_pre_norm'], 'v_mix_post_norm': out['v_mix_post_norm'], 'v_w_in': out['v_w_in'], 'v_pool_maps': out['v_pool_maps'], 'v_pool_scale': out['v_pool_scale'], 'v_conf_dw_w': out['v_conf_dw_w'], 'v_conf_dw_b': out['v_conf_dw_b'], 'v_conf_ln_g': out['v_conf_ln_g'], 'v_conf_ln_b': out['v_conf_ln_b'], 'v_sconv_w': out['v_sconv_w'], 'v_w_out': out['v_w_out'], 'v_xattn_pre_norm': out['v_xattn_pre_norm'], 'v_xattn_post_norm': out['v_xattn_post_norm'], 'v_xattn_wq': out['v_xattn_wq'], 'v_xattn_wk': out['v_xattn_wk'], 'v_xattn_wv': out['v_xattn_wv'], 'v_xattn_wo': out['v_xattn_wo'], 'v_ffn_pre_norm': out['v_ffn_pre_norm'], 'v_ffn_post_norm': out['v_ffn_post_norm'], 'v_ffn_w_up': out['v_ffn_w_up'], 'v_ffn_conv_w': out['v_ffn_conv_w'], 'v_ffn_w_down': out['v_ffn_w_down']}


def _loss(weights, diff, rest, loss_target):
    with _jax.named_scope("forward"):
        args = {**rest, TWIN_DIFF_INPUT: diff, **{k: w.astype(_WEIGHT_DTYPES[k]) for k, w in weights.items()}}
        y = _forward(args)
    with _jax.named_scope("loss_head"):
        err = _jnp.square(y.astype(_jnp.float32) - loss_target)
        return 0.5 * _jnp.sum(_jnp.mean(err, axis=-1)) if err.ndim else 0.5 * err


def _adamw(w, g, m, v):
    m = ADAM_B1 * m + (1.0 - ADAM_B1) * g
    v = ADAM_B2 * v + (1.0 - ADAM_B2) * _jnp.square(g)
    m_hat = m / (1.0 - ADAM_B1 ** ADAM_STEP)
    v_hat = v / (1.0 - ADAM_B2 ** ADAM_STEP)
    delta = -ADAM_LR * (m_hat / (_jnp.sqrt(v_hat) + ADAM_EPS) + ADAM_WD * w)
    return delta, m, v


def reference(x, mem, mem_norm, mix_pre_norm, mix_post_norm, w_in, pool_maps, pool_scale, conf_dw_w, conf_dw_b, conf_ln_g, conf_ln_b, sconv_w, w_out, xattn_pre_norm, xattn_post_norm, xattn_wq, xattn_wk, xattn_wv, xattn_wo, ffn_pre_norm, ffn_post_norm, ffn_w_up, ffn_conv_w, ffn_w_down, loss_target, m_mem_norm, m_mix_pre_norm, m_mix_post_norm, m_w_in, m_pool_maps, m_pool_scale, m_conf_dw_w, m_conf_dw_b, m_conf_ln_g, m_conf_ln_b, m_sconv_w, m_w_out, m_xattn_pre_norm, m_xattn_post_norm, m_xattn_wq, m_xattn_wk, m_xattn_wv, m_xattn_wo, m_ffn_pre_norm, m_ffn_post_norm, m_ffn_w_up, m_ffn_conv_w, m_ffn_w_down, v_mem_norm, v_mix_pre_norm, v_mix_post_norm, v_w_in, v_pool_maps, v_pool_scale, v_conf_dw_w, v_conf_dw_b, v_conf_ln_g, v_conf_ln_b, v_sconv_w, v_w_out, v_xattn_pre_norm, v_xattn_post_norm, v_xattn_wq, v_xattn_wk, v_xattn_wv, v_xattn_wo, v_ffn_pre_norm, v_ffn_post_norm, v_ffn_w_up, v_ffn_conv_w, v_ffn_w_down):
    given = dict(x=x, mem=mem, mem_norm=mem_norm, mix_pre_norm=mix_pre_norm, mix_post_norm=mix_post_norm, w_in=w_in, pool_maps=pool_maps, pool_scale=pool_scale, conf_dw_w=conf_dw_w, conf_dw_b=conf_dw_b, conf_ln_g=conf_ln_g, conf_ln_b=conf_ln_b, sconv_w=sconv_w, w_out=w_out, xattn_pre_norm=xattn_pre_norm, xattn_post_norm=xattn_post_norm, xattn_wq=xattn_wq, xattn_wk=xattn_wk, xattn_wv=xattn_wv, xattn_wo=xattn_wo, ffn_pre_norm=ffn_pre_norm, ffn_post_norm=ffn_post_norm, ffn_w_up=ffn_w_up, ffn_conv_w=ffn_conv_w, ffn_w_down=ffn_w_down, loss_target=loss_target, m_mem_norm=m_mem_norm, m_mix_pre_norm=m_mix_pre_norm, m_mix_post_norm=m_mix_post_norm, m_w_in=m_w_in, m_pool_maps=m_pool_maps, m_pool_scale=m_pool_scale, m_conf_dw_w=m_conf_dw_w, m_conf_dw_b=m_conf_dw_b, m_conf_ln_g=m_conf_ln_g, m_conf_ln_b=m_conf_ln_b, m_sconv_w=m_sconv_w, m_w_out=m_w_out, m_xattn_pre_norm=m_xattn_pre_norm, m_xattn_post_norm=m_xattn_post_norm, m_xattn_wq=m_xattn_wq, m_xattn_wk=m_xattn_wk, m_xattn_wv=m_xattn_wv, m_xattn_wo=m_xattn_wo, m_ffn_pre_norm=m_ffn_pre_norm, m_ffn_post_norm=m_ffn_post_norm, m_ffn_w_up=m_ffn_w_up, m_ffn_conv_w=m_ffn_conv_w, m_ffn_w_down=m_ffn_w_down, v_mem_norm=v_mem_norm, v_mix_pre_norm=v_mix_pre_norm, v_mix_post_norm=v_mix_post_norm, v_w_in=v_w_in, v_pool_maps=v_pool_maps, v_pool_scale=v_pool_scale, v_conf_dw_w=v_conf_dw_w, v_conf_dw_b=v_conf_dw_b, v_conf_ln_g=v_conf_ln_g, v_conf_ln_b=v_conf_ln_b, v_sconv_w=v_sconv_w, v_w_out=v_w_out, v_xattn_pre_norm=v_xattn_pre_norm, v_xattn_post_norm=v_xattn_post_norm, v_xattn_wq=v_xattn_wq, v_xattn_wk=v_xattn_wk, v_xattn_wv=v_xattn_wv, v_xattn_wo=v_xattn_wo, v_ffn_pre_norm=v_ffn_pre_norm, v_ffn_post_norm=v_ffn_post_norm, v_ffn_w_up=v_ffn_w_up, v_ffn_conv_w=v_ffn_conv_w, v_ffn_w_down=v_ffn_w_down)
    weights = {n: given[n] for n in TWIN_WEIGHTS}
    shared = {n: given[n] for n in SHARED_INPUTS}
    per_example = {n: given[n] for n in ['x', 'mem']}
    grad_fn = _jax.value_and_grad(_loss, argnums=(0, 1))

    def one_microbatch(ex, loss_target):
        ex = dict(ex)
        diff = ex.pop(TWIN_DIFF_INPUT)
        return grad_fn(weights, diff, {**shared, **ex}, loss_target)

    if N_MICROBATCH == 1:
        loss, (grad_w, grad_x) = one_microbatch(per_example, given["loss_target"])
    else:
        def body(carry, xs):
            loss_sum, grad_sum = carry
            l_k, (gw_k, gx_k) = one_microbatch(xs[0], xs[1])
            with _jax.named_scope("update"):
                return (loss_sum + l_k, _jax.tree.map(_jnp.add, grad_sum, gw_k)), gx_k

        init = (_jnp.zeros((), _jnp.float32), _jax.tree.map(_jnp.zeros_like, weights))
        (loss, grad_w), grad_x = _jax.lax.scan(body, init, (per_example, given["loss_target"]))
    with _jax.named_scope("update"):
        delta_w, new_m, new_v = {}, {}, {}
        for n in TWIN_WEIGHTS:
            delta_w[n], new_m[n], new_v[n] = _adamw(weights[n], grad_w[n], given["m_" + n], given["v_" + n])
    return (loss, grad_x, *[grad_w[n] for n in TWIN_WEIGHTS], *[delta_w[n] for n in TWIN_WEIGHTS],
            *[new_m[n] for n in TWIN_WEIGHTS], *[new_v[n] for n in TWIN_WEIGHTS])
```

```python
import functools

import jax
import jax.numpy as jnp
from jax import lax
from jax.experimental import pallas as pl
from jax.experimental.pallas import tpu as pltpu

F32 = jnp.float32
BF16 = jnp.bfloat16

EPS = 1e-6
D = 1024
D_POOL, D_CONF, D_SCONV = 256, 384, 384
D_IN = D_POOL + 2 * D_CONF + 3 * D_SCONV
D_FF = 2816
FF_CHUNK = 1408
HEADS, HEAD_DIM = 4, 256
CONF_K = 31
POOL_WINDOWS = (2, 4, 8, 16)
POOL_GROUP = 64
HALO = 32
FHALO = 16
TM_FWD = 512
TM_BWD = 256
N_CHIPS = 4
N_DEV = 8
MESH_ID = pl.DeviceIdType.MESH
VMEM_LIMIT = 56 << 20

ADAM_LR, ADAM_B1, ADAM_B2, ADAM_EPS, ADAM_WD, ADAM_STEP = 0.001, 0.9, 0.999, 1e-08, 0.01, 10

C_P = (0, 256)
C_A = (256, 640)
C_G = (640, 1024)
C_B = (1024, 1408)
C_C = (1408, 1792)
C_X = (1792, 2176)


def _nn(a, b):
    return jnp.dot(a, b, preferred_element_type=F32)


def _nt(a, b):
    return lax.dot_general(a, b, (((1,), (1,)), ((), ())), preferred_element_type=F32)


def _tn(a, b):
    return lax.dot_general(a, b, (((0,), (0,)), ((), ())), preferred_element_type=F32)


def _sigmoid(v):
    return 1.0 / (1.0 + jnp.exp(-v))


def _rms(v, g):
    r = lax.rsqrt(jnp.mean(v * v, axis=-1, keepdims=True) + EPS)
    vh = v * r
    return vh * g, vh, r


def _rms_bwd(vh, r, g, dy):
    dvh = dy * g
    dv = r * (dvh - vh * jnp.mean(dvh * vh, axis=-1, keepdims=True))
    return dv, jnp.sum(dy * vh, axis=0, keepdims=True)


def _params(sem):
    return pltpu.CompilerParams(dimension_semantics=sem, vmem_limit_bytes=VMEM_LIMIT)


def _rows(tm, n, nt=None):
    if nt is None:
        return pl.BlockSpec((tm, n), lambda i: (i, 0))
    return pl.BlockSpec((tm, n), lambda i: (nt - 1 - i, 0))


def _const(shape):
    nd = len(shape)
    return pl.BlockSpec(shape, lambda i: (0,) * nd, pipeline_mode=pl.Buffered(1))


def _acc(shape):
    nd = len(shape)
    return pl.BlockSpec(shape, lambda i: (0,) * nd)


def _sds(shape, dtype):
    return jax.ShapeDtypeStruct(shape, dtype)


def _pool_lane():
    return lax.broadcasted_iota(jnp.int32, (1, D_POOL), 1)


def _pool_sums(buf, base, tm, sign):
    lane = _pool_lane()
    acc = buf[base:base + tm, :]
    res = None
    for k in range(1, POOL_WINDOWS[-1]):
        off = base + sign * k
        acc = acc + buf[off:off + tm, :]
        if (k + 1) in POOL_WINDOWS:
            gi = POOL_WINDOWS.index(k + 1)
            res = acc if res is None else jnp.where(lane >= POOL_GROUP * gi, acc, res)
    return res


def _pool_count(t0, tm):
    lane = _pool_lane()
    w = jnp.where(lane < 64, 2, jnp.where(lane < 128, 4, jnp.where(lane < 192, 8, 16)))
    pos1 = lax.broadcasted_iota(jnp.int32, (tm, D_POOL), 0) + (t0 + 1)
    return jnp.minimum(pos1, w).astype(F32)


def _conf_conv(vbuf, dww_ref, dwb_ref, tm):
    base = HALO - (CONF_K - 1)
    acc = dww_ref[0:1, :] * vbuf[base:base + tm, :]
    for j in range(1, CONF_K):
        acc = acc + dww_ref[j:j + 1, :] * vbuf[base + j:base + j + tm, :]
    return acc + dwb_ref[...]


def _sconv(sbuf, sw_ref, tm):
    base = HALO - 2
    return (sw_ref[0:1, :] * sbuf[base:base + tm, :] + sw_ref[1:2, :] * sbuf[base + 1:base + 1 + tm, :]
            + sw_ref[2:3, :] * sbuf[base + 2:base + 2 + tm, :])


def _layer_norm_stats(v1):
    mu = jnp.mean(v1, axis=-1, keepdims=True)
    xc = v1 - mu
    rs = lax.rsqrt(jnp.mean(xc * xc, axis=-1, keepdims=True) + EPS)
    return xc * rs, rs


def _mix_fwd(x, g1, wint, maps_bd, scale, dww, dwb, lng, lnb, sw, wout, g2):
    t_len = x.shape[0]
    tm = min(TM_FWD, t_len)
    nt = t_len // tm
    h0 = HALO

    def body(x_ref, g1_ref, wint_ref, maps_ref, scale_ref, dww_ref, dwb_ref, lng_ref, lnb_ref, sw_ref, wout_ref,
             g2_ref, z_ref, ycat_ref, y_ref, x1_ref, pbuf, vbuf, sbuf):
        i = pl.program_id(0)

        @pl.when(i == 0)
        def _():
            pbuf[0:h0, :] = jnp.zeros((h0, D_POOL), F32)
            vbuf[0:h0, :] = jnp.zeros((h0, D_CONF), F32)
            sbuf[0:h0, :] = jnp.zeros((h0, D_SCONV), F32)

        @pl.when(i > 0)
        def _():
            pbuf[0:h0, :] = pbuf[tm:tm + h0, :]
            vbuf[0:h0, :] = vbuf[tm:tm + h0, :]
            sbuf[0:h0, :] = sbuf[tm:tm + h0, :]

        xv = x_ref[...]
        h, _, _ = _rms(xv, g1_ref[...])
        z = _nt(h.astype(BF16), wint_ref[...])
        z_ref[...] = z.astype(BF16)
        zp = z[:, C_P[0]:C_P[1]]
        pbuf[h0:h0 + tm, :] = zp
        vbuf[h0:h0 + tm, :] = z[:, C_A[0]:C_A[1]] * _sigmoid(z[:, C_G[0]:C_G[1]])
        sbuf[h0:h0 + tm, :] = z[:, C_C[0]:C_C[1]] * z[:, C_X[0]:C_X[1]]

        pooled = _pool_sums(pbuf, h0, tm, -1) / _pool_count(i * tm, tm) - zp
        ya = _nn(pooled.astype(BF16), maps_ref[...]) * scale_ref[...]
        ycat_ref[:, 0:D_POOL] = ya.astype(BF16)

        v1 = _conf_conv(vbuf, dww_ref, dwb_ref, tm)
        vh, _ = _layer_norm_stats(v1)
        v2 = vh * lng_ref[...] + lnb_ref[...]
        ycat_ref[:, D_POOL:D_POOL + D_CONF] = (v2 * _sigmoid(v2)).astype(BF16)

        yc = z[:, C_B[0]:C_B[1]] * _sconv(sbuf, sw_ref, tm)
        ycat_ref[:, D_POOL + D_CONF:D] = yc.astype(BF16)

        y = _nn(ycat_ref[...], wout_ref[...])
        yb = y.astype(BF16)
        y_ref[...] = yb
        yn, _, _ = _rms(yb.astype(F32), g2_ref[...])
        x1_ref[...] = xv + yn

    return pl.pallas_call(
        body, grid=(nt,), name="mix_fwd",
        in_specs=[_rows(tm, D), _const((1, D)), _const((D_IN, D)), _const((D_POOL, D_POOL)), _const((1, D_POOL)),
                  _const((CONF_K, D_CONF)), _const((1, D_CONF)), _const((1, D_CONF)), _const((1, D_CONF)),
                  _const((3, D_SCONV)), _const((D, D)), _const((1, D))],
        out_specs=[_rows(tm, D_IN), _rows(tm, D), _rows(tm, D), _rows(tm, D)],
        out_shape=[_sds((t_len, D_IN), BF16), _sds((t_len, D), BF16), _sds((t_len, D), BF16), _sds((t_len, D), F32)],
        scratch_shapes=[pltpu.VMEM((h0 + tm, D_POOL), F32), pltpu.VMEM((h0 + tm, D_CONF), F32),
                        pltpu.VMEM((h0 + tm, D_SCONV), F32)],
        compiler_params=_params(("arbitrary",)),
    )(x, g1, wint, maps_bd, scale, dww, dwb, lng, lnb, sw, wout, g2)


def _mix_bwd(dx1, y, x, z, ycat, g1, wint, maps_bd, scale, dww, dwb, lng, lnb, sw, wout, g2):
    t_len = x.shape[0]
    tm = min(TM_BWD, t_len)
    nt = t_len // tm
    h0 = HALO
    hb = tm // h0

    def body(dx1_ref, y_ref, x_ref, z_ref, zh_ref, ycat_ref, g1_ref, wint_ref, maps_ref, scale_ref, dww_ref,
             dwb_ref, lng_ref, lnb_ref, sw_ref, wout_ref, g2_ref,
             dx_ref, dwin_ref, dwout_ref, dg1_ref, dg2_ref, dmaps_ref, dscale_ref, ddww_ref, misc_ref,
             pbuf, vbuf, sbuf, ebuf, dvbuf, dcbuf, dzbuf, acc_in, acc_out):
        s = pl.program_id(0)
        ti = nt - 1 - s

        @pl.when(s == 0)
        def _():
            ebuf[tm:tm + h0, :] = jnp.zeros((h0, D_POOL), F32)
            dvbuf[tm:tm + h0, :] = jnp.zeros((h0, D_CONF), F32)
            dcbuf[tm:tm + h0, :] = jnp.zeros((h0, D_SCONV), F32)
            acc_in[...] = jnp.zeros_like(acc_in)
            acc_out[...] = jnp.zeros_like(acc_out)
            dg1_ref[...] = jnp.zeros_like(dg1_ref)
            dg2_ref[...] = jnp.zeros_like(dg2_ref)
            dmaps_ref[...] = jnp.zeros_like(dmaps_ref)
            dscale_ref[...] = jnp.zeros_like(dscale_ref)
            ddww_ref[...] = jnp.zeros_like(ddww_ref)
            misc_ref[...] = jnp.zeros_like(misc_ref)

        @pl.when(s > 0)
        def _():
            ebuf[tm:tm + h0, :] = ebuf[0:h0, :]
            dvbuf[tm:tm + h0, :] = dvbuf[0:h0, :]
            dcbuf[tm:tm + h0, :] = dcbuf[0:h0, :]

        zt = z_ref[...].astype(F32)
        zh = jnp.where(ti > 0, zh_ref[...].astype(F32), 0.0)
        zp, za, zg = zt[:, C_P[0]:C_P[1]], zt[:, C_A[0]:C_A[1]], zt[:, C_G[0]:C_G[1]]
        zb, zc, zx = zt[:, C_B[0]:C_B[1]], zt[:, C_C[0]:C_C[1]], zt[:, C_X[0]:C_X[1]]
        sg = _sigmoid(zg)
        pbuf[0:h0, :] = zh[:, C_P[0]:C_P[1]]
        pbuf[h0:h0 + tm, :] = zp
        vbuf[0:h0, :] = zh[:, C_A[0]:C_A[1]] * _sigmoid(zh[:, C_G[0]:C_G[1]])
        vbuf[h0:h0 + tm, :] = za * sg
        sbuf[0:h0, :] = zh[:, C_C[0]:C_C[1]] * zh[:, C_X[0]:C_X[1]]
        sbuf[h0:h0 + tm, :] = zc * zx

        yv = y_ref[...].astype(F32)
        _, yh, yr = _rms(yv, g2_ref[...])
        dy, dg2 = _rms_bwd(yh, yr, g2_ref[...], dx1_ref[...])
        dg2_ref[...] += dg2
        dyb = dy.astype(BF16)
        acc_out[...] += _tn(ycat_ref[...], dyb)
        dycat = _nt(dyb, wout_ref[...])
        dya = dycat[:, 0:D_POOL]
        dyb2 = dycat[:, D_POOL:D_POOL + D_CONF]
        dyc = dycat[:, D_POOL + D_CONF:D]

        cnt = _pool_count(ti * tm, tm)
        pooled = (_pool_sums(pbuf, h0, tm, -1) / cnt - zp).astype(BF16)
        pm = _nn(pooled, maps_ref[...])
        dscale_ref[...] += jnp.sum(dya * pm, axis=0, keepdims=True)
        dq = (dya * scale_ref[...]).astype(BF16)
        dmaps_ref[...] += _tn(pooled, dq)
        dpooled = _nt(dq, maps_ref[...])
        ebuf[0:tm, :] = dpooled / cnt
        dzbuf[:, C_P[0]:C_P[1]] = (_pool_sums(ebuf, 0, tm, 1) - dpooled).astype(BF16)

        v1 = _conf_conv(vbuf, dww_ref, dwb_ref, tm)
        vh, rs = _layer_norm_stats(v1)
        v2 = vh * lng_ref[...] + lnb_ref[...]
        s2 = _sigmoid(v2)
        dv2 = dyb2 * (s2 * (1.0 + v2 * (1.0 - s2)))
        misc_ref[1:2, :] += jnp.sum(dv2 * vh, axis=0, keepdims=True)
        misc_ref[2:3, :] += jnp.sum(dv2, axis=0, keepdims=True)
        dvh = dv2 * lng_ref[...]
        dv1 = rs * (dvh - jnp.mean(dvh, axis=-1, keepdims=True) - vh * jnp.mean(dvh * vh, axis=-1, keepdims=True))
        misc_ref[0:1, :] += jnp.sum(dv1, axis=0, keepdims=True)
        dvbuf[0:tm, :] = dv1
        base = h0 - (CONF_K - 1)
        dv0 = None
        for j in range(CONF_K):
            ddww_ref[j:j + 1, :] += jnp.sum(dv1 * vbuf[base + j:base + j + tm, :], axis=0, keepdims=True)
            term = dww_ref[j:j + 1, :] * dvbuf[CONF_K - 1 - j:CONF_K - 1 - j + tm, :]
            dv0 = term if dv0 is None else dv0 + term
        dzbuf[:, C_A[0]:C_A[1]] = (dv0 * sg).astype(BF16)
        dzbuf[:, C_G[0]:C_G[1]] = (dv0 * za * sg * (1.0 - sg)).astype(BF16)

        cv = _sconv(sbuf, sw_ref, tm)
        dzbuf[:, C_B[0]:C_B[1]] = (dyc * cv).astype(BF16)
        dcv = dyc * zb
        dcbuf[0:tm, :] = dcv
        dp = None
        for j in range(3):
            misc_ref[3 + j:4 + j, :] += jnp.sum(dcv * sbuf[h0 - 2 + j:h0 - 2 + j + tm, :], axis=0, keepdims=True)
            term = sw_ref[j:j + 1, :] * dcbuf[2 - j:2 - j + tm, :]
            dp = term if dp is None else dp + term
        dzbuf[:, C_C[0]:C_C[1]] = (dp * zx).astype(BF16)
        dzbuf[:, C_X[0]:C_X[1]] = (dp * zc).astype(BF16)

        xv = x_ref[...]
        h, xh, xr = _rms(xv, g1_ref[...])
        dz = dzbuf[...]
        acc_in[...] += _tn(dz, h.astype(BF16))
        dh = _nn(dz, wint_ref[...])
        dxn, dg1 = _rms_bwd(xh, xr, g1_ref[...], dh)
        dg1_ref[...] += dg1
        dx_ref[...] = dx1_ref[...] + dxn

        @pl.when(s == nt - 1)
        def _():
            dwin_ref[...] = acc_in[...].astype(BF16)
            dwout_ref[...] = acc_out[...].astype(BF16)

    zh_spec = pl.BlockSpec((h0, D_IN), lambda i: (jnp.maximum((nt - 1 - i) * hb - 1, 0), 0))
    return pl.pallas_call(
        body, grid=(nt,), name="mix_bwd",
        in_specs=[_rows(tm, D, nt), _rows(tm, D, nt), _rows(tm, D, nt), _rows(tm, D_IN, nt), zh_spec,
                  _rows(tm, D, nt), _const((1, D)), _const((D_IN, D)), _const((D_POOL, D_POOL)),
                  _const((1, D_POOL)), _const((CONF_K, D_CONF)), _const((1, D_CONF)), _const((1, D_CONF)),
                  _const((1, D_CONF)), _const((3, D_SCONV)), _const((D, D)), _const((1, D))],
        out_specs=[_rows(tm, D, nt), _acc((D_IN, D)), _acc((D, D)), _acc((1, D)), _acc((1, D)),
                   _acc((D_POOL, D_POOL)), _acc((1, D_POOL)), _acc((32, D_CONF)), _acc((8, D_CONF))],
        out_shape=[_sds((t_len, D), F32), _sds((D_IN, D), BF16), _sds((D, D), BF16), _sds((1, D), F32),
                   _sds((1, D), F32), _sds((D_POOL, D_POOL), F32), _sds((1, D_POOL), F32), _sds((32, D_CONF), F32),
                   _sds((8, D_CONF), F32)],
        scratch_shapes=[pltpu.VMEM((h0 + tm, D_POOL), F32), pltpu.VMEM((h0 + tm, D_CONF), F32),
                        pltpu.VMEM((h0 + tm, D_SCONV), F32), pltpu.VMEM((tm + h0, D_POOL), F32),
                        pltpu.VMEM((tm + h0, D_CONF), F32), pltpu.VMEM((tm + h0, D_SCONV), F32),
                        pltpu.VMEM((tm, D_IN), BF16), pltpu.VMEM((D_IN, D), F32), pltpu.VMEM((D, D), F32)],
        compiler_params=_params(("arbitrary",)),
    )(dx1, y, x, z, z, ycat, g1, wint, maps_bd, scale, dww, dwb, lng, lnb, sw, wout, g2)


def _kv_fwd(mem, gmem, wk, wv):
    def body(mem_ref, g_ref, wk_ref, wv_ref, k_ref, v_ref):
        mn, _, _ = _rms(mem_ref[...], g_ref[...])
        mnb = mn.astype(BF16)
        k_ref[...] = _nn(mnb, wk_ref[...]).astype(BF16)
        v_ref[...] = _nn(mnb, wv_ref[...]).astype(BF16)

    n = mem.shape[0]
    return pl.pallas_call(
        body, name="kv_fwd", out_shape=[_sds((n, D), BF16), _sds((n, D), BF16)],
        compiler_params=pltpu.CompilerParams(vmem_limit_bytes=VMEM_LIMIT),
    )(mem, gmem, wk, wv)


def _kv_bwd(mem, gmem, dk, dv, wk, wv):
    def body(mem_ref, g_ref, dk_ref, dv_ref, wk_ref, wv_ref, dwk_ref, dwv_ref, dg_ref):
        mn, mh, _ = _rms(mem_ref[...], g_ref[...])
        mnb = mn.astype(BF16)
        dkb = dk_ref[...].astype(BF16)
        dvb = dv_ref[...].astype(BF16)
        dwk_ref[...] = _tn(mnb, dkb).astype(BF16)
        dwv_ref[...] = _tn(mnb, dvb).astype(BF16)
        dmn = _nt(dkb, wk_ref[...]) + _nt(dvb, wv_ref[...])
        dg_ref[...] = jnp.sum(dmn * mh, axis=0, keepdims=True)

    return pl.pallas_call(
        body, name="kv_bwd", out_shape=[_sds((D, D), BF16), _sds((D, D), BF16), _sds((1, D), F32)],
        compiler_params=pltpu.CompilerParams(vmem_limit_bytes=VMEM_LIMIT),
    )(mem, gmem, dk, dv, wk, wv)


def _softmax_rows(s):
    e = jnp.exp(s - jnp.max(s, axis=-1, keepdims=True))
    return e / jnp.sum(e, axis=-1, keepdims=True)


def _xattn_fwd(x1, g3, wq, k, v, wo, g4):
    t_len = x1.shape[0]
    tm = min(TM_FWD, t_len)
    nt = t_len // tm
    n_mem = k.shape[0]
    sc = HEAD_DIM ** -0.5

    def body(x_ref, g3_ref, wq_ref, k_ref, v_ref, wo_ref, g4_ref, q_ref, o_ref, y_ref, x2_ref):
        xv = x_ref[...]
        h, _, _ = _rms(xv, g3_ref[...])
        qb = _nn(h.astype(BF16), wq_ref[...]).astype(BF16)
        q_ref[...] = qb
        for hd in range(HEADS):
            sl = slice(hd * HEAD_DIM, (hd + 1) * HEAD_DIM)
            p = _softmax_rows(_nt(qb[:, sl], k_ref[:, sl]) * sc)
            o_ref[:, sl] = _nn(p.astype(BF16), v_ref[:, sl]).astype(BF16)
        yb = _nn(o_ref[...], wo_ref[...]).astype(BF16)
        y_ref[...] = yb
        yn, _, _ = _rms(yb.astype(F32), g4_ref[...])
        x2_ref[...] = xv + yn

    return pl.pallas_call(
        body, grid=(nt,), name="xattn_fwd",
        in_specs=[_rows(tm, D), _const((1, D)), _const((D, D)), _const((n_mem, D)), _const((n_mem, D)),
                  _const((D, D)), _const((1, D))],
        out_specs=[_rows(tm, D), _rows(tm, D), _rows(tm, D), _rows(tm, D)],
        out_shape=[_sds((t_len, D), BF16), _sds((t_len, D), BF16), _sds((t_len, D), BF16), _sds((t_len, D), F32)],
        compiler_params=_params(("arbitrary",)),
    )(x1, g3, wq, k, v, wo, g4)


def _xattn_bwd(dx2, y, x1, q, o, g3, wq, k, v, wo, g4):
    t_len = x1.shape[0]
    tm = min(TM_BWD, t_len)
    nt = t_len // tm
    n_mem = k.shape[0]
    sc = HEAD_DIM ** -0.5

    def body(dx2_ref, y_ref, x_ref, q_ref, o_ref, g3_ref, wq_ref, k_ref, v_ref, wo_ref, g4_ref,
             dx_ref, dwq_ref, dwo_ref, dk_ref, dv_ref, dg3_ref, dg4_ref, dqbuf, acc_q, acc_o):
        s = pl.program_id(0)

        @pl.when(s == 0)
        def _():
            acc_q[...] = jnp.zeros_like(acc_q)
            acc_o[...] = jnp.zeros_like(acc_o)
            dk_ref[...] = jnp.zeros_like(dk_ref)
            dv_ref[...] = jnp.zeros_like(dv_ref)
            dg3_ref[...] = jnp.zeros_like(dg3_ref)
            dg4_ref[...] = jnp.zeros_like(dg4_ref)

        yv = y_ref[...].astype(F32)
        _, yh, yr = _rms(yv, g4_ref[...])
        dy, dg4 = _rms_bwd(yh, yr, g4_ref[...], dx2_ref[...])
        dg4_ref[...] += dg4
        dyb = dy.astype(BF16)
        acc_o[...] += _tn(o_ref[...], dyb)
        do = _nt(dyb, wo_ref[...])
        qb = q_ref[...]
        for hd in range(HEADS):
            sl = slice(hd * HEAD_DIM, (hd + 1) * HEAD_DIM)
            p = _softmax_rows(_nt(qb[:, sl], k_ref[:, sl]) * sc)
            dob = do[:, sl].astype(BF16)
            dp = _nt(dob, v_ref[:, sl])
            dv_ref[:, sl] += _tn(p.astype(BF16), dob)
            ds = (p * (dp - jnp.sum(dp * p, axis=-1, keepdims=True)) * sc).astype(BF16)
            dqbuf[:, sl] = _nn(ds, k_ref[:, sl]).astype(BF16)
            dk_ref[:, sl] += _tn(ds, qb[:, sl])
        xv = x_ref[...]
        h, xh, xr = _rms(xv, g3_ref[...])
        dq = dqbuf[...]
        acc_q[...] += _tn(h.astype(BF16), dq)
        dh = _nt(dq, wq_ref[...])
        dxn, dg3 = _rms_bwd(xh, xr, g3_ref[...], dh)
        dg3_ref[...] += dg3
        dx_ref[...] = dx2_ref[...] + dxn

        @pl.when(s == nt - 1)
        def _():
            dwq_ref[...] = acc_q[...].astype(BF16)
            dwo_ref[...] = acc_o[...].astype(BF16)

    return pl.pallas_call(
        body, grid=(nt,), name="xattn_bwd",
        in_specs=[_rows(tm, D), _rows(tm, D), _rows(tm, D), _rows(tm, D), _rows(tm, D), _const((1, D)),
                  _const((D, D)), _const((n_mem, D)), _const((n_mem, D)), _const((D, D)), _const((1, D))],
        out_specs=[_rows(tm, D), _acc((D, D)), _acc((D, D)), _acc((n_mem, D)), _acc((n_mem, D)), _acc((1, D)),
                   _acc((1, D))],
        out_shape=[_sds((t_len, D), F32), _sds((D, D), BF16), _sds((D, D), BF16), _sds((n_mem, D), F32),
                   _sds((n_mem, D), F32), _sds((1, D), F32), _sds((1, D), F32)],
        scratch_shapes=[pltpu.VMEM((tm, D), BF16), pltpu.VMEM((D, D), F32), pltpu.VMEM((D, D), F32)],
        compiler_params=_params(("arbitrary",)),
    )(dx2, y, x1, q, o, g3, wq, k, v, wo, g4)


def _ffn_cols(half, part):
    c0 = part * D_FF + half * FF_CHUNK
    return c0, c0 + FF_CHUNK


def _conv3(buf, w_ref, c0, c1, base, tm):
    return (w_ref[0:1, c0:c1] * buf[base:base + tm, :] + w_ref[1:2, c0:c1] * buf[base + 1:base + 1 + tm, :]
            + w_ref[2:3, c0:c1] * buf[base + 2:base + 2 + tm, :])


def _ffn_fwd(x2, g5, wupt, wc, wdown, g6):
    t_len = x2.shape[0]
    tm = min(TM_BWD, t_len)
    nt = t_len // tm
    fh = FHALO

    def body(x_ref, g5_ref, wupt_ref, wc_ref, wdown_ref, g6_ref, u_ref, y_ref, x3_ref, carry, gbuf, vbuf):
        i = pl.program_id(0)

        @pl.when(i == 0)
        def _():
            carry[...] = jnp.zeros_like(carry)

        xv = x_ref[...]
        h, _, _ = _rms(xv, g5_ref[...])
        hb = h.astype(BF16)
        y = None
        for half in range(2):
            conv = []
            for part, buf in ((0, gbuf), (1, vbuf)):
                c0, c1 = _ffn_cols(half, part)
                u = _nt(hb, wupt_ref[c0:c1, :])
                u_ref[:, c0:c1] = u.astype(BF16)
                buf[0:fh, :] = carry[:, c0:c1]
                buf[fh:fh + tm, :] = u
                carry[:, c0:c1] = u[tm - fh:tm, :]
                conv.append(_conv3(buf, wc_ref, c0, c1, fh - 2, tm))
            a = (conv[0] * _sigmoid(conv[0]) * conv[1]).astype(BF16)
            part_y = _nn(a, wdown_ref[half * FF_CHUNK:(half + 1) * FF_CHUNK, :])
            y = part_y if y is None else y + part_y
        yb = y.astype(BF16)
        y_ref[...] = yb
        yn, _, _ = _rms(yb.astype(F32), g6_ref[...])
        x3_ref[...] = xv + yn

    return pl.pallas_call(
        body, grid=(nt,), name="ffn_fwd",
        in_specs=[_rows(tm, D), _const((1, D)), _const((2 * D_FF, D)), _const((3, 2 * D_FF)), _const((D_FF, D)),
                  _const((1, D))],
        out_specs=[_rows(tm, 2 * D_FF), _rows(tm, D), _rows(tm, D)],
        out_shape=[_sds((t_len, 2 * D_FF), BF16), _sds((t_len, D), BF16), _sds((t_len, D), F32)],
        scratch_shapes=[pltpu.VMEM((fh, 2 * D_FF), F32), pltpu.VMEM((fh + tm, FF_CHUNK), F32),
                        pltpu.VMEM((fh + tm, FF_CHUNK), F32)],
        compiler_params=_params(("arbitrary",)),
    )(x2, g5, wupt, wc, wdown, g6)


def _ffn_bwd(dx3, y, x2, u, g5, wupt, wc, wdown, g6):
    t_len = x2.shape[0]
    tm = min(TM_BWD, t_len)
    nt = t_len // tm
    fh = FHALO
    hb_blocks = tm // fh

    def body(dx3_ref, y_ref, x_ref, u_ref, uh_ref, g5_ref, wupt_ref, wc_ref, wdown_ref, g6_ref,
             dx_ref, du_ref, a_ref, dyo_ref, h_ref, dg5_ref, dg6_ref, dwc_ref, carry, gbuf, vbuf, dbuf):
        s = pl.program_id(0)
        ti = nt - 1 - s

        @pl.when(s == 0)
        def _():
            carry[...] = jnp.zeros_like(carry)
            dg5_ref[...] = jnp.zeros_like(dg5_ref)
            dg6_ref[...] = jnp.zeros_like(dg6_ref)
            dwc_ref[...] = jnp.zeros_like(dwc_ref)

        yv = y_ref[...].astype(F32)
        _, yh, yr = _rms(yv, g6_ref[...])
        dy, dg6 = _rms_bwd(yh, yr, g6_ref[...], dx3_ref[...])
        dg6_ref[...] += dg6
        dyb = dy.astype(BF16)
        dyo_ref[...] = dyb
        xv = x_ref[...]
        h, xh, xr = _rms(xv, g5_ref[...])
        h_ref[...] = h.astype(BF16)

        dh = None
        for half in range(2):
            conv = []
            for part, buf in ((0, gbuf), (1, vbuf)):
                c0, c1 = _ffn_cols(half, part)
                buf[0:fh, :] = jnp.where(ti > 0, uh_ref[:, c0:c1].astype(F32), 0.0)
                buf[fh:fh + tm, :] = u_ref[:, c0:c1].astype(F32)
                conv.append(_conv3(buf, wc_ref, c0, c1, fh - 2, tm))
            gt, vl = conv
            sg = _sigmoid(gt)
            sil = gt * sg
            a_ref[:, half * FF_CHUNK:(half + 1) * FF_CHUNK] = (sil * vl).astype(BF16)
            da = _nt(dyb, wdown_ref[half * FF_CHUNK:(half + 1) * FF_CHUNK, :])
            dcs = (da * vl * (sg * (1.0 + gt * (1.0 - sg))), da * sil)
            for part, buf in ((0, gbuf), (1, vbuf)):
                c0, c1 = _ffn_cols(half, part)
                dc = dcs[part]
                dbuf[0:tm, :] = dc
                dbuf[tm:tm + 8, :] = carry[:, c0:c1]
                carry[:, c0:c1] = dc[0:8, :]
                for j in range(3):
                    dwc_ref[j:j + 1, c0:c1] += jnp.sum(dc * buf[fh - 2 + j:fh - 2 + j + tm, :], axis=0, keepdims=True)
                du = (wc_ref[2:3, c0:c1] * dc + wc_ref[1:2, c0:c1] * dbuf[1:1 + tm, :]
                      + wc_ref[0:1, c0:c1] * dbuf[2:2 + tm, :]).astype(BF16)
                du_ref[:, c0:c1] = du
                term = _nn(du, wupt_ref[c0:c1, :])
                dh = term if dh is None else dh + term
        dxn, dg5 = _rms_bwd(xh, xr, g5_ref[...], dh)
        dg5_ref[...] += dg5
        dx_ref[...] = dx3_ref[...] + dxn

    uh_spec = pl.BlockSpec((fh, 2 * D_FF), lambda i: (jnp.maximum((nt - 1 - i) * hb_blocks - 1, 0), 0))
    return pl.pallas_call(
        body, grid=(nt,), name="ffn_bwd",
        in_specs=[_rows(tm, D, nt), _rows(tm, D, nt), _rows(tm, D, nt), _rows(tm, 2 * D_FF, nt), uh_spec,
                  _const((1, D)), _const((2 * D_FF, D)), _const((3, 2 * D_FF)), _const((D_FF, D)), _const((1, D))],
        out_specs=[_rows(tm, D, nt), _rows(tm, 2 * D_FF, nt), _rows(tm, D_FF, nt), _rows(tm, D, nt),
                   _rows(tm, D, nt), _acc((1, D)), _acc((1, D)), _acc((8, 2 * D_FF))],
        out_shape=[_sds((t_len, D), F32), _sds((t_len, 2 * D_FF), BF16), _sds((t_len, D_FF), BF16),
                   _sds((t_len, D), BF16), _sds((t_len, D), BF16), _sds((1, D), F32), _sds((1, D), F32),
                   _sds((8, 2 * D_FF), F32)],
        scratch_shapes=[pltpu.VMEM((8, 2 * D_FF), F32), pltpu.VMEM((fh + tm, FF_CHUNK), F32),
                        pltpu.VMEM((fh + tm, FF_CHUNK), F32), pltpu.VMEM((tm + 8, FF_CHUNK), F32)],
        compiler_params=_params(("arbitrary",)),
    )(dx3, y, x2, u, u, g5, wupt, wc, wdown, g6)


def _tn_matmul(a, b):
    t_len, m = a.shape
    bm = FF_CHUNK
    bt = min(TM_FWD, t_len)
    nt = t_len // bt

    def body(a_ref, b_ref, o_ref, acc):
        t = pl.program_id(1)

        @pl.when(t == 0)
        def _():
            acc[...] = jnp.zeros_like(acc)

        acc[...] += _tn(a_ref[...], b_ref[...])

        @pl.when(t == nt - 1)
        def _():
            o_ref[...] = acc[...].astype(BF16)

    return pl.pallas_call(
        body, grid=(m // bm, nt), name="tn_matmul",
        in_specs=[pl.BlockSpec((bt, bm), lambda i, t: (t, i)), pl.BlockSpec((bt, D), lambda i, t: (t, 0))],
        out_specs=pl.BlockSpec((bm, D), lambda i, t: (i, 0)),
        out_shape=_sds((m, D), BF16),
        scratch_shapes=[pltpu.VMEM((bm, D), F32)],
        compiler_params=_params(("parallel", "arbitrary")),
    )(a, b)


def _loss_grad(xf, target):
    t_len = xf.shape[0]
    tm = min(TM_FWD, t_len)
    nt = t_len // tm

    def body(x_ref, t_ref, dx_ref, loss_ref):
        @pl.when(pl.program_id(0) == 0)
        def _():
            loss_ref[...] = jnp.zeros_like(loss_ref)

        err = x_ref[...] - t_ref[...]
        dx_ref[...] = err * (1.0 / D)
        part = 0.5 * jnp.sum(jnp.mean(err * err, axis=-1, keepdims=True), axis=0, keepdims=True)
        loss_ref[...] += jnp.broadcast_to(part, loss_ref.shape)

    return pl.pallas_call(
        body, grid=(nt,), name="loss_grad",
        in_specs=[_rows(tm, D), _rows(tm, D)], out_specs=[_rows(tm, D), _acc((8, 128))],
        out_shape=[_sds((t_len, D), F32), _sds((8, 128), F32)],
        compiler_params=_params(("arbitrary",)),
    )(xf, target)


def _row_block(rows):
    for rb in (512, 256, 128, 64, 32, 16, 8):
        if rows % rb == 0:
            return rb
    return rows


def _cast_bf16(w):
    rows, cols = w.shape
    rb = _row_block(rows)

    def body(w_ref, o_ref):
        o_ref[...] = w_ref[...].astype(BF16)

    return pl.pallas_call(
        body, grid=(rows // rb,), name="cast_bf16", in_specs=[_rows(rb, cols)], out_specs=_rows(rb, cols),
        out_shape=_sds((rows, cols), BF16), compiler_params=_params(("parallel",)),
    )(w)


def _sum_chips(parts):
    n, rows, cols = parts.shape
    rb = _row_block(rows)

    def body(p_ref, o_ref):
        acc = p_ref[0].astype(F32)
        for j in range(1, n):
            acc = acc + p_ref[j].astype(F32)
        o_ref[...] = acc

    return pl.pallas_call(
        body, grid=(rows // rb,), name="sum_chips",
        in_specs=[pl.BlockSpec((n, rb, cols), lambda i: (0, i, 0))], out_specs=_rows(rb, cols),
        out_shape=_sds((rows, cols), F32), compiler_params=_params(("parallel",)),
    )(parts)


def _add2(a, b):
    rows, cols = a.shape
    rb = _row_block(rows)

    def body(a_ref, b_ref, o_ref):
        o_ref[...] = a_ref[...] + b_ref[...]

    return pl.pallas_call(
        body, grid=(rows // rb,), name="add2", in_specs=[_rows(rb, cols), _rows(rb, cols)],
        out_specs=_rows(rb, cols), out_shape=_sds((rows, cols), F32), compiler_params=_params(("parallel",)),
    )(a, b)


def _adamw(w, g, m, v):
    rows, cols = w.shape
    rb = _row_block(rows)

    def body(w_ref, g_ref, m_ref, v_ref, d_ref, nm_ref, nv_ref):
        gv = g_ref[...]
        nm = ADAM_B1 * m_ref[...] + (1.0 - ADAM_B1) * gv
        nv = ADAM_B2 * v_ref[...] + (1.0 - ADAM_B2) * (gv * gv)
        m_hat = nm / (1.0 - ADAM_B1 ** ADAM_STEP)
        v_hat = nv / (1.0 - ADAM_B2 ** ADAM_STEP)
        d_ref[...] = -ADAM_LR * (m_hat / (jnp.sqrt(v_hat) + ADAM_EPS) + ADAM_WD * w_ref[...])
        nm_ref[...] = nm
        nv_ref[...] = nv

    spec = _rows(rb, cols)
    return pl.pallas_call(
        body, grid=(rows // rb,), name="adamw", in_specs=[spec] * 4, out_specs=[spec] * 3,
        out_shape=[_sds((rows, cols), F32)] * 3, compiler_params=_params(("parallel",)),
    )(w, g, m, v)


_HBM = pl.BlockSpec(memory_space=pltpu.HBM)


def _mesh_pos():
    return lax.axis_index("x"), lax.axis_index("y"), lax.axis_index("c")


def _gather_chips(shards):
    n = len(shards)

    def body(*refs):
        ins, outs = refs[:n], refs[n:2 * n]
        send_sems, recv_sems, local_sems = refs[2 * n:]
        x, y, c = _mesh_pos()
        me = 2 * x + y
        peers = [(1 - x, y), (x, 1 - y), (1 - x, 1 - y)]
        started, sends = [], []
        for k in range(n):
            own = pltpu.make_async_copy(ins[k], outs[k].at[me], local_sems.at[k])
            own.start()
            started.append(own)
            for j, (px, py) in enumerate(peers):
                cp = pltpu.make_async_remote_copy(
                    src_ref=ins[k], dst_ref=outs[k].at[me], send_sem=send_sems.at[3 * k + j],
                    recv_sem=recv_sems.at[3 * k + j], device_id=(px, py, c), device_id_type=MESH_ID)
                cp.start()
                sends.append(cp)
        for k in range(n):
            for j, (px, py) in enumerate(peers):
                pltpu.make_async_remote_copy(
                    src_ref=ins[k], dst_ref=outs[k].at[2 * px + py], send_sem=send_sems.at[3 * k + j],
                    recv_sem=recv_sems.at[3 * k + j], device_id=(px, py, c), device_id_type=MESH_ID).wait_recv()
        for cp in sends:
            cp.wait_send()
        for cp in started:
            cp.wait()

    return pl.pallas_call(
        body, name="gather_chips", in_specs=[_HBM] * n, out_specs=[_HBM] * n,
        out_shape=[_sds((N_CHIPS,) + s.shape, s.dtype) for s in shards],
        scratch_shapes=[pltpu.SemaphoreType.DMA((3 * n,)), pltpu.SemaphoreType.DMA((3 * n,)),
                        pltpu.SemaphoreType.DMA((n,))],
    )(*shards)


def _scatter_chips(fulls, shard_rows):
    n = len(fulls)

    def body(*refs):
        ins, outs = refs[:n], refs[n:2 * n]
        send_sems, recv_sems, local_sems = refs[2 * n:]
        x, y, c = _mesh_pos()
        me = 2 * x + y
        peers = [(1 - x, y), (x, 1 - y), (1 - x, 1 - y)]

        def rows_of(k, chip):
            r = shard_rows[k]
            return ins[k].at[:, pl.ds(pl.multiple_of(chip * r, 16), r), :]

        started = []
        for k in range(n):
            own = pltpu.make_async_copy(rows_of(k, me), outs[k].at[me], local_sems.at[k])
            own.start()
            started.append(own)
        sends = []
        for k in range(n):
            for j, (px, py) in enumerate(peers):
                cp = pltpu.make_async_remote_copy(
                    src_ref=rows_of(k, 2 * px + py), dst_ref=outs[k].at[me], send_sem=send_sems.at[3 * k + j],
                    recv_sem=recv_sems.at[3 * k + j], device_id=(px, py, c), device_id_type=MESH_ID)
                cp.start()
                sends.append(cp)
        for k in range(n):
            for j, (px, py) in enumerate(peers):
                pltpu.make_async_remote_copy(
                    src_ref=rows_of(k, me), dst_ref=outs[k].at[2 * px + py], send_sem=send_sems.at[3 * k + j],
                    recv_sem=recv_sems.at[3 * k + j], device_id=(px, py, c), device_id_type=MESH_ID).wait_recv()
        for cp in sends:
            cp.wait_send()
        for cp in started:
            cp.wait()

    return pl.pallas_call(
        body, name="scatter_chips", in_specs=[_HBM] * n, out_specs=[_HBM] * n,
        out_shape=[_sds((N_CHIPS, f.shape[0], r, f.shape[2]), f.dtype) for f, r in zip(fulls, shard_rows)],
        scratch_shapes=[pltpu.SemaphoreType.DMA((3 * n,)), pltpu.SemaphoreType.DMA((3 * n,)),
                        pltpu.SemaphoreType.DMA((n,))],
    )(*fulls)


def _sibling_swap(arrs):
    n = len(arrs)

    def body(*refs):
        ins, outs = refs[:n], refs[n:2 * n]
        send_sems, recv_sems = refs[2 * n:]
        x, y, c = _mesh_pos()
        cps = []
        for k in range(n):
            cp = pltpu.make_async_remote_copy(
                src_ref=ins[k], dst_ref=outs[k], send_sem=send_sems.at[k], recv_sem=recv_sems.at[k],
                device_id=(x, y, 1 - c), device_id_type=MESH_ID)
            cp.start()
            cps.append(cp)
        for cp in cps:
            cp.wait()

    return pl.pallas_call(
        body, name="sibling_swap", in_specs=[_HBM] * n, out_specs=[_HBM] * n,
        out_shape=[_sds(a.shape, a.dtype) for a in arrs],
        scratch_shapes=[pltpu.SemaphoreType.DMA((n,)), pltpu.SemaphoreType.DMA((n,))],
    )(*arrs)


def _allgather_devices(v):
    m_per, n = v.shape

    def body(v_ref, out_ref, send_sems, recv_sems, local_sem):
        x, y, c = _mesh_pos()
        me, sibling = (x, y, c), (x, y, 1 - c)
        chips = [(1 - x, y), (x, 1 - y), (1 - x, 1 - y)]

        def rows(px, py, pc):
            return out_ref.at[4 * px + 2 * py + pc]

        def copy(k, block, to, src=None):
            return pltpu.make_async_remote_copy(
                src_ref=rows(*block) if src is None else src, dst_ref=rows(*block), send_sem=send_sems.at[k],
                recv_sem=recv_sems.at[k], device_id=to, device_id_type=MESH_ID)

        mine = pltpu.make_async_copy(v_ref, rows(*me), local_sem)
        mine.start()
        first = [copy(0, me, sibling, src=v_ref)]
        first += [copy(1 + j, me, (*chip, c), src=v_ref) for j, chip in enumerate(chips)]
        for cp in first:
            cp.start()
        passed = [copy(4 + j, (*chip, c), sibling) for j, chip in enumerate(chips)]
        for j, chip in enumerate(chips):
            copy(1 + j, (*chip, c), me).wait_recv()
            passed[j].start()
        copy(0, sibling, me).wait_recv()
        for j, chip in enumerate(chips):
            copy(4 + j, (*chip, 1 - c), me).wait_recv()
        for cp in first + passed:
            cp.wait_send()
        mine.wait()

    return pl.pallas_call(
        body, name="allgather_devices", out_shape=_sds((N_DEV, m_per, n), v.dtype),
        in_specs=[pl.BlockSpec(memory_space=pltpu.VMEM)], out_specs=pl.BlockSpec(memory_space=pltpu.VMEM),
        scratch_shapes=[pltpu.SemaphoreType.DMA((7,)), pltpu.SemaphoreType.DMA((7,)), pltpu.SemaphoreType.DMA],
        compiler_params=pltpu.CompilerParams(vmem_limit_bytes=VMEM_LIMIT),
    )(v)


def _sum_devices(parts):
    n, rows, cols = parts.shape

    def body(p_ref, o_ref):
        acc = p_ref[0]
        for j in range(1, n):
            acc = acc + p_ref[j]
        o_ref[...] = acc

    return pl.pallas_call(
        body, name="sum_devices", out_shape=_sds((rows, cols), F32),
        compiler_params=pltpu.CompilerParams(vmem_limit_bytes=VMEM_LIMIT),
    )(parts)


def _pack(parts):
    flat = jnp.concatenate([p.reshape(-1) for p in parts])
    rows = -(-flat.shape[0] // 1024) * 8
    return jnp.pad(flat, (0, rows * 128 - flat.shape[0])).reshape(rows, 128)


def _unpack(packed, shapes):
    flat = packed.reshape(-1)
    out, off = [], 0
    for shp in shapes:
        size = 1
        for d in shp:
            size *= d
        out.append(flat[off:off + size].reshape(shp))
        off += size
    return out


def _block_diag(maps):
    out = jnp.zeros((D_POOL, D_POOL), maps.dtype)
    for g in range(len(POOL_WINDOWS)):
        out = lax.dynamic_update_slice(out, maps[g], (g * POOL_GROUP, g * POOL_GROUP))
    return out


def _local_step(x, mem, target, small, big):
    depth = len(big)
    row = lambda a: a.reshape(1, -1)
    gmem = row(small["mem_norm"])
    saved = []
    for l in range(depth):
        w = big[l]
        sp = dict(
            g1=row(small["mix_pre_norm"][l]), g2=row(small["mix_post_norm"][l]),
            maps=_block_diag(small["pool_maps"][l]).astype(BF16), scale=row(small["pool_scale"][l]),
            dww=small["conf_dw_w"][l], dwb=row(small["conf_dw_b"][l]), lng=row(small["conf_ln_g"][l]),
            lnb=row(small["conf_ln_b"][l]), sw=small["sconv_w"][l],
            g3=row(small["xattn_pre_norm"][l]), g4=row(small["xattn_post_norm"][l]),
            g5=row(small["ffn_pre_norm"][l]), g6=row(small["ffn_post_norm"][l]), wc=small["ffn_conv_w"][l])
        z, ycat, y1, x1 = _mix_fwd(x, sp["g1"], w["wint"], sp["maps"], sp["scale"], sp["dww"], sp["dwb"],
                                   sp["lng"], sp["lnb"], sp["sw"], w["wout"], sp["g2"])
        k, v = _kv_fwd(mem, gmem, w["wk"], w["wv"])
        q, o, y2, x2 = _xattn_fwd(x1, sp["g3"], w["wq"], k, v, w["wo"], sp["g4"])
        u, y3, x3 = _ffn_fwd(x2, sp["g5"], w["wupt"], sp["wc"], w["wdown"], sp["g6"])
        saved.append(dict(sp=sp, x=x, z=z, ycat=ycat, y1=y1, x1=x1, k=k, v=v, q=q, o=o, y2=y2, x2=x2, u=u, y3=y3))
        x = x3

    dx, loss_blk = _loss_grad(x, target)
    big_grads = [None] * depth
    sg = {n: [None] * depth for n in ("mix_pre_norm", "mix_post_norm", "pool_maps", "pool_scale", "conf_dw_w",
                                      "conf_dw_b", "conf_ln_g", "conf_ln_b", "sconv_w", "xattn_pre_norm",
                                      "xattn_post_norm", "ffn_pre_norm", "ffn_post_norm", "ffn_conv_w")}
    dgmem = None
    for l in reversed(range(depth)):
        w, s = big[l], saved[l]
        sp = s["sp"]
        dx, du, a, dy3, h3, dg5, dg6, dwc = _ffn_bwd(dx, s["y3"], s["x2"], s["u"], sp["g5"], w["wupt"], sp["wc"],
                                                     w["wdown"], sp["g6"])
        dwupt = _tn_matmul(du, h3)
        dwdown = _tn_matmul(a, dy3)
        dx, dwq, dwo, dk, dv, dg3, dg4 = _xattn_bwd(dx, s["y2"], s["x1"], s["q"], s["o"], sp["g3"], w["wq"], s["k"],
                                                    s["v"], w["wo"], sp["g4"])
        dwk, dwv, dgm = _kv_bwd(mem, gmem, dk, dv, w["wk"], w["wv"])
        dgmem = dgm if dgmem is None else dgmem + dgm
        dx, dwint, dwout, dg1, dg2, dmaps, dscale, ddww, misc = _mix_bwd(
            dx, s["y1"], s["x"], s["z"], s["ycat"], sp["g1"], w["wint"], sp["maps"], sp["scale"], sp["dww"],
            sp["dwb"], sp["lng"], sp["lnb"], sp["sw"], w["wout"], sp["g2"])
        big_grads[l] = dict(wint=dwint, wout=dwout, wq=dwq, wk=dwk, wv=dwv, wo=dwo, wupt=dwupt, wdown=dwdown)
        sg["mix_pre_norm"][l] = dg1[0]
        sg["mix_post_norm"][l] = dg2[0]
        sg["pool_maps"][l] = jnp.stack([dmaps[g * 64:(g + 1) * 64, g * 64:(g + 1) * 64] for g in range(4)])
        sg["pool_scale"][l] = dscale[0]
        sg["conf_dw_w"][l] = ddww[0:CONF_K]
        sg["conf_dw_b"][l] = misc[0]
        sg["conf_ln_g"][l] = misc[1]
        sg["conf_ln_b"][l] = misc[2]
        sg["sconv_w"][l] = misc[3:6]
        sg["xattn_pre_norm"][l] = dg3[0]
        sg["xattn_post_norm"][l] = dg4[0]
        sg["ffn_pre_norm"][l] = dg5[0]
        sg["ffn_post_norm"][l] = dg6[0]
        sg["ffn_conv_w"][l] = dwc[0:3]
    small_grads = {n: jnp.stack(vs) for n, vs in sg.items()}
    small_grads["mem_norm"] = dgmem[0]
    return loss_blk, dx, big_grads, small_grads


_WEIGHTS = ['mem_norm', 'mix_pre_norm', 'mix_post_norm', 'w_in', 'pool_maps', 'pool_scale', 'conf_dw_w', 'conf_dw_b',
            'conf_ln_g', 'conf_ln_b', 'sconv_w', 'w_out', 'xattn_pre_norm', 'xattn_post_norm', 'xattn_wq',
            'xattn_wk', 'xattn_wv', 'xattn_wo', 'ffn_pre_norm', 'ffn_post_norm', 'ffn_w_up', 'ffn_conv_w',
            'ffn_w_down']
_BIG = {'w_in': ('wint', True), 'w_out': ('wout', False), 'xattn_wq': ('wq', False), 'xattn_wk': ('wk', False),
        'xattn_wv': ('wv', False), 'xattn_wo': ('wo', False), 'ffn_w_up': ('wupt', True),
        'ffn_w_down': ('wdown', False)}
_CHANNEL_SHARDED = ('conf_dw_w', 'sconv_w', 'ffn_conv_w')
_SMALL = [n for n in _WEIGHTS if n not in _BIG]


def kernel(x, mem, mem_norm, mix_pre_norm, mix_post_norm, w_in, pool_maps, pool_scale, conf_dw_w, conf_dw_b, conf_ln_g, conf_ln_b, sconv_w, w_out, xattn_pre_norm, xattn_post_norm, xattn_wq, xattn_wk, xattn_wv, xattn_wo, ffn_pre_norm, ffn_post_norm, ffn_w_up, ffn_conv_w, ffn_w_down, loss_target, m_mem_norm, m_mix_pre_norm, m_mix_post_norm, m_w_in, m_pool_maps, m_pool_scale, m_conf_dw_w, m_conf_dw_b, m_conf_ln_g, m_conf_ln_b, m_sconv_w, m_w_out, m_xattn_pre_norm, m_xattn_post_norm, m_xattn_wq, m_xattn_wk, m_xattn_wv, m_xattn_wo, m_ffn_pre_norm, m_ffn_post_norm, m_ffn_w_up, m_ffn_conv_w, m_ffn_w_down, v_mem_norm, v_mix_pre_norm, v_mix_post_norm, v_w_in, v_pool_maps, v_pool_scale, v_conf_dw_w, v_conf_dw_b, v_conf_ln_g, v_conf_ln_b, v_sconv_w, v_w_out, v_xattn_pre_norm, v_xattn_post_norm, v_xattn_wq, v_xattn_wk, v_xattn_wv, v_xattn_wo, v_ffn_pre_norm, v_ffn_post_norm, v_ffn_w_up, v_ffn_conv_w, v_ffn_w_down):
    wts = dict(mem_norm=mem_norm, mix_pre_norm=mix_pre_norm, mix_post_norm=mix_post_norm, w_in=w_in,
               pool_maps=pool_maps, pool_scale=pool_scale, conf_dw_w=conf_dw_w, conf_dw_b=conf_dw_b,
               conf_ln_g=conf_ln_g, conf_ln_b=conf_ln_b, sconv_w=sconv_w, w_out=w_out,
               xattn_pre_norm=xattn_pre_norm, xattn_post_norm=xattn_post_norm, xattn_wq=xattn_wq,
               xattn_wk=xattn_wk, xattn_wv=xattn_wv, xattn_wo=xattn_wo, ffn_pre_norm=ffn_pre_norm,
               ffn_post_norm=ffn_post_norm, ffn_w_up=ffn_w_up, ffn_conv_w=ffn_conv_w, ffn_w_down=ffn_w_down)
    mom_m = dict(mem_norm=m_mem_norm, mix_pre_norm=m_mix_pre_norm, mix_post_norm=m_mix_post_norm, w_in=m_w_in,
                 pool_maps=m_pool_maps, pool_scale=m_pool_scale, conf_dw_w=m_conf_dw_w, conf_dw_b=m_conf_dw_b,
                 conf_ln_g=m_conf_ln_g, conf_ln_b=m_conf_ln_b, sconv_w=m_sconv_w, w_out=m_w_out,
                 xattn_pre_norm=m_xattn_pre_norm, xattn_post_norm=m_xattn_post_norm, xattn_wq=m_xattn_wq,
                 xattn_wk=m_xattn_wk, xattn_wv=m_xattn_wv, xattn_wo=m_xattn_wo, ffn_pre_norm=m_ffn_pre_norm,
                 ffn_post_norm=m_ffn_post_norm, ffn_w_up=m_ffn_w_up, ffn_conv_w=m_ffn_conv_w,
                 ffn_w_down=m_ffn_w_down)
    mom_v = dict(mem_norm=v_mem_norm, mix_pre_norm=v_mix_pre_norm, mix_post_norm=v_mix_post_norm, w_in=v_w_in,
                 pool_maps=v_pool_maps, pool_scale=v_pool_scale, conf_dw_w=v_conf_dw_w, conf_dw_b=v_conf_dw_b,
                 conf_ln_g=v_conf_ln_g, conf_ln_b=v_conf_ln_b, sconv_w=v_sconv_w, w_out=v_w_out,
                 xattn_pre_norm=v_xattn_pre_norm, xattn_post_norm=v_xattn_post_norm, xattn_wq=v_xattn_wq,
                 xattn_wk=v_xattn_wk, xattn_wv=v_xattn_wv, xattn_wo=v_xattn_wo, ffn_pre_norm=v_ffn_pre_norm,
                 ffn_post_norm=v_ffn_post_norm, ffn_w_up=v_ffn_w_up, ffn_conv_w=v_ffn_conv_w,
                 ffn_w_down=v_ffn_w_down)
    depth = w_in.shape[0]
    chip = 2 * lax.axis_index("x") + lax.axis_index("y")

    shards = {}
    for name, (key, transposed) in _BIG.items():
        w = wts[name]
        wb = _cast_bf16(w.reshape(-1, w.shape[-1])).reshape(w.shape)
        shards[key] = wb.transpose(0, 2, 1) if transposed else wb
    keys = [key for key, _ in _BIG.values()]
    big = []
    for l in range(depth):
        gathered = _gather_chips([shards[key][l] for key in keys])
        big.append({key: g.reshape(-1, D) for key, g in zip(keys, gathered)})

    conv_shapes = [wts[n].shape for n in _CHANNEL_SHARDED]
    conv_all = _allgather_devices(_pack([wts[n] for n in _CHANNEL_SHARDED]))
    per_chip = [_unpack(conv_all[2 * j], conv_shapes) for j in range(N_CHIPS)]
    small = {n: wts[n] for n in _SMALL}
    for i, n in enumerate(_CHANNEL_SHARDED):
        small[n] = jnp.concatenate([per_chip[j][i] for j in range(N_CHIPS)], axis=-1)

    loss_blk, dx, big_grads, small_grads = _local_step(x[0], mem[0], loss_target[0], small, big)

    stacked = [jnp.stack([big_grads[l][key] for l in range(depth)]) for key in keys]
    shard_rows = [s.shape[1] // N_CHIPS for s in stacked]
    landed = _scatter_chips(stacked, shard_rows)
    core_sums = [_sum_chips(p.reshape(N_CHIPS, -1, D)) for p in landed]
    sibling_sums = _sibling_swap(core_sums)
    grads = {}
    for (name, (key, transposed)), mine, theirs, r in zip(_BIG.items(), core_sums, sibling_sums, shard_rows):
        g = _add2(mine, theirs).reshape(depth, r, D)
        grads[name] = g.transpose(0, 2, 1) if transposed else g

    small_shapes = [(128,)] + [small[n].shape for n in _SMALL]
    partial = _pack([loss_blk[0]] + [small_grads[n] for n in _SMALL])
    total = _unpack(_sum_devices(_allgather_devices(partial)), small_shapes)
    loss = total[0][0]
    for n, g in zip(_SMALL, total[1:]):
        if n in _CHANNEL_SHARDED:
            width = wts[n].shape[-1]
            g = lax.dynamic_slice_in_dim(g, chip * width, width, axis=-1)
        grads[n] = g

    delta, new_m, new_v = {}, {}, {}
    for name in _BIG:
        shp = wts[name].shape
        flat = lambda a: a.reshape(-1, shp[-1])
        d, nm, nv = _adamw(flat(wts[name]), flat(grads[name]), flat(mom_m[name]), flat(mom_v[name]))
        delta[name], new_m[name], new_v[name] = d.reshape(shp), nm.reshape(shp), nv.reshape(shp)
    shapes = [wts[n].shape for n in _SMALL]
    d, nm, nv = _adamw(_pack([wts[n] for n in _SMALL]), _pack([grads[n] for n in _SMALL]),
                       _pack([mom_m[n] for n in _SMALL]), _pack([mom_v[n] for n in _SMALL]))
    for out, packed in ((delta, d), (new_m, nm), (new_v, nv)):
        for n, a in zip(_SMALL, _unpack(packed, shapes)):
            out[n] = a

    return (loss, dx[None], *[grads[n] for n in _WEIGHTS], *[delta[n] for n in _WEIGHTS],
            *[new_m[n] for n in _WEIGHTS], *[new_v[n] for n in _WEIGHTS])
```

```python
import jax
import jax.numpy as jnp
from jax import lax
from jax.experimental import pallas as pl
from jax.experimental.pallas import tpu as pltpu

F32 = jnp.float32
BF16 = jnp.bfloat16

EPS = 1e-6
D = 1024
D_POOL, D_CONF, D_SCONV = 256, 384, 384
D_IN = D_POOL + 2 * D_CONF + 3 * D_SCONV
D_FF = 2816
FF_CHUNK = 1408
HEADS, HEAD_DIM = 4, 256
CONF_K = 31
POOL_WINDOWS = (2, 4, 8, 16)
POOL_GROUP = 64
SUBLANES = 8
HALO = 32
PHASE_ROWS = HALO - SUBLANES
FHALO = 16
TM_FWD = 512
TM_BWD = 256
N_CHIPS = 4
N_DEV = 8
MESH_ID = pl.DeviceIdType.MESH
VMEM_LIMIT = 56 << 20

ADAM_LR, ADAM_B1, ADAM_B2, ADAM_EPS, ADAM_WD, ADAM_STEP = 0.001, 0.9, 0.999, 1e-08, 0.01, 10

C_P = (0, 256)
C_A = (256, 640)
C_G = (640, 1024)
C_B = (1024, 1408)
C_C = (1408, 1792)
C_X = (1792, 2176)


def _nn(a, b):
    return jnp.dot(a, b, preferred_element_type=F32)


def _nt(a, b):
    return lax.dot_general(a, b, (((1,), (1,)), ((), ())), preferred_element_type=F32)


def _tn(a, b):
    return lax.dot_general(a, b, (((0,), (0,)), ((), ())), preferred_element_type=F32)


def _sigmoid(v):
    return 1.0 / (1.0 + jnp.exp(-v))


def _rms(v, g):
    r = lax.rsqrt(jnp.mean(v * v, axis=-1, keepdims=True) + EPS)
    vh = v * r
    return vh * g, vh, r


def _rms_bwd(vh, r, g, dy):
    dvh = dy * g
    dv = r * (dvh - vh * jnp.mean(dvh * vh, axis=-1, keepdims=True))
    return dv, jnp.sum(dy * vh, axis=0, keepdims=True)


def _colsum(v):
    return jnp.sum(v, axis=0, keepdims=True)


def _rows(tm, n, nt=None):
    if nt is None:
        return pl.BlockSpec((tm, n), lambda i: (i, 0))
    return pl.BlockSpec((tm, n), lambda i: (nt - 1 - i, 0))


def _const(shape):
    nd = len(shape)
    return pl.BlockSpec(shape, lambda i: (0,) * nd, pipeline_mode=pl.Buffered(1))


def _acc(shape):
    nd = len(shape)
    return pl.BlockSpec(shape, lambda i: (0,) * nd)


def _sds(shape, dtype):
    return jax.ShapeDtypeStruct(shape, dtype)


_HBM = pl.BlockSpec(memory_space=pltpu.HBM)


def _mesh_pos():
    return lax.axis_index("x"), lax.axis_index("y"), lax.axis_index("c")


def _chip_peers():
    x, y, c = _mesh_pos()
    flips = [(1 - x, y), (x, 1 - y), (1 - x, 1 - y)]
    return 2 * x + y, [((px, py, c), 2 * px + py) for px, py in flips]


def _remote(src, dst, send_sems, recv_sems, idx, dev):
    return pltpu.make_async_remote_copy(src_ref=src, dst_ref=dst, send_sem=send_sems.at[idx],
                                        recv_sem=recv_sems.at[idx], device_id=dev, device_id_type=MESH_ID)


class _Plan:
    def __init__(self, arrays):
        self.arrays = list(arrays)

    def scratch(self):
        n = len(self.arrays)
        return [pltpu.SemaphoreType.DMA((3 * n,)), pltpu.SemaphoreType.DMA((3 * n,)),
                pltpu.SemaphoreType.DMA((n,))]

    def _copies(self, ins, outs, sems):
        send_sems, recv_sems, local_sems = sems
        me, peers = _chip_peers()
        own, sends, recvs = [], [], []
        for k in range(len(ins)):
            own.append(pltpu.make_async_copy(self.src(ins, k, me), self.dst(outs, k, me), local_sems.at[k]))
            for j, (dev, chip) in enumerate(peers):
                sends.append(_remote(self.src(ins, k, chip), self.dst(outs, k, me), send_sems, recv_sems,
                                     3 * k + j, dev))
                recvs.append(_remote(self.src(ins, k, me), self.dst(outs, k, chip), send_sems, recv_sems,
                                     3 * k + j, dev))
        return own, sends, recvs

    def start(self, ins, outs, sems):
        own, sends, _ = self._copies(ins, outs, sems)
        for cp in own + sends:
            cp.start()

    def wait(self, ins, outs, sems):
        own, sends, recvs = self._copies(ins, outs, sems)
        for cp in recvs:
            cp.wait_recv()
        for cp in sends:
            cp.wait_send()
        for cp in own:
            cp.wait()


class _Gather(_Plan):
    tag = "gather"

    def out_shapes(self):
        return [_sds((N_CHIPS,) + a.shape, a.dtype) for a in self.arrays]

    def src(self, ins, k, chip):
        return ins[k]

    def dst(self, outs, k, chip):
        return outs[k].at[chip]


class _Scatter(_Plan):
    tag = "scatter"

    def __init__(self, arrays):
        super().__init__(arrays)
        self.rows = [a.shape[0] // N_CHIPS for a in self.arrays]
        self.offs = [sum(self.rows[:k]) for k in range(len(self.rows))]

    def out_shapes(self):
        a = self.arrays[0]
        return [_sds((N_CHIPS, sum(self.rows), a.shape[1]), a.dtype)]

    def src(self, ins, k, chip):
        r = self.rows[k]
        return ins[k].at[pl.ds(pl.multiple_of(chip * r, 16), r), :]

    def dst(self, outs, k, chip):
        return outs[0].at[chip, pl.ds(self.offs[k], self.rows[k]), :]


def _fused_call(name, body, grid, in_specs, out_specs, out_shape, scratch, args, comm=None, sem=("arbitrary",)):
    n_in, n_out, n_scr = len(in_specs), len(out_specs), len(scratch)
    c_in = comm.arrays if comm else []
    c_out = comm.out_shapes() if comm else []
    c_scr = comm.scratch() if comm else []

    def kernel_fn(*refs):
        ins, cins = refs[:n_in], refs[n_in:n_in + len(c_in)]
        o0 = n_in + len(c_in)
        outs, couts = refs[o0:o0 + n_out], refs[o0 + n_out:o0 + n_out + len(c_out)]
        s0 = o0 + n_out + len(c_out)
        scr, csems = refs[s0:s0 + n_scr], refs[s0 + n_scr:]
        if comm:
            body(ins, outs, scr, lambda: comm.start(cins, couts, csems), lambda: comm.wait(cins, couts, csems))
        else:
            body(ins, outs, scr, None, None)

    res = pl.pallas_call(
        kernel_fn, grid=grid, name=name + ("_" + comm.tag if comm else ""),
        in_specs=list(in_specs) + [_HBM] * len(c_in), out_specs=list(out_specs) + [_HBM] * len(c_out),
        out_shape=list(out_shape) + c_out, scratch_shapes=list(scratch) + c_scr,
        compiler_params=pltpu.CompilerParams(dimension_semantics=sem, vmem_limit_bytes=VMEM_LIMIT),
    )(*args, *c_in)
    return res[:n_out], res[n_out:]


def _bracket(start, wait, first, last):
    if start is not None:
        pl.when(first)(start)

    def finish():
        if wait is not None:
            pl.when(last)(wait)
    return finish


def _comm_only(plan):
    n_in, n_out = len(plan.arrays), len(plan.out_shapes())

    def body(*refs):
        ins, outs, sems = refs[:n_in], refs[n_in:n_in + n_out], refs[n_in + n_out:]
        plan.start(ins, outs, sems)
        plan.wait(ins, outs, sems)

    return pl.pallas_call(
        body, name=plan.tag + "_chips", in_specs=[_HBM] * n_in, out_specs=[_HBM] * n_out,
        out_shape=plan.out_shapes(), scratch_shapes=plan.scratch(),
    )(*plan.arrays)


def _sibling_swap(arrs):
    n = len(arrs)

    def body(*refs):
        ins, outs = refs[:n], refs[n:2 * n]
        send_sems, recv_sems = refs[2 * n:]
        x, y, c = _mesh_pos()
        cps = [_remote(ins[k], outs[k], send_sems, recv_sems, k, (x, y, 1 - c)) for k in range(n)]
        for cp in cps:
            cp.start()
        for cp in cps:
            cp.wait()

    return pl.pallas_call(
        body, name="sibling_swap", in_specs=[_HBM] * n, out_specs=[_HBM] * n,
        out_shape=[_sds(a.shape, a.dtype) for a in arrs],
        scratch_shapes=[pltpu.SemaphoreType.DMA((n,)), pltpu.SemaphoreType.DMA((n,))],
    )(*arrs)


def _allgather_devices(v):
    m_per, n = v.shape

    def body(v_ref, out_ref, send_sems, recv_sems, local_sem):
        x, y, c = _mesh_pos()
        me, sibling = (x, y, c), (x, y, 1 - c)
        chips = [(1 - x, y), (x, 1 - y), (1 - x, 1 - y)]

        def rows(px, py, pc):
            return out_ref.at[4 * px + 2 * py + pc]

        def copy(k, block, to, src=None):
            return _remote(rows(*block) if src is None else src, rows(*block), send_sems, recv_sems, k, to)

        mine = pltpu.make_async_copy(v_ref, rows(*me), local_sem)
        mine.start()
        first = [copy(0, me, sibling, src=v_ref)]
        first += [copy(1 + j, me, (*chip, c), src=v_ref) for j, chip in enumerate(chips)]
        for cp in first:
            cp.start()
        passed = [copy(4 + j, (*chip, c), sibling) for j, chip in enumerate(chips)]
        for j, chip in enumerate(chips):
            copy(1 + j, (*chip, c), me).wait_recv()
            passed[j].start()
        copy(0, sibling, me).wait_recv()
        for j, chip in enumerate(chips):
            copy(4 + j, (*chip, 1 - c), me).wait_recv()
        for cp in first + passed:
            cp.wait_send()
        mine.wait()

    return pl.pallas_call(
        body, name="allgather_devices", out_shape=_sds((N_DEV, m_per, n), v.dtype),
        in_specs=[pl.BlockSpec(memory_space=pltpu.VMEM)], out_specs=pl.BlockSpec(memory_space=pltpu.VMEM),
        scratch_shapes=[pltpu.SemaphoreType.DMA((7,)), pltpu.SemaphoreType.DMA((7,)), pltpu.SemaphoreType.DMA],
        compiler_params=pltpu.CompilerParams(vmem_limit_bytes=VMEM_LIMIT),
    )(v)


def _pool_lane():
    return lax.broadcasted_iota(jnp.int32, (1, D_POOL), 1)


def _pool_count(t0, tm):
    lane = _pool_lane()
    w = jnp.where(lane < 64, 2, jnp.where(lane < 128, 4, jnp.where(lane < 192, 8, 16)))
    pos1 = lax.broadcasted_iota(jnp.int32, (tm, D_POOL), 0) + (t0 + 1)
    return jnp.minimum(pos1, w).astype(F32)


def _fill_bands(band_ref, tm, causal):
    r = lax.broadcasted_iota(jnp.int32, (tm, tm + HALO), 0)
    s = lax.broadcasted_iota(jnp.int32, (tm, tm + HALO), 1)
    d = (r + HALO - s) if causal else (s - r)
    for g, w in enumerate(POOL_WINDOWS):
        band_ref[g] = jnp.where((d >= 0) & (d < w), 1.0, 0.0).astype(BF16)


def _window_sums(band_ref, operand, width):
    lane = _pool_lane()
    res = None
    for g in range(len(POOL_WINDOWS)):
        r = _nn(band_ref[g], operand)
        acc = r[:, 0:width]
        for c0 in range(width, r.shape[1], width):
            acc = acc + r[:, c0:c0 + width]
        res = acc if res is None else jnp.where(lane >= POOL_GROUP * g, acc, res)
    return res


def _phase_copies(src, phases, tm):
    for b in range(1, SUBLANES):
        phases[b - 1] = src[b:b + tm + PHASE_ROWS, :]


def _tap(src, phases, off, tm):
    a, b = divmod(off, SUBLANES)
    if b == 0:
        return src[SUBLANES * a:SUBLANES * a + tm, :]
    return phases[b - 1, SUBLANES * a:SUBLANES * a + tm, :]


def _layer_norm_stats(v1):
    mu = jnp.mean(v1, axis=-1, keepdims=True)
    xc = v1 - mu
    rs = lax.rsqrt(jnp.mean(xc * xc, axis=-1, keepdims=True) + EPS)
    return xc * rs, rs


def _mix_fwd(x, g1, wint, maps_bd, scale, dww, dwb, lng, lnb, sw, wout, g2, comm=None):
    t_len = x.shape[0]
    tm = min(TM_FWD, t_len)
    nt = t_len // tm
    h0 = HALO

    def body(ins, outs, scr, start, wait):
        (x_ref, g1_ref, wint_ref, maps_ref, scale_ref, dww_ref, dwb_ref, lng_ref, lnb_ref, sw_ref, wout_ref,
         g2_ref) = ins
        z_ref, ycat_ref, y_ref, x1_ref, v1_ref, cv_ref, pooled_ref = outs
        pbuf, vbuf, sbuf, phases, band = scr
        i = pl.program_id(0)
        finish = _bracket(start, wait, i == 0, i == nt - 1)

        @pl.when(i == 0)
        def _():
            pbuf[0:h0, :] = jnp.zeros((h0, D_POOL), F32)
            vbuf[0:h0, :] = jnp.zeros((h0, D_CONF), F32)
            sbuf[0:h0, :] = jnp.zeros((h0, D_SCONV), F32)
            _fill_bands(band, tm, True)

        @pl.when(i > 0)
        def _():
            pbuf[0:h0, :] = pbuf[tm:tm + h0, :]
            vbuf[0:h0, :] = vbuf[tm:tm + h0, :]
            sbuf[0:h0, :] = sbuf[tm:tm + h0, :]

        xv = x_ref[...]
        h, _, _ = _rms(xv, g1_ref[...])
        z = _nt(h.astype(BF16), wint_ref[...])
        z_ref[...] = z.astype(BF16)
        zp = z[:, C_P[0]:C_P[1]]
        pbuf[h0:h0 + tm, :] = zp
        vbuf[h0:h0 + tm, :] = z[:, C_A[0]:C_A[1]] * _sigmoid(z[:, C_G[0]:C_G[1]])
        sbuf[h0:h0 + tm, :] = z[:, C_C[0]:C_C[1]] * z[:, C_X[0]:C_X[1]]

        pv = pbuf[...]
        hi = pv.astype(BF16)
        lo = (pv - hi.astype(F32)).astype(BF16)
        sums = _window_sums(band, jnp.concatenate([hi, lo], axis=1), D_POOL)
        pooled = (sums / _pool_count(i * tm, tm) - zp).astype(BF16)
        pooled_ref[...] = pooled
        ycat_ref[:, 0:D_POOL] = (_nn(pooled, maps_ref[...]) * scale_ref[...]).astype(BF16)

        _phase_copies(vbuf, phases, tm)
        base = h0 - (CONF_K - 1)
        v1 = dww_ref[0:1, :] * _tap(vbuf, phases, base, tm)
        for j in range(1, CONF_K):
            v1 = v1 + dww_ref[j:j + 1, :] * _tap(vbuf, phases, base + j, tm)
        v1 = v1 + dwb_ref[...]
        v1_ref[...] = v1
        vh, _ = _layer_norm_stats(v1)
        v2 = vh * lng_ref[...] + lnb_ref[...]
        ycat_ref[:, D_POOL:D_POOL + D_CONF] = (v2 * _sigmoid(v2)).astype(BF16)

        cv = (sw_ref[0:1, :] * sbuf[h0 - 2:h0 - 2 + tm, :] + sw_ref[1:2, :] * sbuf[h0 - 1:h0 - 1 + tm, :]
              + sw_ref[2:3, :] * sbuf[h0:h0 + tm, :])
        cv_ref[...] = cv
        ycat_ref[:, D_POOL + D_CONF:D] = (z[:, C_B[0]:C_B[1]] * cv).astype(BF16)

        yb = _nn(ycat_ref[...], wout_ref[...]).astype(BF16)
        y_ref[...] = yb
        yn, _, _ = _rms(yb.astype(F32), g2_ref[...])
        x1_ref[...] = xv + yn
        finish()

    return _fused_call(
        "mix_fwd", body, (nt,),
        [_rows(tm, D), _const((1, D)), _const((D_IN, D)), _const((D_POOL, D_POOL)), _const((1, D_POOL)),
         _const((CONF_K, D_CONF)), _const((1, D_CONF)), _const((1, D_CONF)), _const((1, D_CONF)),
         _const((3, D_SCONV)), _const((D, D)), _const((1, D))],
        [_rows(tm, D_IN), _rows(tm, D), _rows(tm, D), _rows(tm, D), _rows(tm, D_CONF), _rows(tm, D_SCONV),
         _rows(tm, D_POOL)],
        [_sds((t_len, D_IN), BF16), _sds((t_len, D), BF16), _sds((t_len, D), BF16), _sds((t_len, D), F32),
         _sds((t_len, D_CONF), F32), _sds((t_len, D_SCONV), F32), _sds((t_len, D_POOL), BF16)],
        [pltpu.VMEM((h0 + tm, D_POOL), F32), pltpu.VMEM((h0 + tm, D_CONF), F32),
         pltpu.VMEM((h0 + tm, D_SCONV), F32), pltpu.VMEM((SUBLANES - 1, tm + PHASE_ROWS, D_CONF), F32),
         pltpu.VMEM((len(POOL_WINDOWS), tm, tm + h0), BF16)],
        (x, g1, wint, maps_bd, scale, dww, dwb, lng, lnb, sw, wout, g2), comm)


def _mix_bwd(dx1, y, x, z, ycat, v1, cv, pooled, g1, wint, maps_bd, scale, dww, lng, lnb, sw, wout, g2, comm=None):
    t_len = x.shape[0]
    tm = min(TM_BWD, t_len)
    nt = t_len // tm
    h0 = HALO

    def body(ins, outs, scr, start, wait):
        (dx1_ref, y_ref, x_ref, z_ref, ycat_ref, v1_ref, cv_ref, pooled_ref, g1_ref, wint_ref, maps_ref, scale_ref,
         dww_ref, lng_ref, lnb_ref, sw_ref, wout_ref, g2_ref) = ins
        dx_ref, dwin_ref, dwout_ref, dg1_ref, dg2_ref, dmaps_ref, dscale_ref, ddww_ref, misc_ref = outs
        ebuf, dvbuf, dcbuf, phases, band, dzbuf, acc_in, acc_out = scr
        s = pl.program_id(0)
        ti = nt - 1 - s
        finish = _bracket(start, wait, s == 0, s == nt - 1)

        @pl.when(s == 0)
        def _():
            ebuf[tm:tm + h0, :] = jnp.zeros((h0, D_POOL), F32)
            dvbuf[tm:tm + h0, :] = jnp.zeros((h0, D_CONF), F32)
            dcbuf[tm:tm + h0, :] = jnp.zeros((h0, D_SCONV), F32)
            _fill_bands(band, tm, False)
            acc_in[...] = jnp.zeros_like(acc_in)
            acc_out[...] = jnp.zeros_like(acc_out)
            dg1_ref[...] = jnp.zeros_like(dg1_ref)
            dg2_ref[...] = jnp.zeros_like(dg2_ref)
            dmaps_ref[...] = jnp.zeros_like(dmaps_ref)
            dscale_ref[...] = jnp.zeros_like(dscale_ref)
            ddww_ref[...] = jnp.zeros_like(ddww_ref)
            misc_ref[...] = jnp.zeros_like(misc_ref)

        @pl.when(s > 0)
        def _():
            ebuf[tm:tm + h0, :] = ebuf[0:h0, :]
            dvbuf[tm:tm + h0, :] = dvbuf[0:h0, :]
            dcbuf[tm:tm + h0, :] = dcbuf[0:h0, :]

        zt = z_ref[...].astype(F32)
        za, zg = zt[:, C_A[0]:C_A[1]], zt[:, C_G[0]:C_G[1]]
        zb, zc, zx = zt[:, C_B[0]:C_B[1]], zt[:, C_C[0]:C_C[1]], zt[:, C_X[0]:C_X[1]]
        sg = _sigmoid(zg)
        v0 = za * sg
        pv = zc * zx

        yv = y_ref[...].astype(F32)
        _, yh, yr = _rms(yv, g2_ref[...])
        dy, dg2 = _rms_bwd(yh, yr, g2_ref[...], dx1_ref[...])
        dg2_ref[...] += dg2
        dyb = dy.astype(BF16)
        acc_out[...] += _tn(ycat_ref[...], dyb)
        dycat = _nt(dyb, wout_ref[...])
        dya = dycat[:, 0:D_POOL]
        dyb2 = dycat[:, D_POOL:D_POOL + D_CONF]
        dyc = dycat[:, D_POOL + D_CONF:D]

        pooled_v = pooled_ref[...]
        pm = _nn(pooled_v, maps_ref[...])
        dscale_ref[...] += _colsum(dya * pm)
        dq = (dya * scale_ref[...]).astype(BF16)
        dmaps_ref[...] += _tn(pooled_v, dq)
        dpooled = _nt(dq, maps_ref[...])
        ebuf[0:tm, :] = dpooled / _pool_count(ti * tm, tm)
        dzbuf[:, C_P[0]:C_P[1]] = (_window_sums(band, ebuf[...].astype(BF16), D_POOL) - dpooled).astype(BF16)

        vh, rs = _layer_norm_stats(v1_ref[...])
        v2 = vh * lng_ref[...] + lnb_ref[...]
        s2 = _sigmoid(v2)
        dv2 = dyb2 * (s2 * (1.0 + v2 * (1.0 - s2)))
        misc_ref[1:2, :] += _colsum(dv2 * vh)
        misc_ref[2:3, :] += _colsum(dv2)
        dvh = dv2 * lng_ref[...]
        dv1 = rs * (dvh - jnp.mean(dvh, axis=-1, keepdims=True) - vh * jnp.mean(dvh * vh, axis=-1, keepdims=True))
        misc_ref[0:1, :] += _colsum(dv1)
        dvbuf[0:tm, :] = dv1
        _phase_copies(dvbuf, phases, tm)
        dv0 = None
        for k in range(CONF_K):
            j = CONF_K - 1 - k
            dk = _tap(dvbuf, phases, k, tm)
            ddww_ref[j:j + 1, :] += _colsum(v0 * dk)
            term = dww_ref[j:j + 1, :] * dk
            dv0 = term if dv0 is None else dv0 + term
        dzbuf[:, C_A[0]:C_A[1]] = (dv0 * sg).astype(BF16)
        dzbuf[:, C_G[0]:C_G[1]] = (dv0 * za * sg * (1.0 - sg)).astype(BF16)

        dzbuf[:, C_B[0]:C_B[1]] = (dyc * cv_ref[...]).astype(BF16)
        dcbuf[0:tm, :] = dyc * zb
        dp = None
        for k in range(3):
            j = 2 - k
            dk = dcbuf[k:k + tm, :]
            misc_ref[3 + j:4 + j, :] += _colsum(pv * dk)
            term = sw_ref[j:j + 1, :] * dk
            dp = term if dp is None else dp + term
        dzbuf[:, C_C[0]:C_C[1]] = (dp * zx).astype(BF16)
        dzbuf[:, C_X[0]:C_X[1]] = (dp * zc).astype(BF16)

        xv = x_ref[...]
        h, xh, xr = _rms(xv, g1_ref[...])
        dz = dzbuf[...]
        acc_in[...] += _tn(dz, h.astype(BF16))
        dh = _nn(dz, wint_ref[...])
        dxn, dg1 = _rms_bwd(xh, xr, g1_ref[...], dh)
        dg1_ref[...] += dg1
        dx_ref[...] = dx1_ref[...] + dxn

        @pl.when(s == nt - 1)
        def _():
            dwin_ref[...] = acc_in[...].astype(BF16)
            dwout_ref[...] = acc_out[...].astype(BF16)

        finish()

    return _fused_call(
        "mix_bwd", body, (nt,),
        [_rows(tm, D, nt), _rows(tm, D, nt), _rows(tm, D, nt), _rows(tm, D_IN, nt), _rows(tm, D, nt),
         _rows(tm, D_CONF, nt), _rows(tm, D_SCONV, nt), _rows(tm, D_POOL, nt), _const((1, D)), _const((D_IN, D)),
         _const((D_POOL, D_POOL)), _const((1, D_POOL)), _const((CONF_K, D_CONF)), _const((1, D_CONF)),
         _const((1, D_CONF)), _const((3, D_SCONV)), _const((D, D)), _const((1, D))],
        [_rows(tm, D, nt), _acc((D_IN, D)), _acc((D, D)), _acc((1, D)), _acc((1, D)), _acc((D_POOL, D_POOL)),
         _acc((1, D_POOL)), _acc((32, D_CONF)), _acc((8, D_CONF))],
        [_sds((t_len, D), F32), _sds((D_IN, D), BF16), _sds((D, D), BF16), _sds((1, D), F32), _sds((1, D), F32),
         _sds((D_POOL, D_POOL), F32), _sds((1, D_POOL), F32), _sds((32, D_CONF), F32), _sds((8, D_CONF), F32)],
        [pltpu.VMEM((tm + h0, D_POOL), F32), pltpu.VMEM((tm + h0, D_CONF), F32),
         pltpu.VMEM((tm + h0, D_SCONV), F32), pltpu.VMEM((SUBLANES - 1, tm + PHASE_ROWS, D_CONF), F32),
         pltpu.VMEM((len(POOL_WINDOWS), tm, tm + h0), BF16), pltpu.VMEM((tm, D_IN), BF16),
         pltpu.VMEM((D_IN, D), F32), pltpu.VMEM((D, D), F32)],
        (dx1, y, x, z, ycat, v1, cv, pooled, g1, wint, maps_bd, scale, dww, lng, lnb, sw, wout, g2), comm)


def _kv_fwd(mem, gmem, wk, wv):
    def body(mem_ref, g_ref, wk_ref, wv_ref, k_ref, v_ref):
        mn, _, _ = _rms(mem_ref[...], g_ref[...])
        mnb = mn.astype(BF16)
        k_ref[...] = _nn(mnb, wk_ref[...]).astype(BF16)
        v_ref[...] = _nn(mnb, wv_ref[...]).astype(BF16)

    n = mem.shape[0]
    return pl.pallas_call(
        body, name="kv_fwd", out_shape=[_sds((n, D), BF16), _sds((n, D), BF16)],
        compiler_params=pltpu.CompilerParams(vmem_limit_bytes=VMEM_LIMIT),
    )(mem, gmem, wk, wv)


def _kv_bwd(mem, gmem, dk, dv, wk, wv):
    def body(mem_ref, g_ref, dk_ref, dv_ref, wk_ref, wv_ref, dwk_ref, dwv_ref, dg_ref):
        mn, mh, _ = _rms(mem_ref[...], g_ref[...])
        mnb = mn.astype(BF16)
        dkb = dk_ref[...].astype(BF16)
        dvb = dv_ref[...].astype(BF16)
        dwk_ref[...] = _tn(mnb, dkb).astype(BF16)
        dwv_ref[...] = _tn(mnb, dvb).astype(BF16)
        dmn = _nt(dkb, wk_ref[...]) + _nt(dvb, wv_ref[...])
        dg_ref[...] = _colsum(dmn * mh)

    return pl.pallas_call(
        body, name="kv_bwd", out_shape=[_sds((D, D), BF16), _sds((D, D), BF16), _sds((1, D), F32)],
        compiler_params=pltpu.CompilerParams(vmem_limit_bytes=VMEM_LIMIT),
    )(mem, gmem, dk, dv, wk, wv)


def _softmax_rows(s):
    e = jnp.exp(s - jnp.max(s, axis=-1, keepdims=True))
    return e / jnp.sum(e, axis=-1, keepdims=True)


def _xattn_fwd(x1, g3, wq, k, v, wo, g4, comm=None):
    t_len = x1.shape[0]
    tm = min(TM_FWD, t_len)
    nt = t_len // tm
    n_mem = k.shape[0]
    sc = HEAD_DIM ** -0.5

    def body(ins, outs, scr, start, wait):
        x_ref, g3_ref, wq_ref, k_ref, v_ref, wo_ref, g4_ref = ins
        q_ref, o_ref, y_ref, x2_ref = outs
        i = pl.program_id(0)
        finish = _bracket(start, wait, i == 0, i == nt - 1)
        xv = x_ref[...]
        h, _, _ = _rms(xv, g3_ref[...])
        qb = _nn(h.astype(BF16), wq_ref[...]).astype(BF16)
        q_ref[...] = qb
        for hd in range(HEADS):
            sl = slice(hd * HEAD_DIM, (hd + 1) * HEAD_DIM)
            p = _softmax_rows(_nt(qb[:, sl], k_ref[:, sl]) * sc)
            o_ref[:, sl] = _nn(p.astype(BF16), v_ref[:, sl]).astype(BF16)
        yb = _nn(o_ref[...], wo_ref[...]).astype(BF16)
        y_ref[...] = yb
        yn, _, _ = _rms(yb.astype(F32), g4_ref[...])
        x2_ref[...] = xv + yn
        finish()

    return _fused_call(
        "xattn_fwd", body, (nt,),
        [_rows(tm, D), _const((1, D)), _const((D, D)), _const((n_mem, D)), _const((n_mem, D)), _const((D, D)),
         _const((1, D))],
        [_rows(tm, D), _rows(tm, D), _rows(tm, D), _rows(tm, D)],
        [_sds((t_len, D), BF16), _sds((t_len, D), BF16), _sds((t_len, D), BF16), _sds((t_len, D), F32)],
        [], (x1, g3, wq, k, v, wo, g4), comm)


def _xattn_bwd(dx2, y, x1, q, o, g3, wq, k, v, wo, g4):
    t_len = x1.shape[0]
    tm = min(TM_BWD, t_len)
    nt = t_len // tm
    n_mem = k.shape[0]
    sc = HEAD_DIM ** -0.5

    def body(ins, outs, scr, start, wait):
        dx2_ref, y_ref, x_ref, q_ref, o_ref, g3_ref, wq_ref, k_ref, v_ref, wo_ref, g4_ref = ins
        dx_ref, dwq_ref, dwo_ref, dk_ref, dv_ref, dg3_ref, dg4_ref = outs
        dqbuf, acc_q, acc_o = scr
        s = pl.program_id(0)

        @pl.when(s == 0)
        def _():
            acc_q[...] = jnp.zeros_like(acc_q)
            acc_o[...] = jnp.zeros_like(acc_o)
            dk_ref[...] = jnp.zeros_like(dk_ref)
            dv_ref[...] = jnp.zeros_like(dv_ref)
            dg3_ref[...] = jnp.zeros_like(dg3_ref)
            dg4_ref[...] = jnp.zeros_like(dg4_ref)

        yv = y_ref[...].astype(F32)
        _, yh, yr = _rms(yv, g4_ref[...])
        dy, dg4 = _rms_bwd(yh, yr, g4_ref[...], dx2_ref[...])
        dg4_ref[...] += dg4
        dyb = dy.astype(BF16)
        acc_o[...] += _tn(o_ref[...], dyb)
        do = _nt(dyb, wo_ref[...])
        qb = q_ref[...]
        for hd in range(HEADS):
            sl = slice(hd * HEAD_DIM, (hd + 1) * HEAD_DIM)
            p = _softmax_rows(_nt(qb[:, sl], k_ref[:, sl]) * sc)
            dob = do[:, sl].astype(BF16)
            dp = _nt(dob, v_ref[:, sl])
            dv_ref[:, sl] += _tn(p.astype(BF16), dob)
            ds = (p * (dp - jnp.sum(dp * p, axis=-1, keepdims=True)) * sc).astype(BF16)
            dqbuf[:, sl] = _nn(ds, k_ref[:, sl]).astype(BF16)
            dk_ref[:, sl] += _tn(ds, qb[:, sl])
        xv = x_ref[...]
        h, xh, xr = _rms(xv, g3_ref[...])
        dq = dqbuf[...]
        acc_q[...] += _tn(h.astype(BF16), dq)
        dh = _nt(dq, wq_ref[...])
        dxn, dg3 = _rms_bwd(xh, xr, g3_ref[...], dh)
        dg3_ref[...] += dg3
        dx_ref[...] = dx2_ref[...] + dxn

        @pl.when(s == nt - 1)
        def _():
            dwq_ref[...] = acc_q[...].astype(BF16)
            dwo_ref[...] = acc_o[...].astype(BF16)

    outs, _ = _fused_call(
        "xattn_bwd", body, (nt,),
        [_rows(tm, D), _rows(tm, D), _rows(tm, D), _rows(tm, D), _rows(tm, D), _const((1, D)), _const((D, D)),
         _const((n_mem, D)), _const((n_mem, D)), _const((D, D)), _const((1, D))],
        [_rows(tm, D), _acc((D, D)), _acc((D, D)), _acc((n_mem, D)), _acc((n_mem, D)), _acc((1, D)), _acc((1, D))],
        [_sds((t_len, D), F32), _sds((D, D), BF16), _sds((D, D), BF16), _sds((n_mem, D), F32),
         _sds((n_mem, D), F32), _sds((1, D), F32), _sds((1, D), F32)],
        [pltpu.VMEM((tm, D), BF16), pltpu.VMEM((D, D), F32), pltpu.VMEM((D, D), F32)],
        (dx2, y, x1, q, o, g3, wq, k, v, wo, g4))
    return outs


def _ffn_cols(half, part):
    c0 = part * D_FF + half * FF_CHUNK
    return c0, c0 + FF_CHUNK


def _ffn_fwd(x2, g5, wupt, wc, wdown, g6, comm=None):
    t_len = x2.shape[0]
    tm = min(TM_BWD, t_len)
    nt = t_len // tm
    fh = FHALO

    def body(ins, outs, scr, start, wait):
        x_ref, g5_ref, wupt_ref, wc_ref, wdown_ref, g6_ref = ins
        u_ref, c_ref, a_ref, y_ref, x3_ref = outs
        carry, gbuf, vbuf = scr
        i = pl.program_id(0)
        finish = _bracket(start, wait, i == 0, i == nt - 1)

        @pl.when(i == 0)
        def _():
            carry[...] = jnp.zeros_like(carry)

        xv = x_ref[...]
        h, _, _ = _rms(xv, g5_ref[...])
        hb = h.astype(BF16)
        y = None
        for half in range(2):
            conv = []
            for part, buf in ((0, gbuf), (1, vbuf)):
                c0, c1 = _ffn_cols(half, part)
                u = _nt(hb, wupt_ref[c0:c1, :])
                u_ref[:, c0:c1] = u.astype(BF16)
                buf[0:fh, :] = carry[:, c0:c1]
                buf[fh:fh + tm, :] = u
                carry[:, c0:c1] = u[tm - fh:tm, :]
                cb = (wc_ref[0:1, c0:c1] * buf[fh - 2:fh - 2 + tm, :] + wc_ref[1:2, c0:c1] * buf[fh - 1:fh - 1 + tm, :]
                      + wc_ref[2:3, c0:c1] * u).astype(BF16)
                c_ref[:, c0:c1] = cb
                conv.append(cb.astype(F32))
            a = (conv[0] * _sigmoid(conv[0]) * conv[1]).astype(BF16)
            a_ref[:, half * FF_CHUNK:(half + 1) * FF_CHUNK] = a
            part_y = _nn(a, wdown_ref[half * FF_CHUNK:(half + 1) * FF_CHUNK, :])
            y = part_y if y is None else y + part_y
        yb = y.astype(BF16)
        y_ref[...] = yb
        yn, _, _ = _rms(yb.astype(F32), g6_ref[...])
        x3_ref[...] = xv + yn
        finish()

    return _fused_call(
        "ffn_fwd", body, (nt,),
        [_rows(tm, D), _const((1, D)), _const((2 * D_FF, D)), _const((3, 2 * D_FF)), _const((D_FF, D)),
         _const((1, D))],
        [_rows(tm, 2 * D_FF), _rows(tm, 2 * D_FF), _rows(tm, D_FF), _rows(tm, D), _rows(tm, D)],
        [_sds((t_len, 2 * D_FF), BF16), _sds((t_len, 2 * D_FF), BF16), _sds((t_len, D_FF), BF16),
         _sds((t_len, D), BF16), _sds((t_len, D), F32)],
        [pltpu.VMEM((fh, 2 * D_FF), F32), pltpu.VMEM((fh + tm, FF_CHUNK), F32), pltpu.VMEM((fh + tm, FF_CHUNK), F32)],
        (x2, g5, wupt, wc, wdown, g6), comm)


def _ffn_bwd(dx3, y, x2, u, c, g5, wupt, wc, wdown, g6, comm=None):
    t_len = x2.shape[0]
    tm = min(TM_BWD, t_len)
    nt = t_len // tm

    def body(ins, outs, scr, start, wait):
        dx3_ref, y_ref, x_ref, u_ref, c_ref, g5_ref, wupt_ref, wc_ref, wdown_ref, g6_ref = ins
        dx_ref, du_ref, dyo_ref, h_ref, dg5_ref, dg6_ref, dwc_ref = outs
        carry, dbuf = scr
        s = pl.program_id(0)
        finish = _bracket(start, wait, s == 0, s == nt - 1)

        @pl.when(s == 0)
        def _():
            carry[...] = jnp.zeros_like(carry)
            dg5_ref[...] = jnp.zeros_like(dg5_ref)
            dg6_ref[...] = jnp.zeros_like(dg6_ref)
            dwc_ref[...] = jnp.zeros_like(dwc_ref)

        yv = y_ref[...].astype(F32)
        _, yh, yr = _rms(yv, g6_ref[...])
        dy, dg6 = _rms_bwd(yh, yr, g6_ref[...], dx3_ref[...])
        dg6_ref[...] += dg6
        dyb = dy.astype(BF16)
        dyo_ref[...] = dyb
        xv = x_ref[...]
        h, xh, xr = _rms(xv, g5_ref[...])
        h_ref[...] = h.astype(BF16)

        dh = None
        for half in range(2):
            g0, g1 = _ffn_cols(half, 0)
            v0, v1 = _ffn_cols(half, 1)
            gt = c_ref[:, g0:g1].astype(F32)
            vl = c_ref[:, v0:v1].astype(F32)
            sg = _sigmoid(gt)
            sil = gt * sg
            da = _nt(dyb, wdown_ref[half * FF_CHUNK:(half + 1) * FF_CHUNK, :])
            dcs = (da * vl * (sg * (1.0 + gt * (1.0 - sg))), da * sil)
            for part in range(2):
                c0, c1 = _ffn_cols(half, part)
                dc = dcs[part]
                dbuf[0:tm, :] = dc
                dbuf[tm:tm + SUBLANES, :] = carry[:, c0:c1]
                carry[:, c0:c1] = dc[0:SUBLANES, :]
                d1 = dbuf[1:1 + tm, :]
                d2 = dbuf[2:2 + tm, :]
                uu = u_ref[:, c0:c1].astype(F32)
                dwc_ref[2:3, c0:c1] += _colsum(uu * dc)
                dwc_ref[1:2, c0:c1] += _colsum(uu * d1)
                dwc_ref[0:1, c0:c1] += _colsum(uu * d2)
                du = (wc_ref[2:3, c0:c1] * dc + wc_ref[1:2, c0:c1] * d1 + wc_ref[0:1, c0:c1] * d2).astype(BF16)
                du_ref[:, c0:c1] = du
                term = _nn(du, wupt_ref[c0:c1, :])
                dh = term if dh is None else dh + term
        dxn, dg5 = _rms_bwd(xh, xr, g5_ref[...], dh)
        dg5_ref[...] += dg5
        dx_ref[...] = dx3_ref[...] + dxn
        finish()

    return _fused_call(
        "ffn_bwd", body, (nt,),
        [_rows(tm, D, nt), _rows(tm, D, nt), _rows(tm, D, nt), _rows(tm, 2 * D_FF, nt), _rows(tm, 2 * D_FF, nt),
         _const((1, D)), _const((2 * D_FF, D)), _const((3, 2 * D_FF)), _const((D_FF, D)), _const((1, D))],
        [_rows(tm, D, nt), _rows(tm, 2 * D_FF, nt), _rows(tm, D, nt), _rows(tm, D, nt), _acc((1, D)), _acc((1, D)),
         _acc((8, 2 * D_FF))],
        [_sds((t_len, D), F32), _sds((t_len, 2 * D_FF), BF16), _sds((t_len, D), BF16), _sds((t_len, D), BF16),
         _sds((1, D), F32), _sds((1, D), F32), _sds((8, 2 * D_FF), F32)],
        [pltpu.VMEM((SUBLANES, 2 * D_FF), F32), pltpu.VMEM((tm + SUBLANES, FF_CHUNK), F32)],
        (dx3, y, x2, u, c, g5, wupt, wc, wdown, g6), comm)


def _tn_matmul(a, b):
    t_len, m = a.shape
    bm = FF_CHUNK
    bt = min(TM_FWD, t_len)
    nt = t_len // bt

    def body(a_ref, b_ref, o_ref, acc):
        t = pl.program_id(1)

        @pl.when(t == 0)
        def _():
            acc[...] = jnp.zeros_like(acc)

        acc[...] += _tn(a_ref[...], b_ref[...])

        @pl.when(t == nt - 1)
        def _():
            o_ref[...] = acc[...].astype(BF16)

    return pl.pallas_call(
        body, grid=(m // bm, nt), name="tn_matmul",
        in_specs=[pl.BlockSpec((bt, bm), lambda i, t: (t, i)), pl.BlockSpec((bt, D), lambda i, t: (t, 0))],
        out_specs=pl.BlockSpec((bm, D), lambda i, t: (i, 0)),
        out_shape=_sds((m, D), BF16),
        scratch_shapes=[pltpu.VMEM((bm, D), F32)],
        compiler_params=pltpu.CompilerParams(dimension_semantics=("parallel", "arbitrary"),
                                             vmem_limit_bytes=VMEM_LIMIT),
    )(a, b)


def _loss_grad(xf, target):
    t_len = xf.shape[0]
    tm = min(TM_FWD, t_len)
    nt = t_len // tm

    def body(ins, outs, scr, start, wait):
        x_ref, t_ref = ins
        dx_ref, loss_ref = outs

        @pl.when(pl.program_id(0) == 0)
        def _():
            loss_ref[...] = jnp.zeros_like(loss_ref)

        err = x_ref[...] - t_ref[...]
        dx_ref[...] = err * (1.0 / D)
        part = 0.5 * _colsum(jnp.mean(err * err, axis=-1, keepdims=True))
        loss_ref[...] += jnp.broadcast_to(part, loss_ref.shape)

    outs, _ = _fused_call(
        "loss_grad", body, (nt,), [_rows(tm, D), _rows(tm, D)], [_rows(tm, D), _acc((8, 128))],
        [_sds((t_len, D), F32), _sds((8, 128), F32)], [], (xf, target))
    return outs


def _row_block(rows):
    for rb in (512, 256, 128, 64, 32, 16, 8):
        if rows % rb == 0:
            return rb
    return rows


def _elementwise(name, fn, ins, out_dtypes):
    rows, cols = ins[0].shape
    rb = _row_block(rows)
    n_in = len(ins)

    def body(*refs):
        res = fn(*[r[...] for r in refs[:n_in]])
        for o_ref, r in zip(refs[n_in:], res):
            o_ref[...] = r

    spec = pl.BlockSpec((rb, cols), lambda i: (i, 0))
    return pl.pallas_call(
        body, grid=(rows // rb,), name=name, in_specs=[spec] * n_in, out_specs=[spec] * len(out_dtypes),
        out_shape=[_sds((rows, cols), dt) for dt in out_dtypes],
        compiler_params=pltpu.CompilerParams(dimension_semantics=("parallel",), vmem_limit_bytes=VMEM_LIMIT),
    )(*ins)


def _cast_bf16(w):
    return _elementwise("cast_bf16", lambda v: (v.astype(BF16),), [w], [BF16])[0]


def _add2(a, b):
    return _elementwise("add2", lambda u, v: (u + v,), [a, b], [F32])[0]


def _adam_math(w, g, m, v):
    nm = ADAM_B1 * m + (1.0 - ADAM_B1) * g
    nv = ADAM_B2 * v + (1.0 - ADAM_B2) * (g * g)
    m_hat = nm / (1.0 - ADAM_B1 ** ADAM_STEP)
    v_hat = nv / (1.0 - ADAM_B2 ** ADAM_STEP)
    return -ADAM_LR * (m_hat / (jnp.sqrt(v_hat) + ADAM_EPS) + ADAM_WD * w), nm, nv


def _adamw(w, g, m, v):
    return _elementwise("adamw", _adam_math, [w, g, m, v], [F32, F32, F32])


def _sum_chips(parts):
    n, rows, cols = parts.shape
    rb = _row_block(rows)

    def body(p_ref, o_ref):
        acc = p_ref[0].astype(F32)
        for j in range(1, n):
            acc = acc + p_ref[j].astype(F32)
        o_ref[...] = acc

    return pl.pallas_call(
        body, grid=(rows // rb,), name="sum_chips",
        in_specs=[pl.BlockSpec((n, rb, cols), lambda i: (0, i, 0))], out_specs=pl.BlockSpec((rb, cols), lambda i: (i, 0)),
        out_shape=_sds((rows, cols), F32),
        compiler_params=pltpu.CompilerParams(dimension_semantics=("parallel",), vmem_limit_bytes=VMEM_LIMIT),
    )(parts)


def _sum_devices(parts):
    n, rows, cols = parts.shape

    def body(p_ref, o_ref):
        acc = p_ref[0]
        for j in range(1, n):
            acc = acc + p_ref[j]
        o_ref[...] = acc

    return pl.pallas_call(
        body, name="sum_devices", out_shape=_sds((rows, cols), F32),
        compiler_params=pltpu.CompilerParams(vmem_limit_bytes=VMEM_LIMIT),
    )(parts)


def _pack(parts):
    flat = jnp.concatenate([p.reshape(-1) for p in parts])
    rows = -(-flat.shape[0] // 1024) * 8
    return jnp.pad(flat, (0, rows * 128 - flat.shape[0])).reshape(rows, 128)


def _unpack(packed, shapes):
    flat = packed.reshape(-1)
    out, off = [], 0
    for shp in shapes:
        size = 1
        for d in shp:
            size *= d
        out.append(flat[off:off + size].reshape(shp))
        off += size
    return out


_BIG = {'w_in': ('wint', True), 'w_out': ('wout', False), 'xattn_wq': ('wq', False), 'xattn_wk': ('wk', False),
        'xattn_wv': ('wv', False), 'xattn_wo': ('wo', False), 'ffn_w_up': ('wupt', True),
        'ffn_w_down': ('wdown', False)}
_KEYS = [key for key, _ in _BIG.values()]
_GATHER_WITH = {"mix": ("wint", "wout", "wq", "wk"), "xattn": ("wv", "wo"), "ffn": ("wupt", "wdown")}
_SCATTER_LATE = ("wupt", "wdown", "wq", "wk", "wv", "wo")
_SCATTER_NEXT = ("wint", "wout")


def _block_diag(maps):
    out = jnp.zeros((D_POOL, D_POOL), maps.dtype)
    for g in range(len(POOL_WINDOWS)):
        out = lax.dynamic_update_slice(out, maps[g], (g * POOL_GROUP, g * POOL_GROUP))
    return out


def _local_step(x, mem, target, small, big, shards=None):
    distributed = shards is not None
    depth = len(shards) if distributed else len(big)
    big = list(big)
    row = lambda a: a.reshape(1, -1)
    gmem = row(small["mem_norm"])
    saved = []
    for l in range(depth):
        w = big[l]
        nxt = {}

        def plan(stage):
            if distributed and l + 1 < depth:
                return _Gather([shards[l + 1][key] for key in _GATHER_WITH[stage]])
            return None

        def landed(stage, outs):
            for key, g in zip(_GATHER_WITH[stage], outs):
                nxt[key] = g.reshape(-1, D)

        sp = dict(
            g1=row(small["mix_pre_norm"][l]), g2=row(small["mix_post_norm"][l]),
            maps=_block_diag(small["pool_maps"][l]).astype(BF16), scale=row(small["pool_scale"][l]),
            dww=small["conf_dw_w"][l], dwb=row(small["conf_dw_b"][l]), lng=row(small["conf_ln_g"][l]),
            lnb=row(small["conf_ln_b"][l]), sw=small["sconv_w"][l],
            g3=row(small["xattn_pre_norm"][l]), g4=row(small["xattn_post_norm"][l]),
            g5=row(small["ffn_pre_norm"][l]), g6=row(small["ffn_post_norm"][l]), wc=small["ffn_conv_w"][l])
        (z, ycat, y1, x1, v1, cv, pooled), got = _mix_fwd(
            x, sp["g1"], w["wint"], sp["maps"], sp["scale"], sp["dww"], sp["dwb"], sp["lng"], sp["lnb"], sp["sw"],
            w["wout"], sp["g2"], plan("mix"))
        landed("mix", got)
        k, v = _kv_fwd(mem, gmem, w["wk"], w["wv"])
        (q, o, y2, x2), got = _xattn_fwd(x1, sp["g3"], w["wq"], k, v, w["wo"], sp["g4"], plan("xattn"))
        landed("xattn", got)
        (u, c, a, y3, x3), got = _ffn_fwd(x2, sp["g5"], w["wupt"], sp["wc"], w["wdown"], sp["g6"], plan("ffn"))
        landed("ffn", got)
        if nxt:
            big.append(nxt)
        saved.append(dict(sp=sp, x=x, z=z, ycat=ycat, y1=y1, x1=x1, v1=v1, cv=cv, pooled=pooled, k=k, v=v, q=q, o=o,
                          y2=y2, x2=x2, u=u, c=c, a=a, y3=y3))
        x = x3

    dx, loss_blk = _loss_grad(x, target)
    big_grads = [None] * depth
    sg = {n: [None] * depth for n in ("mix_pre_norm", "mix_post_norm", "pool_maps", "pool_scale", "conf_dw_w",
                                      "conf_dw_b", "conf_ln_g", "conf_ln_b", "sconv_w", "xattn_pre_norm",
                                      "xattn_post_norm", "ffn_pre_norm", "ffn_post_norm", "ffn_conv_w")}
    dgmem = None
    pending = None
    for l in reversed(range(depth)):
        w, s = big[l], saved[l]
        sp = s["sp"]
        comm = _Scatter(pending) if distributed and pending is not None else None
        (dx, du, dy3, h3, dg5, dg6, dwc), got = _ffn_bwd(dx, s["y3"], s["x2"], s["u"], s["c"], sp["g5"], w["wupt"],
                                                        sp["wc"], w["wdown"], sp["g6"], comm)
        if comm is not None:
            big_grads[l + 1] = (big_grads[l + 1], got[0])
        g = dict(wupt=_tn_matmul(du, h3), wdown=_tn_matmul(s["a"], dy3))
        dx, g["wq"], g["wo"], dk, dv, dg3, dg4 = _xattn_bwd(dx, s["y2"], s["x1"], s["q"], s["o"], sp["g3"], w["wq"],
                                                            s["k"], s["v"], w["wo"], sp["g4"])
        g["wk"], g["wv"], dgm = _kv_bwd(mem, gmem, dk, dv, w["wk"], w["wv"])
        dgmem = dgm if dgmem is None else dgmem + dgm
        comm = _Scatter([g[key] for key in _SCATTER_LATE]) if distributed else None
        (dx, g["wint"], g["wout"], dg1, dg2, dmaps, dscale, ddww, misc), got = _mix_bwd(
            dx, s["y1"], s["x"], s["z"], s["ycat"], s["v1"], s["cv"], s["pooled"], sp["g1"], w["wint"], sp["maps"],
            sp["scale"], sp["dww"], sp["lng"], sp["lnb"], sp["sw"], w["wout"], sp["g2"], comm)
        if distributed:
            big_grads[l] = got[0]
            pending = [g[key] for key in _SCATTER_NEXT]
        else:
            big_grads[l] = g
        sg["mix_pre_norm"][l] = dg1[0]
        sg["mix_post_norm"][l] = dg2[0]
        sg["pool_maps"][l] = jnp.stack([dmaps[i * 64:(i + 1) * 64, i * 64:(i + 1) * 64] for i in range(4)])
        sg["pool_scale"][l] = dscale[0]
        sg["conf_dw_w"][l] = ddww[0:CONF_K]
        sg["conf_dw_b"][l] = misc[0]
        sg["conf_ln_g"][l] = misc[1]
        sg["conf_ln_b"][l] = misc[2]
        sg["sconv_w"][l] = misc[3:6]
        sg["xattn_pre_norm"][l] = dg3[0]
        sg["xattn_post_norm"][l] = dg4[0]
        sg["ffn_pre_norm"][l] = dg5[0]
        sg["ffn_post_norm"][l] = dg6[0]
        sg["ffn_conv_w"][l] = dwc[0:3]
    if distributed:
        big_grads[0] = (big_grads[0], _comm_only(_Scatter(pending))[0])
    small_grads = {n: jnp.stack(vs) for n, vs in sg.items()}
    small_grads["mem_norm"] = dgmem[0]
    return loss_blk, dx, big_grads, small_grads


_WEIGHTS = ['mem_norm', 'mix_pre_norm', 'mix_post_norm', 'w_in', 'pool_maps', 'pool_scale', 'conf_dw_w', 'conf_dw_b',
            'conf_ln_g', 'conf_ln_b', 'sconv_w', 'w_out', 'xattn_pre_norm', 'xattn_post_norm', 'xattn_wq',
            'xattn_wk', 'xattn_wv', 'xattn_wo', 'ffn_pre_norm', 'ffn_post_norm', 'ffn_w_up', 'ffn_conv_w',
            'ffn_w_down']
_CHANNEL_SHARDED = ('conf_dw_w', 'sconv_w', 'ffn_conv_w')
_SMALL = [n for n in _WEIGHTS if n not in _BIG]


def kernel(x, mem, mem_norm, mix_pre_norm, mix_post_norm, w_in, pool_maps, pool_scale, conf_dw_w, conf_dw_b, conf_ln_g, conf_ln_b, sconv_w, w_out, xattn_pre_norm, xattn_post_norm, xattn_wq, xattn_wk, xattn_wv, xattn_wo, ffn_pre_norm, ffn_post_norm, ffn_w_up, ffn_conv_w, ffn_w_down, loss_target, m_mem_norm, m_mix_pre_norm, m_mix_post_norm, m_w_in, m_pool_maps, m_pool_scale, m_conf_dw_w, m_conf_dw_b, m_conf_ln_g, m_conf_ln_b, m_sconv_w, m_w_out, m_xattn_pre_norm, m_xattn_post_norm, m_xattn_wq, m_xattn_wk, m_xattn_wv, m_xattn_wo, m_ffn_pre_norm, m_ffn_post_norm, m_ffn_w_up, m_ffn_conv_w, m_ffn_w_down, v_mem_norm, v_mix_pre_norm, v_mix_post_norm, v_w_in, v_pool_maps, v_pool_scale, v_conf_dw_w, v_conf_dw_b, v_conf_ln_g, v_conf_ln_b, v_sconv_w, v_w_out, v_xattn_pre_norm, v_xattn_post_norm, v_xattn_wq, v_xattn_wk, v_xattn_wv, v_xattn_wo, v_ffn_pre_norm, v_ffn_post_norm, v_ffn_w_up, v_ffn_conv_w, v_ffn_w_down):
    wts = dict(mem_norm=mem_norm, mix_pre_norm=mix_pre_norm, mix_post_norm=mix_post_norm, w_in=w_in,
               pool_maps=pool_maps, pool_scale=pool_scale, conf_dw_w=conf_dw_w, conf_dw_b=conf_dw_b,
               conf_ln_g=conf_ln_g, conf_ln_b=conf_ln_b, sconv_w=sconv_w, w_out=w_out,
               xattn_pre_norm=xattn_pre_norm, xattn_post_norm=xattn_post_norm, xattn_wq=xattn_wq,
               xattn_wk=xattn_wk, xattn_wv=xattn_wv, xattn_wo=xattn_wo, ffn_pre_norm=ffn_pre_norm,
               ffn_post_norm=ffn_post_norm, ffn_w_up=ffn_w_up, ffn_conv_w=ffn_conv_w, ffn_w_down=ffn_w_down)
    mom_m = dict(mem_norm=m_mem_norm, mix_pre_norm=m_mix_pre_norm, mix_post_norm=m_mix_post_norm, w_in=m_w_in,
                 pool_maps=m_pool_maps, pool_scale=m_pool_scale, conf_dw_w=m_conf_dw_w, conf_dw_b=m_conf_dw_b,
                 conf_ln_g=m_conf_ln_g, conf_ln_b=m_conf_ln_b, sconv_w=m_sconv_w, w_out=m_w_out,
                 xattn_pre_norm=m_xattn_pre_norm, xattn_post_norm=m_xattn_post_norm, xattn_wq=m_xattn_wq,
                 xattn_wk=m_xattn_wk, xattn_wv=m_xattn_wv, xattn_wo=m_xattn_wo, ffn_pre_norm=m_ffn_pre_norm,
                 ffn_post_norm=m_ffn_post_norm, ffn_w_up=m_ffn_w_up, ffn_conv_w=m_ffn_conv_w,
                 ffn_w_down=m_ffn_w_down)
    mom_v = dict(mem_norm=v_mem_norm, mix_pre_norm=v_mix_pre_norm, mix_post_norm=v_mix_post_norm, w_in=v_w_in,
                 pool_maps=v_pool_maps, pool_scale=v_pool_scale, conf_dw_w=v_conf_dw_w, conf_dw_b=v_conf_dw_b,
                 conf_ln_g=v_conf_ln_g, conf_ln_b=v_conf_ln_b, sconv_w=v_sconv_w, w_out=v_w_out,
                 xattn_pre_norm=v_xattn_pre_norm, xattn_post_norm=v_xattn_post_norm, xattn_wq=v_xattn_wq,
                 xattn_wk=v_xattn_wk, xattn_wv=v_xattn_wv, xattn_wo=v_xattn_wo, ffn_pre_norm=v_ffn_pre_norm,
                 ffn_post_norm=v_ffn_post_norm, ffn_w_up=v_ffn_w_up, ffn_conv_w=v_ffn_conv_w,
                 ffn_w_down=v_ffn_w_down)
    depth = w_in.shape[0]
    chip = 2 * lax.axis_index("x") + lax.axis_index("y")

    stacked = {}
    for name, (key, transposed) in _BIG.items():
        w = wts[name]
        wb = _cast_bf16(w.reshape(-1, w.shape[-1])).reshape(w.shape)
        stacked[key] = wb.transpose(0, 2, 1) if transposed else wb
    shards = [{key: stacked[key][l] for key in _KEYS} for l in range(depth)]
    first = _comm_only(_Gather([shards[0][key] for key in _KEYS]))
    big0 = {key: g.reshape(-1, D) for key, g in zip(_KEYS, first)}

    conv_shapes = [wts[n].shape for n in _CHANNEL_SHARDED]
    conv_all = _allgather_devices(_pack([wts[n] for n in _CHANNEL_SHARDED]))
    per_chip = [_unpack(conv_all[2 * j], conv_shapes) for j in range(N_CHIPS)]
    small = {n: wts[n] for n in _SMALL}
    for i, n in enumerate(_CHANNEL_SHARDED):
        small[n] = jnp.concatenate([per_chip[j][i] for j in range(N_CHIPS)], axis=-1)

    loss_blk, dx, landings, small_grads = _local_step(x[0], mem[0], loss_target[0], small, [big0], shards)

    core_sums = [_sum_chips(buf) for pair in landings for buf in pair]
    sibling_sums = _sibling_swap(core_sums)
    totals = [_add2(mine, theirs) for mine, theirs in zip(core_sums, sibling_sums)]
    layout = {}
    for which, order in ((0, _SCATTER_LATE), (1, _SCATTER_NEXT)):
        off = 0
        for key in order:
            r = stacked[key].shape[1]
            layout[key] = (which, off, r)
            off += r
    grads = {}
    for name, (key, transposed) in _BIG.items():
        which, off, r = layout[key]
        g = jnp.stack([totals[2 * l + which][off:off + r] for l in range(depth)])
        grads[name] = g.transpose(0, 2, 1) if transposed else g

    small_shapes = [(128,)] + [small[n].shape for n in _SMALL]
    partial = _pack([loss_blk[0]] + [small_grads[n] for n in _SMALL])
    total = _unpack(_sum_devices(_allgather_devices(partial)), small_shapes)
    loss = total[0][0]
    for n, g in zip(_SMALL, total[1:]):
        if n in _CHANNEL_SHARDED:
            width = wts[n].shape[-1]
            g = lax.dynamic_slice_in_dim(g, chip * width, width, axis=-1)
        grads[n] = g

    delta, new_m, new_v = {}, {}, {}
    for name in _BIG:
        shp = wts[name].shape
        flat = lambda a: a.reshape(-1, shp[-1])
        d, nm, nv = _adamw(flat(wts[name]), flat(grads[name]), flat(mom_m[name]), flat(mom_v[name]))
        delta[name], new_m[name], new_v[name] = d.reshape(shp), nm.reshape(shp), nv.reshape(shp)
    shapes = [wts[n].shape for n in _SMALL]
    d, nm, nv = _adamw(_pack([wts[n] for n in _SMALL]), _pack([grads[n] for n in _SMALL]),
                       _pack([mom_m[n] for n in _SMALL]), _pack([mom_v[n] for n in _SMALL]))
    for out, packed in ((delta, d), (new_m, nm), (new_v, nv)):
        for n, a in zip(_SMALL, _unpack(packed, shapes)):
            out[n] = a

    return (loss, dx[None], *[grads[n] for n in _WEIGHTS], *[delta[n] for n in _WEIGHTS],
            *[new_m[n] for n in _WEIGHTS], *[new_v[n] for n in _WEIGHTS])
```

```python
import jax
import jax.numpy as jnp
from jax import lax
from jax.experimental import pallas as pl
from jax.experimental.pallas import tpu as pltpu

F32 = jnp.float32
BF16 = jnp.bfloat16

EPS = 1e-6
D = 1024
D_POOL, D_CONF, D_SCONV = 256, 384, 384
D_IN = D_POOL + 2 * D_CONF + 3 * D_SCONV
D_FF = 2816
FF_CHUNK = 1408
HEADS, HEAD_DIM = 4, 256
CONF_K = 31
POOL_WINDOWS = (2, 4, 8, 16)
POOL_GROUP = 64
SUBLANES = 8
HALO = 32
PHASE_ROWS = HALO - SUBLANES
TAP_ROWS = 64
FHALO = 16
TM_FWD = 512
TM_BWD = 256
N_CHIPS = 4
N_DEV = 8
MESH_ID = pl.DeviceIdType.MESH
VMEM_LIMIT = 56 << 20

ADAM_LR, ADAM_B1, ADAM_B2, ADAM_EPS, ADAM_WD, ADAM_STEP = 0.001, 0.9, 0.999, 1e-08, 0.01, 10

C_P = (0, 256)
C_A = (256, 640)
C_G = (640, 1024)
C_B = (1024, 1408)
C_C = (1408, 1792)
C_X = (1792, 2176)


def _nn(a, b):
    return jnp.dot(a, b, preferred_element_type=F32)


def _nt(a, b):
    return lax.dot_general(a, b, (((1,), (1,)), ((), ())), preferred_element_type=F32)


def _tn(a, b):
    return lax.dot_general(a, b, (((0,), (0,)), ((), ())), preferred_element_type=F32)


def _sigmoid(v):
    return 1.0 / (1.0 + jnp.exp(-v))


def _rms(v, g):
    r = lax.rsqrt(jnp.mean(v * v, axis=-1, keepdims=True) + EPS)
    vh = v * r
    return vh * g, vh, r


def _rms_bwd(vh, r, g, dy):
    dvh = dy * g
    dv = r * (dvh - vh * jnp.mean(dvh * vh, axis=-1, keepdims=True))
    return dv, jnp.sum(dy * vh, axis=0, keepdims=True)


def _colsum(v):
    return jnp.sum(v, axis=0, keepdims=True)


def _rows(tm, n, nt=None):
    if nt is None:
        return pl.BlockSpec((tm, n), lambda i: (i, 0))
    return pl.BlockSpec((tm, n), lambda i: (nt - 1 - i, 0))


def _const(shape):
    nd = len(shape)
    return pl.BlockSpec(shape, lambda i: (0,) * nd, pipeline_mode=pl.Buffered(1))


def _acc(shape):
    nd = len(shape)
    return pl.BlockSpec(shape, lambda i: (0,) * nd)


def _sds(shape, dtype):
    return jax.ShapeDtypeStruct(shape, dtype)


_HBM = pl.BlockSpec(memory_space=pltpu.HBM)


def _mesh_pos():
    return lax.axis_index("x"), lax.axis_index("y"), lax.axis_index("c")


def _chip_peers():
    x, y, c = _mesh_pos()
    flips = [(1 - x, y), (x, 1 - y), (1 - x, 1 - y)]
    return 2 * x + y, [((px, py, c), 2 * px + py) for px, py in flips]


def _remote(src, dst, send_sems, recv_sems, idx, dev):
    return pltpu.make_async_remote_copy(src_ref=src, dst_ref=dst, send_sem=send_sems.at[idx],
                                        recv_sem=recv_sems.at[idx], device_id=dev, device_id_type=MESH_ID)


class _Plan:
    def __init__(self, arrays):
        self.arrays = list(arrays)

    def scratch(self):
        n = len(self.arrays)
        return [pltpu.SemaphoreType.DMA((3 * n,)), pltpu.SemaphoreType.DMA((3 * n,)),
                pltpu.SemaphoreType.DMA((n,))]

    def _copies(self, ins, outs, sems):
        send_sems, recv_sems, local_sems = sems
        me, peers = _chip_peers()
        own, sends, recvs = [], [], []
        for k in range(len(ins)):
            own.append(pltpu.make_async_copy(self.src(ins, k, me), self.dst(outs, k, me), local_sems.at[k]))
            for j, (dev, chip) in enumerate(peers):
                sends.append(_remote(self.src(ins, k, chip), self.dst(outs, k, me), send_sems, recv_sems,
                                     3 * k + j, dev))
                recvs.append(_remote(self.src(ins, k, me), self.dst(outs, k, chip), send_sems, recv_sems,
                                     3 * k + j, dev))
        return own, sends, recvs

    def start(self, ins, outs, sems):
        own, sends, _ = self._copies(ins, outs, sems)
        for cp in own + sends:
            cp.start()

    def wait(self, ins, outs, sems):
        own, sends, recvs = self._copies(ins, outs, sems)
        for cp in recvs:
            cp.wait_recv()
        for cp in sends:
            cp.wait_send()
        for cp in own:
            cp.wait()


class _Gather(_Plan):
    tag = "gather"

    def out_shapes(self):
        return [_sds((N_CHIPS,) + a.shape, a.dtype) for a in self.arrays]

    def src(self, ins, k, chip):
        return ins[k]

    def dst(self, outs, k, chip):
        return outs[k].at[chip]


class _Scatter(_Plan):
    tag = "scatter"

    def __init__(self, arrays):
        super().__init__(arrays)
        self.rows = [a.shape[0] // N_CHIPS for a in self.arrays]
        self.offs = [sum(self.rows[:k]) for k in range(len(self.rows))]

    def out_shapes(self):
        a = self.arrays[0]
        return [_sds((N_CHIPS, sum(self.rows), a.shape[1]), a.dtype)]

    def src(self, ins, k, chip):
        r = self.rows[k]
        return ins[k].at[pl.ds(pl.multiple_of(chip * r, 16), r), :]

    def dst(self, outs, k, chip):
        return outs[0].at[chip, pl.ds(self.offs[k], self.rows[k]), :]


def _fused_call(name, body, grid, in_specs, out_specs, out_shape, scratch, args, comm=None, sem=("arbitrary",)):
    n_in, n_out, n_scr = len(in_specs), len(out_specs), len(scratch)
    c_in = comm.arrays if comm else []
    c_out = comm.out_shapes() if comm else []
    c_scr = comm.scratch() if comm else []

    def kernel_fn(*refs):
        ins, cins = refs[:n_in], refs[n_in:n_in + len(c_in)]
        o0 = n_in + len(c_in)
        outs, couts = refs[o0:o0 + n_out], refs[o0 + n_out:o0 + n_out + len(c_out)]
        s0 = o0 + n_out + len(c_out)
        scr, csems = refs[s0:s0 + n_scr], refs[s0 + n_scr:]
        if comm:
            body(ins, outs, scr, lambda: comm.start(cins, couts, csems), lambda: comm.wait(cins, couts, csems))
        else:
            body(ins, outs, scr, None, None)

    res = pl.pallas_call(
        kernel_fn, grid=grid, name=name + ("_" + comm.tag if comm else ""),
        in_specs=list(in_specs) + [_HBM] * len(c_in), out_specs=list(out_specs) + [_HBM] * len(c_out),
        out_shape=list(out_shape) + c_out, scratch_shapes=list(scratch) + c_scr,
        compiler_params=pltpu.CompilerParams(dimension_semantics=sem, vmem_limit_bytes=VMEM_LIMIT),
    )(*args, *c_in)
    return res[:n_out], res[n_out:]


def _bracket(start, wait, first, last):
    if start is not None:
        pl.when(first)(start)

    def finish():
        if wait is not None:
            pl.when(last)(wait)
    return finish


def _comm_only(plan):
    n_in, n_out = len(plan.arrays), len(plan.out_shapes())

    def body(*refs):
        ins, outs, sems = refs[:n_in], refs[n_in:n_in + n_out], refs[n_in + n_out:]
        plan.start(ins, outs, sems)
        plan.wait(ins, outs, sems)

    return pl.pallas_call(
        body, name=plan.tag + "_chips", in_specs=[_HBM] * n_in, out_specs=[_HBM] * n_out,
        out_shape=plan.out_shapes(), scratch_shapes=plan.scratch(),
    )(*plan.arrays)


def _sibling_swap(arrs):
    n = len(arrs)

    def body(*refs):
        ins, outs = refs[:n], refs[n:2 * n]
        send_sems, recv_sems = refs[2 * n:]
        x, y, c = _mesh_pos()
        cps = [_remote(ins[k], outs[k], send_sems, recv_sems, k, (x, y, 1 - c)) for k in range(n)]
        for cp in cps:
            cp.start()
        for cp in cps:
            cp.wait()

    return pl.pallas_call(
        body, name="sibling_swap", in_specs=[_HBM] * n, out_specs=[_HBM] * n,
        out_shape=[_sds(a.shape, a.dtype) for a in arrs],
        scratch_shapes=[pltpu.SemaphoreType.DMA((n,)), pltpu.SemaphoreType.DMA((n,))],
    )(*arrs)


def _allgather_devices(v):
    m_per, n = v.shape

    def body(v_ref, out_ref, send_sems, recv_sems, local_sem):
        x, y, c = _mesh_pos()
        me, sibling = (x, y, c), (x, y, 1 - c)
        chips = [(1 - x, y), (x, 1 - y), (1 - x, 1 - y)]

        def rows(px, py, pc):
            return out_ref.at[4 * px + 2 * py + pc]

        def copy(k, block, to, src=None):
            return _remote(rows(*block) if src is None else src, rows(*block), send_sems, recv_sems, k, to)

        mine = pltpu.make_async_copy(v_ref, rows(*me), local_sem)
        mine.start()
        first = [copy(0, me, sibling, src=v_ref)]
        first += [copy(1 + j, me, (*chip, c), src=v_ref) for j, chip in enumerate(chips)]
        for cp in first:
            cp.start()
        passed = [copy(4 + j, (*chip, c), sibling) for j, chip in enumerate(chips)]
        for j, chip in enumerate(chips):
            copy(1 + j, (*chip, c), me).wait_recv()
            passed[j].start()
        copy(0, sibling, me).wait_recv()
        for j, chip in enumerate(chips):
            copy(4 + j, (*chip, 1 - c), me).wait_recv()
        for cp in first + passed:
            cp.wait_send()
        mine.wait()

    return pl.pallas_call(
        body, name="allgather_devices", out_shape=_sds((N_DEV, m_per, n), v.dtype),
        in_specs=[pl.BlockSpec(memory_space=pltpu.VMEM)], out_specs=pl.BlockSpec(memory_space=pltpu.VMEM),
        scratch_shapes=[pltpu.SemaphoreType.DMA((7,)), pltpu.SemaphoreType.DMA((7,)), pltpu.SemaphoreType.DMA],
        compiler_params=pltpu.CompilerParams(vmem_limit_bytes=VMEM_LIMIT),
    )(v)


def _pool_lane():
    return lax.broadcasted_iota(jnp.int32, (1, D_POOL), 1)


def _pool_count(t0, tm):
    lane = _pool_lane()
    w = jnp.where(lane < 64, 2, jnp.where(lane < 128, 4, jnp.where(lane < 192, 8, 16)))
    pos1 = lax.broadcasted_iota(jnp.int32, (tm, D_POOL), 0) + (t0 + 1)
    return jnp.minimum(pos1, w).astype(F32)


def _fill_bands(band_ref, tm, causal):
    r = lax.broadcasted_iota(jnp.int32, (tm, tm + HALO), 0)
    s = lax.broadcasted_iota(jnp.int32, (tm, tm + HALO), 1)
    d = (r + HALO - s) if causal else (s - r)
    for g, w in enumerate(POOL_WINDOWS):
        band_ref[g] = jnp.where((d >= 0) & (d < w), 1.0, 0.0).astype(BF16)


def _window_sums(band_ref, operand, width):
    lane = _pool_lane()
    res = None
    for g in range(len(POOL_WINDOWS)):
        r = _nn(band_ref[g], operand)
        acc = r[:, 0:width]
        for c0 in range(width, r.shape[1], width):
            acc = acc + r[:, c0:c0 + width]
        res = acc if res is None else jnp.where(lane >= POOL_GROUP * g, acc, res)
    return res


def _phase_copies(src, phases, tm):
    for b in range(1, SUBLANES):
        phases[b - 1] = src[b:b + tm + PHASE_ROWS, :]


def _tap(src, phases, off, rows, r0=0):
    a, b = divmod(off, SUBLANES)
    lo = SUBLANES * a + r0
    if b == 0:
        return src[lo:lo + rows, :]
    return phases[b - 1, lo:lo + rows, :]


def _layer_norm_stats(v1):
    mu = jnp.mean(v1, axis=-1, keepdims=True)
    xc = v1 - mu
    rs = lax.rsqrt(jnp.mean(xc * xc, axis=-1, keepdims=True) + EPS)
    return xc * rs, rs


def _mix_fwd(x, g1, wint, maps_bd, scale, dww, dwb, lng, lnb, sw, wout, g2, comm=None):
    t_len = x.shape[0]
    tm = min(TM_FWD, t_len)
    nt = t_len // tm
    h0 = HALO

    def body(ins, outs, scr, start, wait):
        (x_ref, g1_ref, wint_ref, maps_ref, scale_ref, dww_ref, dwb_ref, lng_ref, lnb_ref, sw_ref, wout_ref,
         g2_ref) = ins
        z_ref, ycat_ref, y_ref, x1_ref, v1_ref, cv_ref, pooled_ref = outs
        pbuf, vbuf, sbuf, phases, band = scr
        i = pl.program_id(0)
        finish = _bracket(start, wait, i == 0, i == nt - 1)

        @pl.when(i == 0)
        def _():
            pbuf[0:h0, :] = jnp.zeros((h0, D_POOL), F32)
            vbuf[0:h0, :] = jnp.zeros((h0, D_CONF), F32)
            sbuf[0:h0, :] = jnp.zeros((h0, D_SCONV), F32)
            _fill_bands(band, tm, True)

        @pl.when(i > 0)
        def _():
            pbuf[0:h0, :] = pbuf[tm:tm + h0, :]
            vbuf[0:h0, :] = vbuf[tm:tm + h0, :]
            sbuf[0:h0, :] = sbuf[tm:tm + h0, :]

        xv = x_ref[...]
        h, _, _ = _rms(xv, g1_ref[...])
        z = _nt(h.astype(BF16), wint_ref[...])
        z_ref[...] = z.astype(BF16)
        zp = z[:, C_P[0]:C_P[1]]
        pbuf[h0:h0 + tm, :] = zp
        vbuf[h0:h0 + tm, :] = z[:, C_A[0]:C_A[1]] * _sigmoid(z[:, C_G[0]:C_G[1]])
        sbuf[h0:h0 + tm, :] = z[:, C_C[0]:C_C[1]] * z[:, C_X[0]:C_X[1]]

        pv = pbuf[...]
        hi = pv.astype(BF16)
        lo = (pv - hi.astype(F32)).astype(BF16)
        sums = _window_sums(band, jnp.concatenate([hi, lo], axis=1), D_POOL)
        pooled = (sums / _pool_count(i * tm, tm) - zp).astype(BF16)
        pooled_ref[...] = pooled
        ycat_ref[:, 0:D_POOL] = (_nn(pooled, maps_ref[...]) * scale_ref[...]).astype(BF16)

        _phase_copies(vbuf, phases, tm)
        base = h0 - (CONF_K - 1)
        for r0 in range(0, tm, TAP_ROWS):
            v1 = dww_ref[0:1, :] * _tap(vbuf, phases, base, TAP_ROWS, r0)
            for j in range(1, CONF_K):
                v1 = v1 + dww_ref[j:j + 1, :] * _tap(vbuf, phases, base + j, TAP_ROWS, r0)
            v1 = v1 + dwb_ref[...]
            v1_ref[r0:r0 + TAP_ROWS, :] = v1
            vh, _ = _layer_norm_stats(v1)
            v2 = vh * lng_ref[...] + lnb_ref[...]
            ycat_ref[r0:r0 + TAP_ROWS, D_POOL:D_POOL + D_CONF] = (v2 * _sigmoid(v2)).astype(BF16)

        cv = (sw_ref[0:1, :] * sbuf[h0 - 2:h0 - 2 + tm, :] + sw_ref[1:2, :] * sbuf[h0 - 1:h0 - 1 + tm, :]
              + sw_ref[2:3, :] * sbuf[h0:h0 + tm, :])
        cv_ref[...] = cv
        ycat_ref[:, D_POOL + D_CONF:D] = (z[:, C_B[0]:C_B[1]] * cv).astype(BF16)

        yb = _nn(ycat_ref[...], wout_ref[...]).astype(BF16)
        y_ref[...] = yb
        yn, _, _ = _rms(yb.astype(F32), g2_ref[...])
        x1_ref[...] = xv + yn
        finish()

    return _fused_call(
        "mix_fwd", body, (nt,),
        [_rows(tm, D), _const((1, D)), _const((D_IN, D)), _const((D_POOL, D_POOL)), _const((1, D_POOL)),
         _const((CONF_K, D_CONF)), _const((1, D_CONF)), _const((1, D_CONF)), _const((1, D_CONF)),
         _const((3, D_SCONV)), _const((D, D)), _const((1, D))],
        [_rows(tm, D_IN), _rows(tm, D), _rows(tm, D), _rows(tm, D), _rows(tm, D_CONF), _rows(tm, D_SCONV),
         _rows(tm, D_POOL)],
        [_sds((t_len, D_IN), BF16), _sds((t_len, D), BF16), _sds((t_len, D), BF16), _sds((t_len, D), F32),
         _sds((t_len, D_CONF), F32), _sds((t_len, D_SCONV), F32), _sds((t_len, D_POOL), BF16)],
        [pltpu.VMEM((h0 + tm, D_POOL), F32), pltpu.VMEM((h0 + tm, D_CONF), F32),
         pltpu.VMEM((h0 + tm, D_SCONV), F32), pltpu.VMEM((SUBLANES - 1, tm + PHASE_ROWS, D_CONF), F32),
         pltpu.VMEM((len(POOL_WINDOWS), tm, tm + h0), BF16)],
        (x, g1, wint, maps_bd, scale, dww, dwb, lng, lnb, sw, wout, g2), comm)


def _mix_bwd(dx1, y, x, z, ycat, v1, cv, pooled, g1, wint, maps_bd, scale, dww, lng, lnb, sw, wout, g2, comm=None):
    t_len = x.shape[0]
    tm = min(TM_BWD, t_len)
    nt = t_len // tm
    h0 = HALO

    def body(ins, outs, scr, start, wait):
        (dx1_ref, y_ref, x_ref, z_ref, ycat_ref, v1_ref, cv_ref, pooled_ref, g1_ref, wint_ref, maps_ref, scale_ref,
         dww_ref, lng_ref, lnb_ref, sw_ref, wout_ref, g2_ref) = ins
        dx_ref, dwin_ref, dwout_ref, dg1_ref, dg2_ref, dmaps_ref, dscale_ref, ddww_ref, misc_ref = outs
        ebuf, dvbuf, dcbuf, phases, band, dzbuf, acc_in, acc_out, ddacc = scr
        s = pl.program_id(0)
        ti = nt - 1 - s
        finish = _bracket(start, wait, s == 0, s == nt - 1)

        @pl.when(s == 0)
        def _():
            ebuf[tm:tm + h0, :] = jnp.zeros((h0, D_POOL), F32)
            dvbuf[tm:tm + h0, :] = jnp.zeros((h0, D_CONF), F32)
            dcbuf[tm:tm + h0, :] = jnp.zeros((h0, D_SCONV), F32)
            _fill_bands(band, tm, False)
            acc_in[...] = jnp.zeros_like(acc_in)
            acc_out[...] = jnp.zeros_like(acc_out)
            ddacc[...] = jnp.zeros_like(ddacc)
            dg1_ref[...] = jnp.zeros_like(dg1_ref)
            dg2_ref[...] = jnp.zeros_like(dg2_ref)
            dmaps_ref[...] = jnp.zeros_like(dmaps_ref)
            dscale_ref[...] = jnp.zeros_like(dscale_ref)
            ddww_ref[...] = jnp.zeros_like(ddww_ref)
            misc_ref[...] = jnp.zeros_like(misc_ref)

        @pl.when(s > 0)
        def _():
            ebuf[tm:tm + h0, :] = ebuf[0:h0, :]
            dvbuf[tm:tm + h0, :] = dvbuf[0:h0, :]
            dcbuf[tm:tm + h0, :] = dcbuf[0:h0, :]

        yv = y_ref[...].astype(F32)
        _, yh, yr = _rms(yv, g2_ref[...])
        dy, dg2 = _rms_bwd(yh, yr, g2_ref[...], dx1_ref[...])
        dg2_ref[...] += dg2
        dyb = dy.astype(BF16)
        acc_out[...] += _tn(ycat_ref[...], dyb)
        dycat = _nt(dyb, wout_ref[...])
        dya = dycat[:, 0:D_POOL]
        dyb2 = dycat[:, D_POOL:D_POOL + D_CONF]
        dyc = dycat[:, D_POOL + D_CONF:D]

        pooled_v = pooled_ref[...]
        pm = _nn(pooled_v, maps_ref[...])
        dscale_ref[...] += _colsum(dya * pm)
        dq = (dya * scale_ref[...]).astype(BF16)
        dmaps_ref[...] += _tn(pooled_v, dq)
        dpooled = _nt(dq, maps_ref[...])
        ebuf[0:tm, :] = dpooled / _pool_count(ti * tm, tm)
        dzbuf[:, C_P[0]:C_P[1]] = (_window_sums(band, ebuf[...].astype(BF16), D_POOL) - dpooled).astype(BF16)

        vh, rs = _layer_norm_stats(v1_ref[...])
        v2 = vh * lng_ref[...] + lnb_ref[...]
        s2 = _sigmoid(v2)
        dv2 = dyb2 * (s2 * (1.0 + v2 * (1.0 - s2)))
        misc_ref[1:2, :] += _colsum(dv2 * vh)
        misc_ref[2:3, :] += _colsum(dv2)
        dvh = dv2 * lng_ref[...]
        dv1 = rs * (dvh - jnp.mean(dvh, axis=-1, keepdims=True) - vh * jnp.mean(dvh * vh, axis=-1, keepdims=True))
        misc_ref[0:1, :] += _colsum(dv1)
        dvbuf[0:tm, :] = dv1
        _phase_copies(dvbuf, phases, tm)
        for r0 in range(0, tm, TAP_ROWS):
            blk = slice(r0, r0 + TAP_ROWS)
            za = z_ref[blk, C_A[0]:C_A[1]].astype(F32)
            sg = _sigmoid(z_ref[blk, C_G[0]:C_G[1]].astype(F32))
            v0 = za * sg
            dv0 = None
            for k in range(CONF_K):
                j = CONF_K - 1 - k
                dk = _tap(dvbuf, phases, k, TAP_ROWS, r0)
                prod = v0 * dk
                part = prod[0:SUBLANES, :]
                for q in range(SUBLANES, TAP_ROWS, SUBLANES):
                    part = part + prod[q:q + SUBLANES, :]
                ddacc[SUBLANES * j:SUBLANES * (j + 1), :] += part
                term = dww_ref[j:j + 1, :] * dk
                dv0 = term if dv0 is None else dv0 + term
            dzbuf[blk, C_A[0]:C_A[1]] = (dv0 * sg).astype(BF16)
            dzbuf[blk, C_G[0]:C_G[1]] = (dv0 * za * sg * (1.0 - sg)).astype(BF16)

        zb = z_ref[:, C_B[0]:C_B[1]].astype(F32)
        zc = z_ref[:, C_C[0]:C_C[1]].astype(F32)
        zx = z_ref[:, C_X[0]:C_X[1]].astype(F32)
        pv = zc * zx
        dzbuf[:, C_B[0]:C_B[1]] = (dyc * cv_ref[...]).astype(BF16)
        dcbuf[0:tm, :] = dyc * zb
        dp = None
        for k in range(3):
            j = 2 - k
            dk = dcbuf[k:k + tm, :]
            misc_ref[3 + j:4 + j, :] += _colsum(pv * dk)
            term = sw_ref[j:j + 1, :] * dk
            dp = term if dp is None else dp + term
        dzbuf[:, C_C[0]:C_C[1]] = (dp * zx).astype(BF16)
        dzbuf[:, C_X[0]:C_X[1]] = (dp * zc).astype(BF16)

        xv = x_ref[...]
        h, xh, xr = _rms(xv, g1_ref[...])
        dz = dzbuf[...]
        acc_in[...] += _tn(dz, h.astype(BF16))
        dh = _nn(dz, wint_ref[...])
        dxn, dg1 = _rms_bwd(xh, xr, g1_ref[...], dh)
        dg1_ref[...] += dg1
        dx_ref[...] = dx1_ref[...] + dxn

        @pl.when(s == nt - 1)
        def _():
            dwin_ref[...] = acc_in[...].astype(BF16)
            dwout_ref[...] = acc_out[...].astype(BF16)
            for j in range(CONF_K):
                ddww_ref[j:j + 1, :] = _colsum(ddacc[SUBLANES * j:SUBLANES * (j + 1), :])

        finish()

    return _fused_call(
        "mix_bwd", body, (nt,),
        [_rows(tm, D, nt), _rows(tm, D, nt), _rows(tm, D, nt), _rows(tm, D_IN, nt), _rows(tm, D, nt),
         _rows(tm, D_CONF, nt), _rows(tm, D_SCONV, nt), _rows(tm, D_POOL, nt), _const((1, D)), _const((D_IN, D)),
         _const((D_POOL, D_POOL)), _const((1, D_POOL)), _const((CONF_K, D_CONF)), _const((1, D_CONF)),
         _const((1, D_CONF)), _const((3, D_SCONV)), _const((D, D)), _const((1, D))],
        [_rows(tm, D, nt), _acc((D_IN, D)), _acc((D, D)), _acc((1, D)), _acc((1, D)), _acc((D_POOL, D_POOL)),
         _acc((1, D_POOL)), _acc((32, D_CONF)), _acc((8, D_CONF))],
        [_sds((t_len, D), F32), _sds((D_IN, D), BF16), _sds((D, D), BF16), _sds((1, D), F32), _sds((1, D), F32),
         _sds((D_POOL, D_POOL), F32), _sds((1, D_POOL), F32), _sds((32, D_CONF), F32), _sds((8, D_CONF), F32)],
        [pltpu.VMEM((tm + h0, D_POOL), F32), pltpu.VMEM((tm + h0, D_CONF), F32),
         pltpu.VMEM((tm + h0, D_SCONV), F32), pltpu.VMEM((SUBLANES - 1, tm + PHASE_ROWS, D_CONF), F32),
         pltpu.VMEM((len(POOL_WINDOWS), tm, tm + h0), BF16), pltpu.VMEM((tm, D_IN), BF16),
         pltpu.VMEM((D_IN, D), F32), pltpu.VMEM((D, D), F32), pltpu.VMEM((SUBLANES * CONF_K, D_CONF), F32)],
        (dx1, y, x, z, ycat, v1, cv, pooled, g1, wint, maps_bd, scale, dww, lng, lnb, sw, wout, g2), comm)


def _kv_fwd(mem, gmem, wk, wv):
    def body(mem_ref, g_ref, wk_ref, wv_ref, k_ref, v_ref):
        mn, _, _ = _rms(mem_ref[...], g_ref[...])
        mnb = mn.astype(BF16)
        k_ref[...] = _nn(mnb, wk_ref[...]).astype(BF16)
        v_ref[...] = _nn(mnb, wv_ref[...]).astype(BF16)

    n = mem.shape[0]
    return pl.pallas_call(
        body, name="kv_fwd", out_shape=[_sds((n, D), BF16), _sds((n, D), BF16)],
        compiler_params=pltpu.CompilerParams(vmem_limit_bytes=VMEM_LIMIT),
    )(mem, gmem, wk, wv)


def _kv_bwd(mem, gmem, dk, dv, wk, wv):
    def body(mem_ref, g_ref, dk_ref, dv_ref, wk_ref, wv_ref, dwk_ref, dwv_ref, dg_ref):
        mn, mh, _ = _rms(mem_ref[...], g_ref[...])
        mnb = mn.astype(BF16)
        dkb = dk_ref[...].astype(BF16)
        dvb = dv_ref[...].astype(BF16)
        dwk_ref[...] = _tn(mnb, dkb).astype(BF16)
        dwv_ref[...] = _tn(mnb, dvb).astype(BF16)
        dmn = _nt(dkb, wk_ref[...]) + _nt(dvb, wv_ref[...])
        dg_ref[...] = _colsum(dmn * mh)

    return pl.pallas_call(
        body, name="kv_bwd", out_shape=[_sds((D, D), BF16), _sds((D, D), BF16), _sds((1, D), F32)],
        compiler_params=pltpu.CompilerParams(vmem_limit_bytes=VMEM_LIMIT),
    )(mem, gmem, dk, dv, wk, wv)


def _softmax_rows(s):
    e = jnp.exp(s - jnp.max(s, axis=-1, keepdims=True))
    return e / jnp.sum(e, axis=-1, keepdims=True)


def _xattn_fwd(x1, g3, wq, k, v, wo, g4, comm=None):
    t_len = x1.shape[0]
    tm = min(TM_FWD, t_len)
    nt = t_len // tm
    n_mem = k.shape[0]
    sc = HEAD_DIM ** -0.5

    def body(ins, outs, scr, start, wait):
        x_ref, g3_ref, wq_ref, k_ref, v_ref, wo_ref, g4_ref = ins
        q_ref, o_ref, y_ref, x2_ref = outs
        i = pl.program_id(0)
        finish = _bracket(start, wait, i == 0, i == nt - 1)
        xv = x_ref[...]
        h, _, _ = _rms(xv, g3_ref[...])
        qb = _nn(h.astype(BF16), wq_ref[...]).astype(BF16)
        q_ref[...] = qb
        for hd in range(HEADS):
            sl = slice(hd * HEAD_DIM, (hd + 1) * HEAD_DIM)
            p = _softmax_rows(_nt(qb[:, sl], k_ref[:, sl]) * sc)
            o_ref[:, sl] = _nn(p.astype(BF16), v_ref[:, sl]).astype(BF16)
        yb = _nn(o_ref[...], wo_ref[...]).astype(BF16)
        y_ref[...] = yb
        yn, _, _ = _rms(yb.astype(F32), g4_ref[...])
        x2_ref[...] = xv + yn
        finish()

    return _fused_call(
        "xattn_fwd", body, (nt,),
        [_rows(tm, D), _const((1, D)), _const((D, D)), _const((n_mem, D)), _const((n_mem, D)), _const((D, D)),
         _const((1, D))],
        [_rows(tm, D), _rows(tm, D), _rows(tm, D), _rows(tm, D)],
        [_sds((t_len, D), BF16), _sds((t_len, D), BF16), _sds((t_len, D), BF16), _sds((t_len, D), F32)],
        [], (x1, g3, wq, k, v, wo, g4), comm)


def _xattn_bwd(dx2, y, x1, q, o, g3, wq, k, v, wo, g4):
    t_len = x1.shape[0]
    tm = min(TM_FWD, t_len)
    nt = t_len // tm
    n_mem = k.shape[0]
    sc = HEAD_DIM ** -0.5

    def body(ins, outs, scr, start, wait):
        dx2_ref, y_ref, x_ref, q_ref, o_ref, g3_ref, wq_ref, k_ref, v_ref, wo_ref, g4_ref = ins
        dx_ref, dwq_ref, dwo_ref, dk_ref, dv_ref, dg3_ref, dg4_ref = outs
        dqbuf, acc_q, acc_o = scr
        s = pl.program_id(0)

        @pl.when(s == 0)
        def _():
            acc_q[...] = jnp.zeros_like(acc_q)
            acc_o[...] = jnp.zeros_like(acc_o)
            dk_ref[...] = jnp.zeros_like(dk_ref)
            dv_ref[...] = jnp.zeros_like(dv_ref)
            dg3_ref[...] = jnp.zeros_like(dg3_ref)
            dg4_ref[...] = jnp.zeros_like(dg4_ref)

        yv = y_ref[...].astype(F32)
        _, yh, yr = _rms(yv, g4_ref[...])
        dy, dg4 = _rms_bwd(yh, yr, g4_ref[...], dx2_ref[...])
        dg4_ref[...] += dg4
        dyb = dy.astype(BF16)
        acc_o[...] += _tn(o_ref[...], dyb)
        do = _nt(dyb, wo_ref[...])
        qb = q_ref[...]
        for hd in range(HEADS):
            sl = slice(hd * HEAD_DIM, (hd + 1) * HEAD_DIM)
            p = _softmax_rows(_nt(qb[:, sl], k_ref[:, sl]) * sc)
            dob = do[:, sl].astype(BF16)
            dp = _nt(dob, v_ref[:, sl])
            dv_ref[:, sl] += _tn(p.astype(BF16), dob)
            ds = (p * (dp - jnp.sum(dp * p, axis=-1, keepdims=True)) * sc).astype(BF16)
            dqbuf[:, sl] = _nn(ds, k_ref[:, sl]).astype(BF16)
            dk_ref[:, sl] += _tn(ds, qb[:, sl])
        xv = x_ref[...]
        h, xh, xr = _rms(xv, g3_ref[...])
        dq = dqbuf[...]
        acc_q[...] += _tn(h.astype(BF16), dq)
        dh = _nt(dq, wq_ref[...])
        dxn, dg3 = _rms_bwd(xh, xr, g3_ref[...], dh)
        dg3_ref[...] += dg3
        dx_ref[...] = dx2_ref[...] + dxn

        @pl.when(s == nt - 1)
        def _():
            dwq_ref[...] = acc_q[...].astype(BF16)
            dwo_ref[...] = acc_o[...].astype(BF16)

    outs, _ = _fused_call(
        "xattn_bwd", body, (nt,),
        [_rows(tm, D), _rows(tm, D), _rows(tm, D), _rows(tm, D), _rows(tm, D), _const((1, D)), _const((D, D)),
         _const((n_mem, D)), _const((n_mem, D)), _const((D, D)), _const((1, D))],
        [_rows(tm, D), _acc((D, D)), _acc((D, D)), _acc((n_mem, D)), _acc((n_mem, D)), _acc((1, D)), _acc((1, D))],
        [_sds((t_len, D), F32), _sds((D, D), BF16), _sds((D, D), BF16), _sds((n_mem, D), F32),
         _sds((n_mem, D), F32), _sds((1, D), F32), _sds((1, D), F32)],
        [pltpu.VMEM((tm, D), BF16), pltpu.VMEM((D, D), F32), pltpu.VMEM((D, D), F32)],
        (dx2, y, x1, q, o, g3, wq, k, v, wo, g4))
    return outs


def _ffn_cols(half, part):
    c0 = part * D_FF + half * FF_CHUNK
    return c0, c0 + FF_CHUNK


def _fill_shifts(ref, tm, step):
    t = lax.broadcasted_iota(jnp.int32, (tm, tm), 0)
    s = lax.broadcasted_iota(jnp.int32, (tm, tm), 1)
    ref[0:tm, :] = jnp.where(s == t + step, 1.0, 0.0).astype(BF16)
    ref[tm:2 * tm, :] = jnp.where(s == t + 2 * step, 1.0, 0.0).astype(BF16)


def _edge_terms(near, far, edge, inner):
    r = lax.broadcasted_iota(jnp.int32, (16, near.shape[1]), 0)
    return jnp.where(r == edge, near, 0.0), jnp.where(r == edge, far, jnp.where(r == inner, near, 0.0))


def _ffn_fwd(x2, g5, wupt, wc, wdown, g6, comm=None):
    t_len = x2.shape[0]
    tm = min(TM_BWD, t_len)
    nt = t_len // tm

    def body(ins, outs, scr, start, wait):
        x_ref, g5_ref, wupt_ref, wc_ref, wdown_ref, g6_ref = ins
        u_ref, c_ref, a_ref, y_ref, x3_ref = outs
        carry, shift = scr
        i = pl.program_id(0)
        finish = _bracket(start, wait, i == 0, i == nt - 1)

        @pl.when(i == 0)
        def _():
            carry[...] = jnp.zeros_like(carry)
            _fill_shifts(shift, tm, -1)

        xv = x_ref[...]
        h, _, _ = _rms(xv, g5_ref[...])
        hb = h.astype(BF16)
        for half in range(2):
            conv = []
            for part in range(2):
                c0, c1 = _ffn_cols(half, part)
                w0, w1, w2 = wc_ref[0:1, c0:c1], wc_ref[1:2, c0:c1], wc_ref[2:3, c0:c1]
                u = _nt(hb, wupt_ref[c0:c1, :])
                ub = u.astype(BF16)
                u_ref[:, c0:c1] = ub
                sh = _nn(shift[...], ub)
                cb = w0 * sh[tm:2 * tm, :] + w1 * sh[0:tm, :] + w2 * u
                c_ref[:, c0:c1] = cb.astype(BF16)
                m1, m2 = _edge_terms(carry[SUBLANES - 1:SUBLANES, c0:c1], carry[SUBLANES - 2:SUBLANES - 1, c0:c1], 0, 1)
                c_ref[0:16, c0:c1] = (cb[0:16, :] + w1 * m1 + w0 * m2).astype(BF16)
                carry[:, c0:c1] = u[tm - SUBLANES:tm, :].astype(BF16).astype(F32)
                conv.append(c_ref[:, c0:c1].astype(F32))
            a = (conv[0] * _sigmoid(conv[0]) * conv[1]).astype(BF16)
            a_ref[:, half * FF_CHUNK:(half + 1) * FF_CHUNK] = a
        yb = _nn(a_ref[...], wdown_ref[...]).astype(BF16)
        y_ref[...] = yb
        yn, _, _ = _rms(yb.astype(F32), g6_ref[...])
        x3_ref[...] = xv + yn
        finish()

    return _fused_call(
        "ffn_fwd", body, (nt,),
        [_rows(tm, D), _const((1, D)), _const((2 * D_FF, D)), _const((3, 2 * D_FF)), _const((D_FF, D)),
         _const((1, D))],
        [_rows(tm, 2 * D_FF), _rows(tm, 2 * D_FF), _rows(tm, D_FF), _rows(tm, D), _rows(tm, D)],
        [_sds((t_len, 2 * D_FF), BF16), _sds((t_len, 2 * D_FF), BF16), _sds((t_len, D_FF), BF16),
         _sds((t_len, D), BF16), _sds((t_len, D), F32)],
        [pltpu.VMEM((SUBLANES, 2 * D_FF), F32), pltpu.VMEM((2 * tm, tm), BF16)],
        (x2, g5, wupt, wc, wdown, g6), comm)


def _ffn_bwd(dx3, y, x2, u, c, g5, wupt, wc, wdown, g6, comm=None):
    t_len = x2.shape[0]
    tm = min(TM_BWD, t_len)
    nt = t_len // tm

    def body(ins, outs, scr, start, wait):
        dx3_ref, y_ref, x_ref, u_ref, c_ref, g5_ref, wupt_ref, wc_ref, wdown_ref, g6_ref = ins
        dx_ref, du_ref, dyo_ref, h_ref, dg5_ref, dg6_ref, dwc_ref = outs
        carry, shift = scr
        s = pl.program_id(0)
        finish = _bracket(start, wait, s == 0, s == nt - 1)

        @pl.when(s == 0)
        def _():
            carry[...] = jnp.zeros_like(carry)
            _fill_shifts(shift, tm, 1)
            dg5_ref[...] = jnp.zeros_like(dg5_ref)
            dg6_ref[...] = jnp.zeros_like(dg6_ref)
            dwc_ref[...] = jnp.zeros_like(dwc_ref)

        yv = y_ref[...].astype(F32)
        _, yh, yr = _rms(yv, g6_ref[...])
        dy, dg6 = _rms_bwd(yh, yr, g6_ref[...], dx3_ref[...])
        dg6_ref[...] += dg6
        dyb = dy.astype(BF16)
        dyo_ref[...] = dyb
        xv = x_ref[...]
        h, xh, xr = _rms(xv, g5_ref[...])
        h_ref[...] = h.astype(BF16)

        for half in range(2):
            g0, g1 = _ffn_cols(half, 0)
            v0, v1 = _ffn_cols(half, 1)
            gt = c_ref[:, g0:g1].astype(F32)
            vl = c_ref[:, v0:v1].astype(F32)
            sg = _sigmoid(gt)
            sil = gt * sg
            da = _nt(dyb, wdown_ref[half * FF_CHUNK:(half + 1) * FF_CHUNK, :])
            dcs = (da * vl * (sg * (1.0 + gt * (1.0 - sg))), da * sil)
            for part in range(2):
                c0, c1 = _ffn_cols(half, part)
                w0, w1, w2 = wc_ref[0:1, c0:c1], wc_ref[1:2, c0:c1], wc_ref[2:3, c0:c1]
                dc = dcs[part]
                sh = _nn(shift[...], dc.astype(BF16))
                d1, d2 = sh[0:tm, :], sh[tm:2 * tm, :]
                m1, m2 = _edge_terms(carry[0:1, c0:c1], carry[1:2, c0:c1], 15, 14)
                carry[:, c0:c1] = dc[0:SUBLANES, :]
                uu = u_ref[:, c0:c1].astype(F32)
                ut = u_ref[tm - 16:tm, c0:c1].astype(F32)
                dwc_ref[2:3, c0:c1] += _colsum(uu * dc)
                dwc_ref[1:2, c0:c1] += _colsum(uu * d1) + _colsum(ut * m1)
                dwc_ref[0:1, c0:c1] += _colsum(uu * d2) + _colsum(ut * m2)
                du = w2 * dc + w1 * d1 + w0 * d2
                du_ref[:, c0:c1] = du.astype(BF16)
                du_ref[tm - 16:tm, c0:c1] = (du[tm - 16:tm, :] + w1 * m1 + w0 * m2).astype(BF16)
        dh = _nn(du_ref[...], wupt_ref[...])
        dxn, dg5 = _rms_bwd(xh, xr, g5_ref[...], dh)
        dg5_ref[...] += dg5
        dx_ref[...] = dx3_ref[...] + dxn
        finish()

    return _fused_call(
        "ffn_bwd", body, (nt,),
        [_rows(tm, D, nt), _rows(tm, D, nt), _rows(tm, D, nt), _rows(tm, 2 * D_FF, nt), _rows(tm, 2 * D_FF, nt),
         _const((1, D)), _const((2 * D_FF, D)), _const((3, 2 * D_FF)), _const((D_FF, D)), _const((1, D))],
        [_rows(tm, D, nt), _rows(tm, 2 * D_FF, nt), _rows(tm, D, nt), _rows(tm, D, nt), _acc((1, D)), _acc((1, D)),
         _acc((8, 2 * D_FF))],
        [_sds((t_len, D), F32), _sds((t_len, 2 * D_FF), BF16), _sds((t_len, D), BF16), _sds((t_len, D), BF16),
         _sds((1, D), F32), _sds((1, D), F32), _sds((8, 2 * D_FF), F32)],
        [pltpu.VMEM((SUBLANES, 2 * D_FF), F32), pltpu.VMEM((2 * tm, tm), BF16)],
        (dx3, y, x2, u, c, g5, wupt, wc, wdown, g6), comm)


def _tn_matmul(a, b):
    t_len, m = a.shape
    bm = FF_CHUNK
    bt = min(2 * TM_FWD, t_len)
    nt = t_len // bt

    def body(a_ref, b_ref, o_ref, acc):
        t = pl.program_id(1)

        @pl.when(t == 0)
        def _():
            acc[...] = jnp.zeros_like(acc)

        acc[...] += _tn(a_ref[...], b_ref[...])

        @pl.when(t == nt - 1)
        def _():
            o_ref[...] = acc[...].astype(BF16)

    return pl.pallas_call(
        body, grid=(m // bm, nt), name="tn_matmul",
        in_specs=[pl.BlockSpec((bt, bm), lambda i, t: (t, i)), pl.BlockSpec((bt, D), lambda i, t: (t, 0))],
        out_specs=pl.BlockSpec((bm, D), lambda i, t: (i, 0)),
        out_shape=_sds((m, D), BF16),
        scratch_shapes=[pltpu.VMEM((bm, D), F32)],
        compiler_params=pltpu.CompilerParams(dimension_semantics=("parallel", "arbitrary"),
                                             vmem_limit_bytes=VMEM_LIMIT),
    )(a, b)


def _loss_grad(xf, target):
    t_len = xf.shape[0]
    tm = min(TM_FWD, t_len)
    nt = t_len // tm

    def body(ins, outs, scr, start, wait):
        x_ref, t_ref = ins
        dx_ref, loss_ref = outs

        @pl.when(pl.program_id(0) == 0)
        def _():
            loss_ref[...] = jnp.zeros_like(loss_ref)

        err = x_ref[...] - t_ref[...]
        dx_ref[...] = err * (1.0 / D)
        part = 0.5 * _colsum(jnp.mean(err * err, axis=-1, keepdims=True))
        loss_ref[...] += jnp.broadcast_to(part, loss_ref.shape)

    outs, _ = _fused_call(
        "loss_grad", body, (nt,), [_rows(tm, D), _rows(tm, D)], [_rows(tm, D), _acc((8, 128))],
        [_sds((t_len, D), F32), _sds((8, 128), F32)], [], (xf, target))
    return outs


BLOCK_BYTES = 1 << 20


def _row_block(rows, cols):
    limit = max(16, BLOCK_BYTES // (4 * cols))
    best = None
    for rb in range(16, min(rows, limit) + 1, 16):
        if rows % rb == 0:
            best = rb
    return best or rows


def _elementwise(name, fn, ins, out_dtypes):
    rows, cols = ins[0].shape
    rb = _row_block(rows, cols)
    n_in = len(ins)

    def body(*refs):
        res = fn(*[r[...] for r in refs[:n_in]])
        for o_ref, r in zip(refs[n_in:], res):
            o_ref[...] = r

    spec = pl.BlockSpec((rb, cols), lambda i: (i, 0))
    return pl.pallas_call(
        body, grid=(rows // rb,), name=name, in_specs=[spec] * n_in, out_specs=[spec] * len(out_dtypes),
        out_shape=[_sds((rows, cols), dt) for dt in out_dtypes],
        compiler_params=pltpu.CompilerParams(dimension_semantics=("parallel",), vmem_limit_bytes=VMEM_LIMIT),
    )(*ins)


def _cast_bf16(w):
    return _elementwise("cast_bf16", lambda v: (v.astype(BF16),), [w], [BF16])[0]


def _add2(a, b):
    return _elementwise("add2", lambda u, v: (u + v,), [a, b], [F32])[0]


def _adam_math(w, g, m, v):
    nm = ADAM_B1 * m + (1.0 - ADAM_B1) * g
    nv = ADAM_B2 * v + (1.0 - ADAM_B2) * (g * g)
    m_hat = nm / (1.0 - ADAM_B1 ** ADAM_STEP)
    v_hat = nv / (1.0 - ADAM_B2 ** ADAM_STEP)
    return -ADAM_LR * (m_hat / (jnp.sqrt(v_hat) + ADAM_EPS) + ADAM_WD * w), nm, nv


def _adamw(w, g, m, v):
    return _elementwise("adamw", _adam_math, [w, g, m, v], [F32, F32, F32])


def _sum_chips(parts):
    n, rows, cols = parts.shape
    rb = _row_block(rows, cols)

    def body(p_ref, o_ref):
        acc = p_ref[0].astype(F32)
        for j in range(1, n):
            acc = acc + p_ref[j].astype(F32)
        o_ref[...] = acc

    return pl.pallas_call(
        body, grid=(rows // rb,), name="sum_chips",
        in_specs=[pl.BlockSpec((n, rb, cols), lambda i: (0, i, 0))], out_specs=pl.BlockSpec((rb, cols), lambda i: (i, 0)),
        out_shape=_sds((rows, cols), F32),
        compiler_params=pltpu.CompilerParams(dimension_semantics=("parallel",), vmem_limit_bytes=VMEM_LIMIT),
    )(parts)


def _sum_devices(parts):
    n, rows, cols = parts.shape

    def body(p_ref, o_ref):
        acc = p_ref[0]
        for j in range(1, n):
            acc = acc + p_ref[j]
        o_ref[...] = acc

    return pl.pallas_call(
        body, name="sum_devices", out_shape=_sds((rows, cols), F32),
        compiler_params=pltpu.CompilerParams(vmem_limit_bytes=VMEM_LIMIT),
    )(parts)


def _pack(parts):
    flat = jnp.concatenate([p.reshape(-1) for p in parts])
    rows = -(-flat.shape[0] // 1024) * 8
    return jnp.pad(flat, (0, rows * 128 - flat.shape[0])).reshape(rows, 128)


def _unpack(packed, shapes):
    flat = packed.reshape(-1)
    out, off = [], 0
    for shp in shapes:
        size = 1
        for d in shp:
            size *= d
        out.append(flat[off:off + size].reshape(shp))
        off += size
    return out


_BIG = {'w_in': ('wint', True), 'w_out': ('wout', False), 'xattn_wq': ('wq', False), 'xattn_wk': ('wk', False),
        'xattn_wv': ('wv', False), 'xattn_wo': ('wo', False), 'ffn_w_up': ('wupt', True),
        'ffn_w_down': ('wdown', False)}
_KEYS = [key for key, _ in _BIG.values()]
_GATHER_EARLY = ("wint", "wout", "wq", "wk", "wv", "wo")
_GATHER_WITH = {"mix": ("wupt",), "xattn": ("wdown",), "ffn": _GATHER_EARLY}
_SCATTER_LATE = ("wupt", "wdown", "wq", "wk", "wv", "wo")
_SCATTER_NEXT = ("wint", "wout")


def _block_diag(maps):
    out = jnp.zeros((D_POOL, D_POOL), maps.dtype)
    for g in range(len(POOL_WINDOWS)):
        out = lax.dynamic_update_slice(out, maps[g], (g * POOL_GROUP, g * POOL_GROUP))
    return out


def _local_step(x, mem, target, small, big, shards=None):
    distributed = shards is not None
    depth = len(shards) if distributed else len(big)
    big = list(big)
    row = lambda a: a.reshape(1, -1)
    gmem = row(small["mem_norm"])
    saved = []
    for l in range(depth):
        w = big[l]
        nxt = {}

        def plan(stage):
            layer = l + 1 if stage == "ffn" else l
            if distributed and layer < depth:
                return _Gather([shards[layer][key] for key in _GATHER_WITH[stage]])
            return None

        def landed(stage, outs):
            into = nxt if stage == "ffn" else w
            for key, g in zip(_GATHER_WITH[stage], outs):
                into[key] = g.reshape(-1, D)

        sp = dict(
            g1=row(small["mix_pre_norm"][l]), g2=row(small["mix_post_norm"][l]),
            maps=_block_diag(small["pool_maps"][l]).astype(BF16), scale=row(small["pool_scale"][l]),
            dww=small["conf_dw_w"][l], dwb=row(small["conf_dw_b"][l]), lng=row(small["conf_ln_g"][l]),
            lnb=row(small["conf_ln_b"][l]), sw=small["sconv_w"][l],
            g3=row(small["xattn_pre_norm"][l]), g4=row(small["xattn_post_norm"][l]),
            g5=row(small["ffn_pre_norm"][l]), g6=row(small["ffn_post_norm"][l]), wc=small["ffn_conv_w"][l])
        (z, ycat, y1, x1, v1, cv, pooled), got = _mix_fwd(
            x, sp["g1"], w["wint"], sp["maps"], sp["scale"], sp["dww"], sp["dwb"], sp["lng"], sp["lnb"], sp["sw"],
            w["wout"], sp["g2"], plan("mix"))
        landed("mix", got)
        k, v = _kv_fwd(mem, gmem, w["wk"], w["wv"])
        (q, o, y2, x2), got = _xattn_fwd(x1, sp["g3"], w["wq"], k, v, w["wo"], sp["g4"], plan("xattn"))
        landed("xattn", got)
        (u, c, a, y3, x3), got = _ffn_fwd(x2, sp["g5"], w["wupt"], sp["wc"], w["wdown"], sp["g6"], plan("ffn"))
        landed("ffn", got)
        if nxt:
            big.append(nxt)
        saved.append(dict(sp=sp, x=x, z=z, ycat=ycat, y1=y1, x1=x1, v1=v1, cv=cv, pooled=pooled, k=k, v=v, q=q, o=o,
                          y2=y2, x2=x2, u=u, c=c, a=a, y3=y3))
        x = x3

    dx, loss_blk = _loss_grad(x, target)
    big_grads = [None] * depth
    sg = {n: [None] * depth for n in ("mix_pre_norm", "mix_post_norm", "pool_maps", "pool_scale", "conf_dw_w",
                                      "conf_dw_b", "conf_ln_g", "conf_ln_b", "sconv_w", "xattn_pre_norm",
                                      "xattn_post_norm", "ffn_pre_norm", "ffn_post_norm", "ffn_conv_w")}
    dgmem = None
    pending = None
    for l in reversed(range(depth)):
        w, s = big[l], saved[l]
        sp = s["sp"]
        comm = _Scatter(pending) if distributed and pending is not None else None
        (dx, du, dy3, h3, dg5, dg6, dwc), got = _ffn_bwd(dx, s["y3"], s["x2"], s["u"], s["c"], sp["g5"], w["wupt"],
                                                        sp["wc"], w["wdown"], sp["g6"], comm)
        if comm is not None:
            big_grads[l + 1] = (big_grads[l + 1], got[0])
        g = dict(wupt=_tn_matmul(du, h3), wdown=_tn_matmul(s["a"], dy3))
        dx, g["wq"], g["wo"], dk, dv, dg3, dg4 = _xattn_bwd(dx, s["y2"], s["x1"], s["q"], s["o"], sp["g3"], w["wq"],
                                                            s["k"], s["v"], w["wo"], sp["g4"])
        g["wk"], g["wv"], dgm = _kv_bwd(mem, gmem, dk, dv, w["wk"], w["wv"])
        dgmem = dgm if dgmem is None else dgmem + dgm
        comm = _Scatter([g[key] for key in _SCATTER_LATE]) if distributed else None
        (dx, g["wint"], g["wout"], dg1, dg2, dmaps, dscale, ddww, misc), got = _mix_bwd(
            dx, s["y1"], s["x"], s["z"], s["ycat"], s["v1"], s["cv"], s["pooled"], sp["g1"], w["wint"], sp["maps"],
            sp["scale"], sp["dww"], sp["lng"], sp["lnb"], sp["sw"], w["wout"], sp["g2"], comm)
        if distributed:
            big_grads[l] = got[0]
            pending = [g[key] for key in _SCATTER_NEXT]
        else:
            big_grads[l] = g
        sg["mix_pre_norm"][l] = dg1[0]
        sg["mix_post_norm"][l] = dg2[0]
        sg["pool_maps"][l] = jnp.stack([dmaps[i * 64:(i + 1) * 64, i * 64:(i + 1) * 64] for i in range(4)])
        sg["pool_scale"][l] = dscale[0]
        sg["conf_dw_w"][l] = ddww[0:CONF_K]
        sg["conf_dw_b"][l] = misc[0]
        sg["conf_ln_g"][l] = misc[1]
        sg["conf_ln_b"][l] = misc[2]
        sg["sconv_w"][l] = misc[3:6]
        sg["xattn_pre_norm"][l] = dg3[0]
        sg["xattn_post_norm"][l] = dg4[0]
        sg["ffn_pre_norm"][l] = dg5[0]
        sg["ffn_post_norm"][l] = dg6[0]
        sg["ffn_conv_w"][l] = dwc[0:3]
    if distributed:
        big_grads[0] = (big_grads[0], _comm_only(_Scatter(pending))[0])
    small_grads = {n: jnp.stack(vs) for n, vs in sg.items()}
    small_grads["mem_norm"] = dgmem[0]
    return loss_blk, dx, big_grads, small_grads


_WEIGHTS = ['mem_norm', 'mix_pre_norm', 'mix_post_norm', 'w_in', 'pool_maps', 'pool_scale', 'conf_dw_w', 'conf_dw_b',
            'conf_ln_g', 'conf_ln_b', 'sconv_w', 'w_out', 'xattn_pre_norm', 'xattn_post_norm', 'xattn_wq',
            'xattn_wk', 'xattn_wv', 'xattn_wo', 'ffn_pre_norm', 'ffn_post_norm', 'ffn_w_up', 'ffn_conv_w',
            'ffn_w_down']
_CHANNEL_SHARDED = ('conf_dw_w', 'sconv_w', 'ffn_conv_w')
_SMALL = [n for n in _WEIGHTS if n not in _BIG]


def kernel(x, mem, mem_norm, mix_pre_norm, mix_post_norm, w_in, pool_maps, pool_scale, conf_dw_w, conf_dw_b, conf_ln_g, conf_ln_b, sconv_w, w_out, xattn_pre_norm, xattn_post_norm, xattn_wq, xattn_wk, xattn_wv, xattn_wo, ffn_pre_norm, ffn_post_norm, ffn_w_up, ffn_conv_w, ffn_w_down, loss_target, m_mem_norm, m_mix_pre_norm, m_mix_post_norm, m_w_in, m_pool_maps, m_pool_scale, m_conf_dw_w, m_conf_dw_b, m_conf_ln_g, m_conf_ln_b, m_sconv_w, m_w_out, m_xattn_pre_norm, m_xattn_post_norm, m_xattn_wq, m_xattn_wk, m_xattn_wv, m_xattn_wo, m_ffn_pre_norm, m_ffn_post_norm, m_ffn_w_up, m_ffn_conv_w, m_ffn_w_down, v_mem_norm, v_mix_pre_norm, v_mix_post_norm, v_w_in, v_pool_maps, v_pool_scale, v_conf_dw_w, v_conf_dw_b, v_conf_ln_g, v_conf_ln_b, v_sconv_w, v_w_out, v_xattn_pre_norm, v_xattn_post_norm, v_xattn_wq, v_xattn_wk, v_xattn_wv, v_xattn_wo, v_ffn_pre_norm, v_ffn_post_norm, v_ffn_w_up, v_ffn_conv_w, v_ffn_w_down):
    wts = dict(mem_norm=mem_norm, mix_pre_norm=mix_pre_norm, mix_post_norm=mix_post_norm, w_in=w_in,
               pool_maps=pool_maps, pool_scale=pool_scale, conf_dw_w=conf_dw_w, conf_dw_b=conf_dw_b,
               conf_ln_g=conf_ln_g, conf_ln_b=conf_ln_b, sconv_w=sconv_w, w_out=w_out,
               xattn_pre_norm=xattn_pre_norm, xattn_post_norm=xattn_post_norm, xattn_wq=xattn_wq,
               xattn_wk=xattn_wk, xattn_wv=xattn_wv, xattn_wo=xattn_wo, ffn_pre_norm=ffn_pre_norm,
               ffn_post_norm=ffn_post_norm, ffn_w_up=ffn_w_up, ffn_conv_w=ffn_conv_w, ffn_w_down=ffn_w_down)
    mom_m = dict(mem_norm=m_mem_norm, mix_pre_norm=m_mix_pre_norm, mix_post_norm=m_mix_post_norm, w_in=m_w_in,
                 pool_maps=m_pool_maps, pool_scale=m_pool_scale, conf_dw_w=m_conf_dw_w, conf_dw_b=m_conf_dw_b,
                 conf_ln_g=m_conf_ln_g, conf_ln_b=m_conf_ln_b, sconv_w=m_sconv_w, w_out=m_w_out,
                 xattn_pre_norm=m_xattn_pre_norm, xattn_post_norm=m_xattn_post_norm, xattn_wq=m_xattn_wq,
                 xattn_wk=m_xattn_wk, xattn_wv=m_xattn_wv, xattn_wo=m_xattn_wo, ffn_pre_norm=m_ffn_pre_norm,
                 ffn_post_norm=m_ffn_post_norm, ffn_w_up=m_ffn_w_up, ffn_conv_w=m_ffn_conv_w,
                 ffn_w_down=m_ffn_w_down)
    mom_v = dict(mem_norm=v_mem_norm, mix_pre_norm=v_mix_pre_norm, mix_post_norm=v_mix_post_norm, w_in=v_w_in,
                 pool_maps=v_pool_maps, pool_scale=v_pool_scale, conf_dw_w=v_conf_dw_w, conf_dw_b=v_conf_dw_b,
                 conf_ln_g=v_conf_ln_g, conf_ln_b=v_conf_ln_b, sconv_w=v_sconv_w, w_out=v_w_out,
                 xattn_pre_norm=v_xattn_pre_norm, xattn_post_norm=v_xattn_post_norm, xattn_wq=v_xattn_wq,
                 xattn_wk=v_xattn_wk, xattn_wv=v_xattn_wv, xattn_wo=v_xattn_wo, ffn_pre_norm=v_ffn_pre_norm,
                 ffn_post_norm=v_ffn_post_norm, ffn_w_up=v_ffn_w_up, ffn_conv_w=v_ffn_conv_w,
                 ffn_w_down=v_ffn_w_down)
    depth = w_in.shape[0]
    chip = 2 * lax.axis_index("x") + lax.axis_index("y")

    stacked = {}
    for name, (key, transposed) in _BIG.items():
        w = wts[name]
        wb = _cast_bf16(w.reshape(-1, w.shape[-1])).reshape(w.shape)
        stacked[key] = wb.transpose(0, 2, 1) if transposed else wb
    shards = [{key: stacked[key][l] for key in _KEYS} for l in range(depth)]
    first = _comm_only(_Gather([shards[0][key] for key in _GATHER_EARLY]))
    big0 = {key: g.reshape(-1, D) for key, g in zip(_GATHER_EARLY, first)}

    conv_shapes = [wts[n].shape for n in _CHANNEL_SHARDED]
    conv_all = _allgather_devices(_pack([wts[n] for n in _CHANNEL_SHARDED]))
    per_chip = [_unpack(conv_all[2 * j], conv_shapes) for j in range(N_CHIPS)]
    small = {n: wts[n] for n in _SMALL}
    for i, n in enumerate(_CHANNEL_SHARDED):
        small[n] = jnp.concatenate([per_chip[j][i] for j in range(N_CHIPS)], axis=-1)

    loss_blk, dx, landings, small_grads = _local_step(x[0], mem[0], loss_target[0], small, [big0], shards)

    core_sums = [_sum_chips(buf) for pair in landings for buf in pair]
    sibling_sums = _sibling_swap(core_sums)
    totals = [_add2(mine, theirs) for mine, theirs in zip(core_sums, sibling_sums)]
    layout = {}
    for which, order in ((0, _SCATTER_LATE), (1, _SCATTER_NEXT)):
        off = 0
        for key in order:
            r = stacked[key].shape[1]
            layout[key] = (which, off, r)
            off += r
    grads = {}
    for name, (key, transposed) in _BIG.items():
        which, off, r = layout[key]
        g = jnp.stack([totals[2 * l + which][off:off + r] for l in range(depth)])
        grads[name] = g.transpose(0, 2, 1) if transposed else g

    small_shapes = [(128,)] + [small[n].shape for n in _SMALL]
    partial = _pack([loss_blk[0]] + [small_grads[n] for n in _SMALL])
    total = _unpack(_sum_devices(_allgather_devices(partial)), small_shapes)
    loss = total[0][0]
    for n, g in zip(_SMALL, total[1:]):
        if n in _CHANNEL_SHARDED:
            width = wts[n].shape[-1]
            g = lax.dynamic_slice_in_dim(g, chip * width, width, axis=-1)
        grads[n] = g

    delta, new_m, new_v = {}, {}, {}
    for name in _BIG:
        shp = wts[name].shape
        flat = lambda a: a.reshape(-1, shp[-1])
        d, nm, nv = _adamw(flat(wts[name]), flat(grads[name]), flat(mom_m[name]), flat(mom_v[name]))
        delta[name], new_m[name], new_v[name] = d.reshape(shp), nm.reshape(shp), nv.reshape(shp)
    shapes = [wts[n].shape for n in _SMALL]
    d, nm, nv = _adamw(_pack([wts[n] for n in _SMALL]), _pack([grads[n] for n in _SMALL]),
                       _pack([mom_m[n] for n in _SMALL]), _pack([mom_v[n] for n in _SMALL]))
    for out, packed in ((delta, d), (new_m, nm), (new_v, nv)):
        for n, a in zip(_SMALL, _unpack(packed, shapes)):
            out[n] = a

    return (loss, dx[None], *[grads[n] for n in _WEIGHTS], *[delta[n] for n in _WEIGHTS],
            *[new_m[n] for n in _WEIGHTS], *[new_v[n] for n in _WEIGHTS])
```

```python
import jax
import jax.numpy as jnp
from jax import lax
from jax.experimental import pallas as pl
from jax.experimental.pallas import tpu as pltpu

F32 = jnp.float32
BF16 = jnp.bfloat16

EPS = 1e-6
D = 1024
D_POOL, D_CONF, D_SCONV = 256, 384, 384
D_IN = D_POOL + 2 * D_CONF + 3 * D_SCONV
D_FF = 2816
FF_CHUNK = 1408
HEADS, HEAD_DIM = 4, 256
CONF_K = 31
POOL_WINDOWS = (2, 4, 8, 16)
POOL_GROUP = 64
SUBLANES = 8
HALO = 32
PHASE_ROWS = HALO - SUBLANES
TAP_ROWS = 64
FHALO = 16
TM_FWD = 512
TM_BWD = 256
N_CHIPS = 4
N_DEV = 8
MESH_ID = pl.DeviceIdType.MESH
VMEM_LIMIT = 56 << 20

ADAM_LR, ADAM_B1, ADAM_B2, ADAM_EPS, ADAM_WD, ADAM_STEP = 0.001, 0.9, 0.999, 1e-08, 0.01, 10

C_P = (0, 256)
C_A = (256, 640)
C_G = (640, 1024)
C_B = (1024, 1408)
C_C = (1408, 1792)
C_X = (1792, 2176)


def _nn(a, b):
    return jnp.dot(a, b, preferred_element_type=F32)


def _nt(a, b):
    return lax.dot_general(a, b, (((1,), (1,)), ((), ())), preferred_element_type=F32)


def _tn(a, b):
    return lax.dot_general(a, b, (((0,), (0,)), ((), ())), preferred_element_type=F32)


def _sigmoid(v):
    return 1.0 / (1.0 + jnp.exp(-v))


def _rms(v, g):
    r = lax.rsqrt(jnp.mean(v * v, axis=-1, keepdims=True) + EPS)
    vh = v * r
    return vh * g, vh, r


def _rms_bwd(vh, r, g, dy):
    dvh = dy * g
    dv = r * (dvh - vh * jnp.mean(dvh * vh, axis=-1, keepdims=True))
    return dv, jnp.sum(dy * vh, axis=0, keepdims=True)


def _colsum(v):
    return jnp.sum(v, axis=0, keepdims=True)


def _rows(tm, n, nt=None):
    if nt is None:
        return pl.BlockSpec((tm, n), lambda i: (i, 0))
    return pl.BlockSpec((tm, n), lambda i: (nt - 1 - i, 0))


def _const(shape):
    nd = len(shape)
    return pl.BlockSpec(shape, lambda i: (0,) * nd, pipeline_mode=pl.Buffered(1))


def _acc(shape):
    nd = len(shape)
    return pl.BlockSpec(shape, lambda i: (0,) * nd)


def _sds(shape, dtype):
    return jax.ShapeDtypeStruct(shape, dtype)


_HBM = pl.BlockSpec(memory_space=pltpu.HBM)


def _mesh_pos():
    return lax.axis_index("x"), lax.axis_index("y"), lax.axis_index("c")


def _chip_peers():
    x, y, c = _mesh_pos()
    flips = [(1 - x, y), (x, 1 - y), (1 - x, 1 - y)]
    return 2 * x + y, [((px, py, c), 2 * px + py) for px, py in flips]


def _remote(src, dst, send_sems, recv_sems, idx, dev):
    return pltpu.make_async_remote_copy(src_ref=src, dst_ref=dst, send_sem=send_sems.at[idx],
                                        recv_sem=recv_sems.at[idx], device_id=dev, device_id_type=MESH_ID)


class _Plan:
    def __init__(self, arrays):
        self.arrays = list(arrays)

    def scratch(self):
        n = len(self.arrays) * self.n_peers
        return [pltpu.SemaphoreType.DMA((n,)), pltpu.SemaphoreType.DMA((n,)),
                pltpu.SemaphoreType.DMA((len(self.arrays),))]

    def _copies(self, ins, outs, sems):
        send_sems, recv_sems, local_sems = sems
        me, peers = self.peers()
        own, sends, recvs = [], [], []
        for k in range(len(ins)):
            own.append(pltpu.make_async_copy(self.src(ins, k, me), self.dst(outs, k, me), local_sems.at[k]))
            for j, (dev, who) in enumerate(peers):
                idx = self.n_peers * k + j
                sends.append(_remote(self.src(ins, k, who), self.dst(outs, k, me), send_sems, recv_sems, idx, dev))
                recvs.append(_remote(self.src(ins, k, me), self.dst(outs, k, who), send_sems, recv_sems, idx, dev))
        return own, sends, recvs

    def start(self, ins, outs, sems):
        own, sends, _ = self._copies(ins, outs, sems)
        for cp in own + sends:
            cp.start()

    def wait(self, ins, outs, sems):
        own, sends, recvs = self._copies(ins, outs, sems)
        for cp in recvs:
            cp.wait_recv()
        for cp in sends:
            cp.wait_send()
        for cp in own:
            cp.wait()


class _Gather(_Plan):
    tag = "gather"
    n_peers = N_CHIPS - 1

    def peers(self):
        return _chip_peers()

    def out_shapes(self):
        return [_sds((N_CHIPS,) + a.shape, a.dtype) for a in self.arrays]

    def src(self, ins, k, chip):
        return ins[k]

    def dst(self, outs, k, chip):
        return outs[k].at[chip]


class _Scatter(_Plan):
    tag = "scatter"
    n_peers = N_DEV - 1

    def __init__(self, arrays):
        super().__init__(arrays)
        self.rows = [a.shape[0] // N_DEV for a in self.arrays]
        self.offs = [sum(self.rows[:k]) for k in range(len(self.rows))]

    def peers(self):
        x, y, c = _mesh_pos()
        flip = lambda v, on: 1 - v if on else v
        others = [(flip(x, m & 4), flip(y, m & 2), flip(c, m & 1)) for m in range(1, N_DEV)]
        return 4 * x + 2 * y + c, [(dev, 4 * dev[0] + 2 * dev[1] + dev[2]) for dev in others]

    def out_shapes(self):
        a = self.arrays[0]
        return [_sds((N_DEV, sum(self.rows), a.shape[1]), a.dtype)]

    def src(self, ins, k, dev):
        r = self.rows[k]
        return ins[k].at[pl.ds(pl.multiple_of(dev * r, 16), r), :]

    def dst(self, outs, k, dev):
        return outs[0].at[dev, pl.ds(self.offs[k], self.rows[k]), :]


def _fused_call(name, body, grid, in_specs, out_specs, out_shape, scratch, args, comm=None, sem=("arbitrary",)):
    n_in, n_out, n_scr = len(in_specs), len(out_specs), len(scratch)
    c_in = comm.arrays if comm else []
    c_out = comm.out_shapes() if comm else []
    c_scr = comm.scratch() if comm else []

    def kernel_fn(*refs):
        ins, cins = refs[:n_in], refs[n_in:n_in + len(c_in)]
        o0 = n_in + len(c_in)
        outs, couts = refs[o0:o0 + n_out], refs[o0 + n_out:o0 + n_out + len(c_out)]
        s0 = o0 + n_out + len(c_out)
        scr, csems = refs[s0:s0 + n_scr], refs[s0 + n_scr:]
        if comm:
            body(ins, outs, scr, lambda: comm.start(cins, couts, csems), lambda: comm.wait(cins, couts, csems))
        else:
            body(ins, outs, scr, None, None)

    res = pl.pallas_call(
        kernel_fn, grid=grid, name=name + ("_" + comm.tag if comm else ""),
        in_specs=list(in_specs) + [_HBM] * len(c_in), out_specs=list(out_specs) + [_HBM] * len(c_out),
        out_shape=list(out_shape) + c_out, scratch_shapes=list(scratch) + c_scr,
        compiler_params=pltpu.CompilerParams(dimension_semantics=sem, vmem_limit_bytes=VMEM_LIMIT),
    )(*args, *c_in)
    return res[:n_out], res[n_out:]


def _bracket(start, wait, first, last):
    if start is not None:
        pl.when(first)(start)

    def finish():
        if wait is not None:
            pl.when(last)(wait)
    return finish


def _comm_only(plan):
    n_in, n_out = len(plan.arrays), len(plan.out_shapes())

    def body(*refs):
        ins, outs, sems = refs[:n_in], refs[n_in:n_in + n_out], refs[n_in + n_out:]
        plan.start(ins, outs, sems)
        plan.wait(ins, outs, sems)

    return pl.pallas_call(
        body, name=plan.tag + "_chips", in_specs=[_HBM] * n_in, out_specs=[_HBM] * n_out,
        out_shape=plan.out_shapes(), scratch_shapes=plan.scratch(),
    )(*plan.arrays)


def _pair_halves(halves):
    n = len(halves)

    def body(*refs):
        ins, outs = refs[:n], refs[n:2 * n]
        send_sems, recv_sems, local_sems = refs[2 * n:]
        x, y, c = _mesh_pos()
        sibling = (x, y, 1 - c)
        own = [pltpu.make_async_copy(ins[k], outs[k].at[c], local_sems.at[k]) for k in range(n)]
        sends = [_remote(ins[k], outs[k].at[c], send_sems, recv_sems, k, sibling) for k in range(n)]
        for cp in own + sends:
            cp.start()
        for k in range(n):
            _remote(ins[k], outs[k].at[1 - c], send_sems, recv_sems, k, sibling).wait_recv()
        for cp in sends:
            cp.wait_send()
        for cp in own:
            cp.wait()

    return pl.pallas_call(
        body, name="pair_halves", in_specs=[_HBM] * n, out_specs=[_HBM] * n,
        out_shape=[_sds((2,) + a.shape, a.dtype) for a in halves],
        scratch_shapes=[pltpu.SemaphoreType.DMA((n,)), pltpu.SemaphoreType.DMA((n,)), pltpu.SemaphoreType.DMA((n,))],
    )(*halves)


def _allgather_devices(v):
    m_per, n = v.shape

    def body(v_ref, out_ref, send_sems, recv_sems, local_sem):
        x, y, c = _mesh_pos()
        me, sibling = (x, y, c), (x, y, 1 - c)
        chips = [(1 - x, y), (x, 1 - y), (1 - x, 1 - y)]

        def rows(px, py, pc):
            return out_ref.at[4 * px + 2 * py + pc]

        def copy(k, block, to, src=None):
            return _remote(rows(*block) if src is None else src, rows(*block), send_sems, recv_sems, k, to)

        mine = pltpu.make_async_copy(v_ref, rows(*me), local_sem)
        mine.start()
        first = [copy(0, me, sibling, src=v_ref)]
        first += [copy(1 + j, me, (*chip, c), src=v_ref) for j, chip in enumerate(chips)]
        for cp in first:
            cp.start()
        passed = [copy(4 + j, (*chip, c), sibling) for j, chip in enumerate(chips)]
        for j, chip in enumerate(chips):
            copy(1 + j, (*chip, c), me).wait_recv()
            passed[j].start()
        copy(0, sibling, me).wait_recv()
        for j, chip in enumerate(chips):
            copy(4 + j, (*chip, 1 - c), me).wait_recv()
        for cp in first + passed:
            cp.wait_send()
        mine.wait()

    return pl.pallas_call(
        body, name="allgather_devices", out_shape=_sds((N_DEV, m_per, n), v.dtype),
        in_specs=[pl.BlockSpec(memory_space=pltpu.VMEM)], out_specs=pl.BlockSpec(memory_space=pltpu.VMEM),
        scratch_shapes=[pltpu.SemaphoreType.DMA((7,)), pltpu.SemaphoreType.DMA((7,)), pltpu.SemaphoreType.DMA],
        compiler_params=pltpu.CompilerParams(vmem_limit_bytes=VMEM_LIMIT),
    )(v)


def _pool_lane():
    return lax.broadcasted_iota(jnp.int32, (1, D_POOL), 1)


def _pool_count(t0, tm):
    lane = _pool_lane()
    w = jnp.where(lane < 64, 2, jnp.where(lane < 128, 4, jnp.where(lane < 192, 8, 16)))
    pos1 = lax.broadcasted_iota(jnp.int32, (tm, D_POOL), 0) + (t0 + 1)
    return jnp.minimum(pos1, w).astype(F32)


def _fill_bands(band_ref, tm, causal):
    r = lax.broadcasted_iota(jnp.int32, (tm, tm + HALO), 0)
    s = lax.broadcasted_iota(jnp.int32, (tm, tm + HALO), 1)
    d = (r + HALO - s) if causal else (s - r)
    for g, w in enumerate(POOL_WINDOWS):
        band_ref[g] = jnp.where((d >= 0) & (d < w), 1.0, 0.0).astype(BF16)


def _window_sums(band_ref, operand, width):
    lane = _pool_lane()
    res = None
    for g in range(len(POOL_WINDOWS)):
        r = _nn(band_ref[g], operand)
        acc = r[:, 0:width]
        for c0 in range(width, r.shape[1], width):
            acc = acc + r[:, c0:c0 + width]
        res = acc if res is None else jnp.where(lane >= POOL_GROUP * g, acc, res)
    return res


def _phase_copies(src, phases, tm):
    for b in range(1, SUBLANES):
        phases[b - 1] = src[b:b + tm + PHASE_ROWS, :]


def _tap(src, phases, off, rows, r0=0):
    a, b = divmod(off, SUBLANES)
    lo = SUBLANES * a + r0
    if b == 0:
        return src[lo:lo + rows, :]
    return phases[b - 1, lo:lo + rows, :]


def _layer_norm_stats(v1):
    mu = jnp.mean(v1, axis=-1, keepdims=True)
    xc = v1 - mu
    rs = lax.rsqrt(jnp.mean(xc * xc, axis=-1, keepdims=True) + EPS)
    return xc * rs, rs


def _mix_fwd(x, g1, wint, maps_bd, scale, dww, dwb, lng, lnb, sw, wout, g2, comm=None):
    t_len = x.shape[0]
    tm = min(TM_FWD, t_len)
    nt = t_len // tm
    h0 = HALO

    def body(ins, outs, scr, start, wait):
        (x_ref, g1_ref, wint_ref, maps_ref, scale_ref, dww_ref, dwb_ref, lng_ref, lnb_ref, sw_ref, wout_ref,
         g2_ref) = ins
        z_ref, ycat_ref, y_ref, x1_ref, v1_ref, cv_ref, pooled_ref = outs
        pbuf, vbuf, sbuf, phases, band = scr
        i = pl.program_id(0)
        finish = _bracket(start, wait, i == 0, i == nt - 1)

        @pl.when(i == 0)
        def _():
            pbuf[0:h0, :] = jnp.zeros((h0, D_POOL), F32)
            vbuf[0:h0, :] = jnp.zeros((h0, D_CONF), F32)
            sbuf[0:h0, :] = jnp.zeros((h0, D_SCONV), F32)
            _fill_bands(band, tm, True)

        @pl.when(i > 0)
        def _():
            pbuf[0:h0, :] = pbuf[tm:tm + h0, :]
            vbuf[0:h0, :] = vbuf[tm:tm + h0, :]
            sbuf[0:h0, :] = sbuf[tm:tm + h0, :]

        xv = x_ref[...]
        h, _, _ = _rms(xv, g1_ref[...])
        z = _nt(h.astype(BF16), wint_ref[...])
        z_ref[...] = z.astype(BF16)
        zp = z[:, C_P[0]:C_P[1]]
        pbuf[h0:h0 + tm, :] = zp
        vbuf[h0:h0 + tm, :] = z[:, C_A[0]:C_A[1]] * _sigmoid(z[:, C_G[0]:C_G[1]])
        sbuf[h0:h0 + tm, :] = z[:, C_C[0]:C_C[1]] * z[:, C_X[0]:C_X[1]]

        pv = pbuf[...]
        hi = pv.astype(BF16)
        lo = (pv - hi.astype(F32)).astype(BF16)
        sums = _window_sums(band, jnp.concatenate([hi, lo], axis=1), D_POOL)
        pooled = (sums / _pool_count(i * tm, tm) - zp).astype(BF16)
        pooled_ref[...] = pooled
        ycat_ref[:, 0:D_POOL] = (_nn(pooled, maps_ref[...]) * scale_ref[...]).astype(BF16)

        _phase_copies(vbuf, phases, tm)
        base = h0 - (CONF_K - 1)
        for r0 in range(0, tm, TAP_ROWS):
            v1 = dww_ref[0:1, :] * _tap(vbuf, phases, base, TAP_ROWS, r0)
            for j in range(1, CONF_K):
                v1 = v1 + dww_ref[j:j + 1, :] * _tap(vbuf, phases, base + j, TAP_ROWS, r0)
            v1 = v1 + dwb_ref[...]
            v1_ref[r0:r0 + TAP_ROWS, :] = v1
            vh, _ = _layer_norm_stats(v1)
            v2 = vh * lng_ref[...] + lnb_ref[...]
            ycat_ref[r0:r0 + TAP_ROWS, D_POOL:D_POOL + D_CONF] = (v2 * _sigmoid(v2)).astype(BF16)

        cv = (sw_ref[0:1, :] * sbuf[h0 - 2:h0 - 2 + tm, :] + sw_ref[1:2, :] * sbuf[h0 - 1:h0 - 1 + tm, :]
              + sw_ref[2:3, :] * sbuf[h0:h0 + tm, :])
        cv_ref[...] = cv
        ycat_ref[:, D_POOL + D_CONF:D] = (z[:, C_B[0]:C_B[1]] * cv).astype(BF16)

        yb = _nn(ycat_ref[...], wout_ref[...]).astype(BF16)
        y_ref[...] = yb
        yn, _, _ = _rms(yb.astype(F32), g2_ref[...])
        x1_ref[...] = xv + yn
        finish()

    return _fused_call(
        "mix_fwd", body, (nt,),
        [_rows(tm, D), _const((1, D)), _const((D_IN, D)), _const((D_POOL, D_POOL)), _const((1, D_POOL)),
         _const((CONF_K, D_CONF)), _const((1, D_CONF)), _const((1, D_CONF)), _const((1, D_CONF)),
         _const((3, D_SCONV)), _const((D, D)), _const((1, D))],
        [_rows(tm, D_IN), _rows(tm, D), _rows(tm, D), _rows(tm, D), _rows(tm, D_CONF), _rows(tm, D_SCONV),
         _rows(tm, D_POOL)],
        [_sds((t_len, D_IN), BF16), _sds((t_len, D), BF16), _sds((t_len, D), BF16), _sds((t_len, D), F32),
         _sds((t_len, D_CONF), F32), _sds((t_len, D_SCONV), F32), _sds((t_len, D_POOL), BF16)],
        [pltpu.VMEM((h0 + tm, D_POOL), F32), pltpu.VMEM((h0 + tm, D_CONF), F32),
         pltpu.VMEM((h0 + tm, D_SCONV), F32), pltpu.VMEM((SUBLANES - 1, tm + PHASE_ROWS, D_CONF), F32),
         pltpu.VMEM((len(POOL_WINDOWS), tm, tm + h0), BF16)],
        (x, g1, wint, maps_bd, scale, dww, dwb, lng, lnb, sw, wout, g2), comm)


def _mix_bwd(dx1, y, x, z, ycat, v1, cv, pooled, g1, wint, maps_bd, scale, dww, lng, lnb, sw, wout, g2, comm=None):
    t_len = x.shape[0]
    tm = min(TM_BWD, t_len)
    nt = t_len // tm
    h0 = HALO

    def body(ins, outs, scr, start, wait):
        (dx1_ref, y_ref, x_ref, z_ref, ycat_ref, v1_ref, cv_ref, pooled_ref, g1_ref, wint_ref, maps_ref, scale_ref,
         dww_ref, lng_ref, lnb_ref, sw_ref, wout_ref, g2_ref) = ins
        dx_ref, dwin_ref, dwout_ref, dg1_ref, dg2_ref, dmaps_ref, dscale_ref, ddww_ref, misc_ref = outs
        ebuf, dvbuf, dcbuf, phases, band, dzbuf, acc_in, acc_out, ddacc = scr
        s = pl.program_id(0)
        ti = nt - 1 - s
        finish = _bracket(start, wait, s == 0, s == nt - 1)

        @pl.when(s == 0)
        def _():
            ebuf[tm:tm + h0, :] = jnp.zeros((h0, D_POOL), F32)
            dvbuf[tm:tm + h0, :] = jnp.zeros((h0, D_CONF), F32)
            dcbuf[tm:tm + h0, :] = jnp.zeros((h0, D_SCONV), F32)
            _fill_bands(band, tm, False)
            acc_in[...] = jnp.zeros_like(acc_in)
            acc_out[...] = jnp.zeros_like(acc_out)
            ddacc[...] = jnp.zeros_like(ddacc)
            dg1_ref[...] = jnp.zeros_like(dg1_ref)
            dg2_ref[...] = jnp.zeros_like(dg2_ref)
            dmaps_ref[...] = jnp.zeros_like(dmaps_ref)
            dscale_ref[...] = jnp.zeros_like(dscale_ref)
            ddww_ref[...] = jnp.zeros_like(ddww_ref)
            misc_ref[...] = jnp.zeros_like(misc_ref)

        @pl.when(s > 0)
        def _():
            ebuf[tm:tm + h0, :] = ebuf[0:h0, :]
            dvbuf[tm:tm + h0, :] = dvbuf[0:h0, :]
            dcbuf[tm:tm + h0, :] = dcbuf[0:h0, :]

        yv = y_ref[...].astype(F32)
        _, yh, yr = _rms(yv, g2_ref[...])
        dy, dg2 = _rms_bwd(yh, yr, g2_ref[...], dx1_ref[...])
        dg2_ref[...] += dg2
        dyb = dy.astype(BF16)
        acc_out[...] += _tn(ycat_ref[...], dyb)
        dycat = _nt(dyb, wout_ref[...])
        dya = dycat[:, 0:D_POOL]
        dyb2 = dycat[:, D_POOL:D_POOL + D_CONF]
        dyc = dycat[:, D_POOL + D_CONF:D]

        pooled_v = pooled_ref[...]
        pm = _nn(pooled_v, maps_ref[...])
        dscale_ref[...] += _colsum(dya * pm)
        dq = (dya * scale_ref[...]).astype(BF16)
        dmaps_ref[...] += _tn(pooled_v, dq)
        dpooled = _nt(dq, maps_ref[...])
        ebuf[0:tm, :] = dpooled / _pool_count(ti * tm, tm)
        dzbuf[:, C_P[0]:C_P[1]] = (_window_sums(band, ebuf[...].astype(BF16), D_POOL) - dpooled).astype(BF16)

        vh, rs = _layer_norm_stats(v1_ref[...])
        v2 = vh * lng_ref[...] + lnb_ref[...]
        s2 = _sigmoid(v2)
        dv2 = dyb2 * (s2 * (1.0 + v2 * (1.0 - s2)))
        misc_ref[1:2, :] += _colsum(dv2 * vh)
        misc_ref[2:3, :] += _colsum(dv2)
        dvh = dv2 * lng_ref[...]
        dv1 = rs * (dvh - jnp.mean(dvh, axis=-1, keepdims=True) - vh * jnp.mean(dvh * vh, axis=-1, keepdims=True))
        misc_ref[0:1, :] += _colsum(dv1)
        dvbuf[0:tm, :] = dv1
        _phase_copies(dvbuf, phases, tm)
        for r0 in range(0, tm, TAP_ROWS):
            blk = slice(r0, r0 + TAP_ROWS)
            za = z_ref[blk, C_A[0]:C_A[1]].astype(F32)
            sg = _sigmoid(z_ref[blk, C_G[0]:C_G[1]].astype(F32))
            v0 = za * sg
            dv0 = None
            for k in range(CONF_K):
                j = CONF_K - 1 - k
                dk = _tap(dvbuf, phases, k, TAP_ROWS, r0)
                prod = v0 * dk
                part = prod[0:SUBLANES, :]
                for q in range(SUBLANES, TAP_ROWS, SUBLANES):
                    part = part + prod[q:q + SUBLANES, :]
                ddacc[SUBLANES * j:SUBLANES * (j + 1), :] += part
                term = dww_ref[j:j + 1, :] * dk
                dv0 = term if dv0 is None else dv0 + term
            dzbuf[blk, C_A[0]:C_A[1]] = (dv0 * sg).astype(BF16)
            dzbuf[blk, C_G[0]:C_G[1]] = (dv0 * za * sg * (1.0 - sg)).astype(BF16)

        zb = z_ref[:, C_B[0]:C_B[1]].astype(F32)
        zc = z_ref[:, C_C[0]:C_C[1]].astype(F32)
        zx = z_ref[:, C_X[0]:C_X[1]].astype(F32)
        pv = zc * zx
        dzbuf[:, C_B[0]:C_B[1]] = (dyc * cv_ref[...]).astype(BF16)
        dcbuf[0:tm, :] = dyc * zb
        dp = None
        for k in range(3):
            j = 2 - k
            dk = dcbuf[k:k + tm, :]
            misc_ref[3 + j:4 + j, :] += _colsum(pv * dk)
            term = sw_ref[j:j + 1, :] * dk
            dp = term if dp is None else dp + term
        dzbuf[:, C_C[0]:C_C[1]] = (dp * zx).astype(BF16)
        dzbuf[:, C_X[0]:C_X[1]] = (dp * zc).astype(BF16)

        xv = x_ref[...]
        h, xh, xr = _rms(xv, g1_ref[...])
        dz = dzbuf[...]
        acc_in[...] += _tn(dz, h.astype(BF16))
        dh = _nn(dz, wint_ref[...])
        dxn, dg1 = _rms_bwd(xh, xr, g1_ref[...], dh)
        dg1_ref[...] += dg1
        dx_ref[...] = dx1_ref[...] + dxn

        @pl.when(s == nt - 1)
        def _():
            dwin_ref[...] = acc_in[...].astype(BF16)
            dwout_ref[...] = acc_out[...].astype(BF16)
            for j in range(CONF_K):
                ddww_ref[j:j + 1, :] = _colsum(ddacc[SUBLANES * j:SUBLANES * (j + 1), :])

        finish()

    return _fused_call(
        "mix_bwd", body, (nt,),
        [_rows(tm, D, nt), _rows(tm, D, nt), _rows(tm, D, nt), _rows(tm, D_IN, nt), _rows(tm, D, nt),
         _rows(tm, D_CONF, nt), _rows(tm, D_SCONV, nt), _rows(tm, D_POOL, nt), _const((1, D)), _const((D_IN, D)),
         _const((D_POOL, D_POOL)), _const((1, D_POOL)), _const((CONF_K, D_CONF)), _const((1, D_CONF)),
         _const((1, D_CONF)), _const((3, D_SCONV)), _const((D, D)), _const((1, D))],
        [_rows(tm, D, nt), _acc((D_IN, D)), _acc((D, D)), _acc((1, D)), _acc((1, D)), _acc((D_POOL, D_POOL)),
         _acc((1, D_POOL)), _acc((32, D_CONF)), _acc((8, D_CONF))],
        [_sds((t_len, D), F32), _sds((D_IN, D), BF16), _sds((D, D), BF16), _sds((1, D), F32), _sds((1, D), F32),
         _sds((D_POOL, D_POOL), F32), _sds((1, D_POOL), F32), _sds((32, D_CONF), F32), _sds((8, D_CONF), F32)],
        [pltpu.VMEM((tm + h0, D_POOL), F32), pltpu.VMEM((tm + h0, D_CONF), F32),
         pltpu.VMEM((tm + h0, D_SCONV), F32), pltpu.VMEM((SUBLANES - 1, tm + PHASE_ROWS, D_CONF), F32),
         pltpu.VMEM((len(POOL_WINDOWS), tm, tm + h0), BF16), pltpu.VMEM((tm, D_IN), BF16),
         pltpu.VMEM((D_IN, D), F32), pltpu.VMEM((D, D), F32), pltpu.VMEM((SUBLANES * CONF_K, D_CONF), F32)],
        (dx1, y, x, z, ycat, v1, cv, pooled, g1, wint, maps_bd, scale, dww, lng, lnb, sw, wout, g2), comm)


def _kv_fwd(mem, gmem, wk, wv):
    def body(mem_ref, g_ref, wk_ref, wv_ref, k_ref, v_ref):
        mn, _, _ = _rms(mem_ref[...], g_ref[...])
        mnb = mn.astype(BF16)
        k_ref[...] = _nn(mnb, wk_ref[...]).astype(BF16)
        v_ref[...] = _nn(mnb, wv_ref[...]).astype(BF16)

    n = mem.shape[0]
    return pl.pallas_call(
        body, name="kv_fwd", out_shape=[_sds((n, D), BF16), _sds((n, D), BF16)],
        compiler_params=pltpu.CompilerParams(vmem_limit_bytes=VMEM_LIMIT),
    )(mem, gmem, wk, wv)


def _kv_bwd(mem, gmem, dk, dv, wk, wv):
    def body(mem_ref, g_ref, dk_ref, dv_ref, wk_ref, wv_ref, dwk_ref, dwv_ref, dg_ref):
        mn, mh, _ = _rms(mem_ref[...], g_ref[...])
        mnb = mn.astype(BF16)
        dkb = dk_ref[...].astype(BF16)
        dvb = dv_ref[...].astype(BF16)
        dwk_ref[...] = _tn(mnb, dkb).astype(BF16)
        dwv_ref[...] = _tn(mnb, dvb).astype(BF16)
        dmn = _nt(dkb, wk_ref[...]) + _nt(dvb, wv_ref[...])
        dg_ref[...] = _colsum(dmn * mh)

    return pl.pallas_call(
        body, name="kv_bwd", out_shape=[_sds((D, D), BF16), _sds((D, D), BF16), _sds((1, D), F32)],
        compiler_params=pltpu.CompilerParams(vmem_limit_bytes=VMEM_LIMIT),
    )(mem, gmem, dk, dv, wk, wv)


def _softmax_rows(s):
    e = jnp.exp(s - jnp.max(s, axis=-1, keepdims=True))
    return e / jnp.sum(e, axis=-1, keepdims=True)


def _xattn_fwd(x1, g3, wq, k, v, wo, g4, comm=None):
    t_len = x1.shape[0]
    tm = min(2 * TM_FWD, t_len)
    nt = t_len // tm
    n_mem = k.shape[0]
    sc = HEAD_DIM ** -0.5

    def body(ins, outs, scr, start, wait):
        x_ref, g3_ref, wq_ref, k_ref, v_ref, wo_ref, g4_ref = ins
        q_ref, p_ref, o_ref, y_ref, x2_ref = outs
        i = pl.program_id(0)
        finish = _bracket(start, wait, i == 0, i == nt - 1)
        xv = x_ref[...]
        h, _, _ = _rms(xv, g3_ref[...])
        qb = _nn(h.astype(BF16), wq_ref[...]).astype(BF16)
        q_ref[...] = qb
        for hd in range(HEADS):
            sl = slice(hd * HEAD_DIM, (hd + 1) * HEAD_DIM)
            pb = _softmax_rows(_nt(qb[:, sl], k_ref[:, sl]) * sc).astype(BF16)
            p_ref[:, hd * n_mem:(hd + 1) * n_mem] = pb
            o_ref[:, sl] = _nn(pb, v_ref[:, sl]).astype(BF16)
        yb = _nn(o_ref[...], wo_ref[...]).astype(BF16)
        y_ref[...] = yb
        yn, _, _ = _rms(yb.astype(F32), g4_ref[...])
        x2_ref[...] = xv + yn
        finish()

    return _fused_call(
        "xattn_fwd", body, (nt,),
        [_rows(tm, D), _const((1, D)), _const((D, D)), _const((n_mem, D)), _const((n_mem, D)), _const((D, D)),
         _const((1, D))],
        [_rows(tm, D), _rows(tm, HEADS * n_mem), _rows(tm, D), _rows(tm, D), _rows(tm, D)],
        [_sds((t_len, D), BF16), _sds((t_len, HEADS * n_mem), BF16), _sds((t_len, D), BF16), _sds((t_len, D), BF16),
         _sds((t_len, D), F32)],
        [], (x1, g3, wq, k, v, wo, g4), comm)


def _xattn_bwd(dx2, y, x1, q, p, o, g3, wq, k, v, wo, g4):
    t_len = x1.shape[0]
    tm = min(TM_FWD, t_len)
    nt = t_len // tm
    n_mem = k.shape[0]
    sc = HEAD_DIM ** -0.5

    def body(ins, outs, scr, start, wait):
        dx2_ref, y_ref, x_ref, q_ref, p_ref, o_ref, g3_ref, wq_ref, k_ref, v_ref, wo_ref, g4_ref = ins
        dx_ref, dwq_ref, dwo_ref, dk_ref, dv_ref, dg3_ref, dg4_ref = outs
        dqbuf, acc_q, acc_o = scr
        s = pl.program_id(0)

        @pl.when(s == 0)
        def _():
            acc_q[...] = jnp.zeros_like(acc_q)
            acc_o[...] = jnp.zeros_like(acc_o)
            dk_ref[...] = jnp.zeros_like(dk_ref)
            dv_ref[...] = jnp.zeros_like(dv_ref)
            dg3_ref[...] = jnp.zeros_like(dg3_ref)
            dg4_ref[...] = jnp.zeros_like(dg4_ref)

        yv = y_ref[...].astype(F32)
        _, yh, yr = _rms(yv, g4_ref[...])
        dy, dg4 = _rms_bwd(yh, yr, g4_ref[...], dx2_ref[...])
        dg4_ref[...] += dg4
        dyb = dy.astype(BF16)
        acc_o[...] += _tn(o_ref[...], dyb)
        do = _nt(dyb, wo_ref[...])
        qb = q_ref[...]
        for hd in range(HEADS):
            sl = slice(hd * HEAD_DIM, (hd + 1) * HEAD_DIM)
            pb = p_ref[:, hd * n_mem:(hd + 1) * n_mem]
            p = pb.astype(F32)
            dob = do[:, sl].astype(BF16)
            dp = _nt(dob, v_ref[:, sl])
            dv_ref[:, sl] += _tn(pb, dob)
            ds = (p * (dp - jnp.sum(dp * p, axis=-1, keepdims=True)) * sc).astype(BF16)
            dqbuf[:, sl] = _nn(ds, k_ref[:, sl]).astype(BF16)
            dk_ref[:, sl] += _tn(ds, qb[:, sl])
        xv = x_ref[...]
        h, xh, xr = _rms(xv, g3_ref[...])
        dq = dqbuf[...]
        acc_q[...] += _tn(h.astype(BF16), dq)
        dh = _nt(dq, wq_ref[...])
        dxn, dg3 = _rms_bwd(xh, xr, g3_ref[...], dh)
        dg3_ref[...] += dg3
        dx_ref[...] = dx2_ref[...] + dxn

        @pl.when(s == nt - 1)
        def _():
            dwq_ref[...] = acc_q[...].astype(BF16)
            dwo_ref[...] = acc_o[...].astype(BF16)

    outs, _ = _fused_call(
        "xattn_bwd", body, (nt,),
        [_rows(tm, D), _rows(tm, D), _rows(tm, D), _rows(tm, D), _rows(tm, HEADS * n_mem), _rows(tm, D), _const((1, D)),
         _const((D, D)), _const((n_mem, D)), _const((n_mem, D)), _const((D, D)), _const((1, D))],
        [_rows(tm, D), _acc((D, D)), _acc((D, D)), _acc((n_mem, D)), _acc((n_mem, D)), _acc((1, D)), _acc((1, D))],
        [_sds((t_len, D), F32), _sds((D, D), BF16), _sds((D, D), BF16), _sds((n_mem, D), F32),
         _sds((n_mem, D), F32), _sds((1, D), F32), _sds((1, D), F32)],
        [pltpu.VMEM((tm, D), BF16), pltpu.VMEM((D, D), F32), pltpu.VMEM((D, D), F32)],
        (dx2, y, x1, q, p, o, g3, wq, k, v, wo, g4))
    return outs


def _ffn_cols(half, part):
    c0 = part * D_FF + half * FF_CHUNK
    return c0, c0 + FF_CHUNK


def _fill_shifts(ref, tm, step):
    t = lax.broadcasted_iota(jnp.int32, (tm, tm), 0)
    s = lax.broadcasted_iota(jnp.int32, (tm, tm), 1)
    ref[0:tm, :] = jnp.where(s == t + step, 1.0, 0.0).astype(BF16)
    ref[tm:2 * tm, :] = jnp.where(s == t + 2 * step, 1.0, 0.0).astype(BF16)


def _edge_terms(near, far, edge, inner):
    r = lax.broadcasted_iota(jnp.int32, (16, near.shape[1]), 0)
    return jnp.where(r == edge, near, 0.0), jnp.where(r == edge, far, jnp.where(r == inner, near, 0.0))


def _ffn_fwd(x2, g5, wupt, wc, wdown, g6, comm=None):
    t_len = x2.shape[0]
    tm = min(TM_BWD, t_len)
    nt = t_len // tm

    def body(ins, outs, scr, start, wait):
        x_ref, g5_ref, wupt_ref, wc_ref, wdown_ref, g6_ref = ins
        u_ref, c_ref, a_ref, y_ref, x3_ref = outs
        carry, shift = scr
        i = pl.program_id(0)
        finish = _bracket(start, wait, i == 0, i == nt - 1)

        @pl.when(i == 0)
        def _():
            carry[...] = jnp.zeros_like(carry)
            _fill_shifts(shift, tm, -1)

        xv = x_ref[...]
        h, _, _ = _rms(xv, g5_ref[...])
        hb = h.astype(BF16)
        for half in range(2):
            conv = []
            for part in range(2):
                c0, c1 = _ffn_cols(half, part)
                w0, w1, w2 = wc_ref[0:1, c0:c1], wc_ref[1:2, c0:c1], wc_ref[2:3, c0:c1]
                u = _nt(hb, wupt_ref[c0:c1, :])
                ub = u.astype(BF16)
                u_ref[:, c0:c1] = ub
                sh = _nn(shift[...], ub)
                cb = w0 * sh[tm:2 * tm, :] + w1 * sh[0:tm, :] + w2 * u
                c_ref[:, c0:c1] = cb.astype(BF16)
                m1, m2 = _edge_terms(carry[SUBLANES - 1:SUBLANES, c0:c1], carry[SUBLANES - 2:SUBLANES - 1, c0:c1], 0, 1)
                c_ref[0:16, c0:c1] = (cb[0:16, :] + w1 * m1 + w0 * m2).astype(BF16)
                carry[:, c0:c1] = u[tm - SUBLANES:tm, :].astype(BF16).astype(F32)
                conv.append(c_ref[:, c0:c1].astype(F32))
            a = (conv[0] * _sigmoid(conv[0]) * conv[1]).astype(BF16)
            a_ref[:, half * FF_CHUNK:(half + 1) * FF_CHUNK] = a
        yb = _nn(a_ref[...], wdown_ref[...]).astype(BF16)
        y_ref[...] = yb
        yn, _, _ = _rms(yb.astype(F32), g6_ref[...])
        x3_ref[...] = xv + yn
        finish()

    return _fused_call(
        "ffn_fwd", body, (nt,),
        [_rows(tm, D), _const((1, D)), _const((2 * D_FF, D)), _const((3, 2 * D_FF)), _const((D_FF, D)),
         _const((1, D))],
        [_rows(tm, 2 * D_FF), _rows(tm, 2 * D_FF), _rows(tm, D_FF), _rows(tm, D), _rows(tm, D)],
        [_sds((t_len, 2 * D_FF), BF16), _sds((t_len, 2 * D_FF), BF16), _sds((t_len, D_FF), BF16),
         _sds((t_len, D), BF16), _sds((t_len, D), F32)],
        [pltpu.VMEM((SUBLANES, 2 * D_FF), F32), pltpu.VMEM((2 * tm, tm), BF16)],
        (x2, g5, wupt, wc, wdown, g6), comm)


def _ffn_bwd(dx3, y, x2, u, c, g5, wupt, wc, wdown, g6, comm=None):
    t_len = x2.shape[0]
    tm = min(TM_BWD, t_len)
    nt = t_len // tm

    def body(ins, outs, scr, start, wait):
        dx3_ref, y_ref, x_ref, u_ref, c_ref, g5_ref, wupt_ref, wc_ref, wdown_ref, g6_ref = ins
        dx_ref, du_ref, dyo_ref, h_ref, dg5_ref, dg6_ref, dwc_ref = outs
        carry, shift = scr
        s = pl.program_id(0)
        finish = _bracket(start, wait, s == 0, s == nt - 1)

        @pl.when(s == 0)
        def _():
            carry[...] = jnp.zeros_like(carry)
            _fill_shifts(shift, tm, 1)
            dg5_ref[...] = jnp.zeros_like(dg5_ref)
            dg6_ref[...] = jnp.zeros_like(dg6_ref)
            dwc_ref[...] = jnp.zeros_like(dwc_ref)

        yv = y_ref[...].astype(F32)
        _, yh, yr = _rms(yv, g6_ref[...])
        dy, dg6 = _rms_bwd(yh, yr, g6_ref[...], dx3_ref[...])
        dg6_ref[...] += dg6
        dyb = dy.astype(BF16)
        dyo_ref[...] = dyb
        xv = x_ref[...]
        h, xh, xr = _rms(xv, g5_ref[...])
        h_ref[...] = h.astype(BF16)

        for half in range(2):
            g0, g1 = _ffn_cols(half, 0)
            v0, v1 = _ffn_cols(half, 1)
            gt = c_ref[:, g0:g1].astype(F32)
            vl = c_ref[:, v0:v1].astype(F32)
            sg = _sigmoid(gt)
            sil = gt * sg
            da = _nt(dyb, wdown_ref[half * FF_CHUNK:(half + 1) * FF_CHUNK, :])
            dcs = (da * vl * (sg * (1.0 + gt * (1.0 - sg))), da * sil)
            for part in range(2):
                c0, c1 = _ffn_cols(half, part)
                w0, w1, w2 = wc_ref[0:1, c0:c1], wc_ref[1:2, c0:c1], wc_ref[2:3, c0:c1]
                dc = dcs[part]
                sh = _nn(shift[...], dc.astype(BF16))
                d1, d2 = sh[0:tm, :], sh[tm:2 * tm, :]
                m1, m2 = _edge_terms(carry[0:1, c0:c1], carry[1:2, c0:c1], 15, 14)
                carry[:, c0:c1] = dc[0:SUBLANES, :]
                uu = u_ref[:, c0:c1].astype(F32)
                ut = u_ref[tm - 16:tm, c0:c1].astype(F32)
                dwc_ref[2:3, c0:c1] += _colsum(uu * dc)
                dwc_ref[1:2, c0:c1] += _colsum(uu * d1) + _colsum(ut * m1)
                dwc_ref[0:1, c0:c1] += _colsum(uu * d2) + _colsum(ut * m2)
                du = w2 * dc + w1 * d1 + w0 * d2
                du_ref[:, c0:c1] = du.astype(BF16)
                du_ref[tm - 16:tm, c0:c1] = (du[tm - 16:tm, :] + w1 * m1 + w0 * m2).astype(BF16)
        dh = _nn(du_ref[...], wupt_ref[...])
        dxn, dg5 = _rms_bwd(xh, xr, g5_ref[...], dh)
        dg5_ref[...] += dg5
        dx_ref[...] = dx3_ref[...] + dxn
        finish()

    return _fused_call(
        "ffn_bwd", body, (nt,),
        [_rows(tm, D, nt), _rows(tm, D, nt), _rows(tm, D, nt), _rows(tm, 2 * D_FF, nt), _rows(tm, 2 * D_FF, nt),
         _const((1, D)), _const((2 * D_FF, D)), _const((3, 2 * D_FF)), _const((D_FF, D)), _const((1, D))],
        [_rows(tm, D, nt), _rows(tm, 2 * D_FF, nt), _rows(tm, D, nt), _rows(tm, D, nt), _acc((1, D)), _acc((1, D)),
         _acc((8, 2 * D_FF))],
        [_sds((t_len, D), F32), _sds((t_len, 2 * D_FF), BF16), _sds((t_len, D), BF16), _sds((t_len, D), BF16),
         _sds((1, D), F32), _sds((1, D), F32), _sds((8, 2 * D_FF), F32)],
        [pltpu.VMEM((SUBLANES, 2 * D_FF), F32), pltpu.VMEM((2 * tm, tm), BF16)],
        (dx3, y, x2, u, c, g5, wupt, wc, wdown, g6), comm)


def _tn_matmul(a, b):
    t_len, m = a.shape
    bm = FF_CHUNK
    bt = min(2 * TM_FWD, t_len)
    nt = t_len // bt

    def body(a_ref, b_ref, o_ref, acc):
        t = pl.program_id(1)

        @pl.when(t == 0)
        def _():
            acc[...] = jnp.zeros_like(acc)

        acc[...] += _tn(a_ref[...], b_ref[...])

        @pl.when(t == nt - 1)
        def _():
            o_ref[...] = acc[...].astype(BF16)

    return pl.pallas_call(
        body, grid=(m // bm, nt), name="tn_matmul",
        in_specs=[pl.BlockSpec((bt, bm), lambda i, t: (t, i)), pl.BlockSpec((bt, D), lambda i, t: (t, 0))],
        out_specs=pl.BlockSpec((bm, D), lambda i, t: (i, 0)),
        out_shape=_sds((m, D), BF16),
        scratch_shapes=[pltpu.VMEM((bm, D), F32)],
        compiler_params=pltpu.CompilerParams(dimension_semantics=("parallel", "arbitrary"),
                                             vmem_limit_bytes=VMEM_LIMIT),
    )(a, b)


def _loss_grad(xf, target):
    t_len = xf.shape[0]
    tm = min(TM_FWD, t_len)
    nt = t_len // tm

    def body(ins, outs, scr, start, wait):
        x_ref, t_ref = ins
        dx_ref, loss_ref = outs

        @pl.when(pl.program_id(0) == 0)
        def _():
            loss_ref[...] = jnp.zeros_like(loss_ref)

        err = x_ref[...] - t_ref[...]
        dx_ref[...] = err * (1.0 / D)
        part = 0.5 * _colsum(jnp.mean(err * err, axis=-1, keepdims=True))
        loss_ref[...] += jnp.broadcast_to(part, loss_ref.shape)

    outs, _ = _fused_call(
        "loss_grad", body, (nt,), [_rows(tm, D), _rows(tm, D)], [_rows(tm, D), _acc((8, 128))],
        [_sds((t_len, D), F32), _sds((8, 128), F32)], [], (xf, target))
    return outs


BLOCK_BYTES = 1 << 20


def _row_block(rows, cols):
    limit = max(16, BLOCK_BYTES // (4 * cols))
    best = None
    for rb in range(16, min(rows, limit) + 1, 16):
        if rows % rb == 0:
            best = rb
    return best or rows


def _elementwise(name, fn, ins, out_dtypes):
    rows, cols = ins[0].shape
    rb = _row_block(rows, cols)
    n_in = len(ins)

    def body(*refs):
        res = fn(*[r[...] for r in refs[:n_in]])
        for o_ref, r in zip(refs[n_in:], res):
            o_ref[...] = r

    spec = pl.BlockSpec((rb, cols), lambda i: (i, 0))
    return pl.pallas_call(
        body, grid=(rows // rb,), name=name, in_specs=[spec] * n_in, out_specs=[spec] * len(out_dtypes),
        out_shape=[_sds((rows, cols), dt) for dt in out_dtypes],
        compiler_params=pltpu.CompilerParams(dimension_semantics=("parallel",), vmem_limit_bytes=VMEM_LIMIT),
    )(*ins)


def _cast_bf16(w):
    return _elementwise("cast_bf16", lambda v: (v.astype(BF16),), [w], [BF16])[0]


def _adam_math(w, g, m, v):
    nm = ADAM_B1 * m + (1.0 - ADAM_B1) * g
    nv = ADAM_B2 * v + (1.0 - ADAM_B2) * (g * g)
    m_hat = nm / (1.0 - ADAM_B1 ** ADAM_STEP)
    v_hat = nv / (1.0 - ADAM_B2 ** ADAM_STEP)
    return -ADAM_LR * (m_hat / (jnp.sqrt(v_hat) + ADAM_EPS) + ADAM_WD * w), nm, nv


def _adamw(w, g, m, v):
    return _elementwise("adamw", _adam_math, [w, g, m, v], [F32, F32, F32])


def _sum_blocks(parts):
    n, rows, cols = parts.shape
    rb = _row_block(rows, cols)

    def body(p_ref, o_ref):
        acc = p_ref[0].astype(F32)
        for j in range(1, n):
            acc = acc + p_ref[j].astype(F32)
        o_ref[...] = acc

    return pl.pallas_call(
        body, grid=(rows // rb,), name="sum_blocks",
        in_specs=[pl.BlockSpec((n, rb, cols), lambda i: (0, i, 0))], out_specs=pl.BlockSpec((rb, cols), lambda i: (i, 0)),
        out_shape=_sds((rows, cols), F32),
        compiler_params=pltpu.CompilerParams(dimension_semantics=("parallel",), vmem_limit_bytes=VMEM_LIMIT),
    )(parts)


def _sum_devices(parts):
    n, rows, cols = parts.shape

    def body(p_ref, o_ref):
        acc = p_ref[0]
        for j in range(1, n):
            acc = acc + p_ref[j]
        o_ref[...] = acc

    return pl.pallas_call(
        body, name="sum_devices", out_shape=_sds((rows, cols), F32),
        compiler_params=pltpu.CompilerParams(vmem_limit_bytes=VMEM_LIMIT),
    )(parts)


def _pack(parts):
    flat = jnp.concatenate([p.reshape(-1) for p in parts])
    rows = -(-flat.shape[0] // 1024) * 8
    return jnp.pad(flat, (0, rows * 128 - flat.shape[0])).reshape(rows, 128)


def _unpack(packed, shapes):
    flat = packed.reshape(-1)
    out, off = [], 0
    for shp in shapes:
        size = 1
        for d in shp:
            size *= d
        out.append(flat[off:off + size].reshape(shp))
        off += size
    return out


_BIG = {'w_in': ('wint', True), 'w_out': ('wout', False), 'xattn_wq': ('wq', False), 'xattn_wk': ('wk', False),
        'xattn_wv': ('wv', False), 'xattn_wo': ('wo', False), 'ffn_w_up': ('wupt', True),
        'ffn_w_down': ('wdown', False)}
_KEYS = [key for key, _ in _BIG.values()]
_GATHER_EARLY = ("wint", "wout", "wq", "wk", "wv", "wo")
_GATHER_WITH = {"mix": ("wupt",), "xattn": ("wdown",), "ffn": _GATHER_EARLY}
_SCATTER_LATE = ("wupt", "wdown", "wq", "wk", "wv", "wo")
_SCATTER_NEXT = ("wint", "wout")


def _block_diag(maps):
    out = jnp.zeros((D_POOL, D_POOL), maps.dtype)
    for g in range(len(POOL_WINDOWS)):
        out = lax.dynamic_update_slice(out, maps[g], (g * POOL_GROUP, g * POOL_GROUP))
    return out


def _local_step(x, mem, target, small, big, shards=None):
    distributed = shards is not None
    depth = len(shards) if distributed else len(big)
    big = list(big)
    row = lambda a: a.reshape(1, -1)
    gmem = row(small["mem_norm"])
    saved = []
    for l in range(depth):
        w = big[l]
        nxt = {}

        def plan(stage):
            layer = l + 1 if stage == "ffn" else l
            if distributed and layer < depth:
                return _Gather([shards[layer][key] for key in _GATHER_WITH[stage]])
            return None

        def landed(stage, outs):
            into = nxt if stage == "ffn" else w
            for key, g in zip(_GATHER_WITH[stage], outs):
                into[key] = g.reshape(-1, D)

        sp = dict(
            g1=row(small["mix_pre_norm"][l]), g2=row(small["mix_post_norm"][l]),
            maps=_block_diag(small["pool_maps"][l]).astype(BF16), scale=row(small["pool_scale"][l]),
            dww=small["conf_dw_w"][l], dwb=row(small["conf_dw_b"][l]), lng=row(small["conf_ln_g"][l]),
            lnb=row(small["conf_ln_b"][l]), sw=small["sconv_w"][l],
            g3=row(small["xattn_pre_norm"][l]), g4=row(small["xattn_post_norm"][l]),
            g5=row(small["ffn_pre_norm"][l]), g6=row(small["ffn_post_norm"][l]), wc=small["ffn_conv_w"][l])
        (z, ycat, y1, x1, v1, cv, pooled), got = _mix_fwd(
            x, sp["g1"], w["wint"], sp["maps"], sp["scale"], sp["dww"], sp["dwb"], sp["lng"], sp["lnb"], sp["sw"],
            w["wout"], sp["g2"], plan("mix"))
        landed("mix", got)
        k, v = _kv_fwd(mem, gmem, w["wk"], w["wv"])
        (q, p, o, y2, x2), got = _xattn_fwd(x1, sp["g3"], w["wq"], k, v, w["wo"], sp["g4"], plan("xattn"))
        landed("xattn", got)
        (u, c, a, y3, x3), got = _ffn_fwd(x2, sp["g5"], w["wupt"], sp["wc"], w["wdown"], sp["g6"], plan("ffn"))
        landed("ffn", got)
        if nxt:
            big.append(nxt)
        saved.append(dict(sp=sp, x=x, z=z, ycat=ycat, y1=y1, x1=x1, v1=v1, cv=cv, pooled=pooled, k=k, v=v, q=q, p=p, o=o,
                          y2=y2, x2=x2, u=u, c=c, a=a, y3=y3))
        x = x3

    dx, loss_blk = _loss_grad(x, target)
    big_grads = [None] * depth
    sg = {n: [None] * depth for n in ("mix_pre_norm", "mix_post_norm", "pool_maps", "pool_scale", "conf_dw_w",
                                      "conf_dw_b", "conf_ln_g", "conf_ln_b", "sconv_w", "xattn_pre_norm",
                                      "xattn_post_norm", "ffn_pre_norm", "ffn_post_norm", "ffn_conv_w")}
    dgmem = None
    pending = None
    for l in reversed(range(depth)):
        w, s = big[l], saved[l]
        sp = s["sp"]
        comm = _Scatter(pending) if distributed and pending is not None else None
        (dx, du, dy3, h3, dg5, dg6, dwc), got = _ffn_bwd(dx, s["y3"], s["x2"], s["u"], s["c"], sp["g5"], w["wupt"],
                                                        sp["wc"], w["wdown"], sp["g6"], comm)
        if comm is not None:
            big_grads[l + 1] = (big_grads[l + 1], got[0])
        g = dict(wupt=_tn_matmul(du, h3), wdown=_tn_matmul(s["a"], dy3))
        dx, g["wq"], g["wo"], dk, dv, dg3, dg4 = _xattn_bwd(dx, s["y2"], s["x1"], s["q"], s["p"], s["o"], sp["g3"], w["wq"],
                                                            s["k"], s["v"], w["wo"], sp["g4"])
        g["wk"], g["wv"], dgm = _kv_bwd(mem, gmem, dk, dv, w["wk"], w["wv"])
        dgmem = dgm if dgmem is None else dgmem + dgm
        comm = _Scatter([g[key] for key in _SCATTER_LATE]) if distributed else None
        (dx, g["wint"], g["wout"], dg1, dg2, dmaps, dscale, ddww, misc), got = _mix_bwd(
            dx, s["y1"], s["x"], s["z"], s["ycat"], s["v1"], s["cv"], s["pooled"], sp["g1"], w["wint"], sp["maps"],
            sp["scale"], sp["dww"], sp["lng"], sp["lnb"], sp["sw"], w["wout"], sp["g2"], comm)
        if distributed:
            big_grads[l] = got[0]
            pending = [g[key] for key in _SCATTER_NEXT]
        else:
            big_grads[l] = g
        sg["mix_pre_norm"][l] = dg1[0]
        sg["mix_post_norm"][l] = dg2[0]
        sg["pool_maps"][l] = jnp.stack([dmaps[i * 64:(i + 1) * 64, i * 64:(i + 1) * 64] for i in range(4)])
        sg["pool_scale"][l] = dscale[0]
        sg["conf_dw_w"][l] = ddww[0:CONF_K]
        sg["conf_dw_b"][l] = misc[0]
        sg["conf_ln_g"][l] = misc[1]
        sg["conf_ln_b"][l] = misc[2]
        sg["sconv_w"][l] = misc[3:6]
        sg["xattn_pre_norm"][l] = dg3[0]
        sg["xattn_post_norm"][l] = dg4[0]
        sg["ffn_pre_norm"][l] = dg5[0]
        sg["ffn_post_norm"][l] = dg6[0]
        sg["ffn_conv_w"][l] = dwc[0:3]
    if distributed:
        big_grads[0] = (big_grads[0], _comm_only(_Scatter(pending))[0])
    small_grads = {n: jnp.stack(vs) for n, vs in sg.items()}
    small_grads["mem_norm"] = dgmem[0]
    return loss_blk, dx, big_grads, small_grads


_WEIGHTS = ['mem_norm', 'mix_pre_norm', 'mix_post_norm', 'w_in', 'pool_maps', 'pool_scale', 'conf_dw_w', 'conf_dw_b',
            'conf_ln_g', 'conf_ln_b', 'sconv_w', 'w_out', 'xattn_pre_norm', 'xattn_post_norm', 'xattn_wq',
            'xattn_wk', 'xattn_wv', 'xattn_wo', 'ffn_pre_norm', 'ffn_post_norm', 'ffn_w_up', 'ffn_conv_w',
            'ffn_w_down']
_CHANNEL_SHARDED = ('conf_dw_w', 'sconv_w', 'ffn_conv_w')
_SMALL = [n for n in _WEIGHTS if n not in _BIG]


def kernel(x, mem, mem_norm, mix_pre_norm, mix_post_norm, w_in, pool_maps, pool_scale, conf_dw_w, conf_dw_b, conf_ln_g, conf_ln_b, sconv_w, w_out, xattn_pre_norm, xattn_post_norm, xattn_wq, xattn_wk, xattn_wv, xattn_wo, ffn_pre_norm, ffn_post_norm, ffn_w_up, ffn_conv_w, ffn_w_down, loss_target, m_mem_norm, m_mix_pre_norm, m_mix_post_norm, m_w_in, m_pool_maps, m_pool_scale, m_conf_dw_w, m_conf_dw_b, m_conf_ln_g, m_conf_ln_b, m_sconv_w, m_w_out, m_xattn_pre_norm, m_xattn_post_norm, m_xattn_wq, m_xattn_wk, m_xattn_wv, m_xattn_wo, m_ffn_pre_norm, m_ffn_post_norm, m_ffn_w_up, m_ffn_conv_w, m_ffn_w_down, v_mem_norm, v_mix_pre_norm, v_mix_post_norm, v_w_in, v_pool_maps, v_pool_scale, v_conf_dw_w, v_conf_dw_b, v_conf_ln_g, v_conf_ln_b, v_sconv_w, v_w_out, v_xattn_pre_norm, v_xattn_post_norm, v_xattn_wq, v_xattn_wk, v_xattn_wv, v_xattn_wo, v_ffn_pre_norm, v_ffn_post_norm, v_ffn_w_up, v_ffn_conv_w, v_ffn_w_down):
    wts = dict(mem_norm=mem_norm, mix_pre_norm=mix_pre_norm, mix_post_norm=mix_post_norm, w_in=w_in,
               pool_maps=pool_maps, pool_scale=pool_scale, conf_dw_w=conf_dw_w, conf_dw_b=conf_dw_b,
               conf_ln_g=conf_ln_g, conf_ln_b=conf_ln_b, sconv_w=sconv_w, w_out=w_out,
               xattn_pre_norm=xattn_pre_norm, xattn_post_norm=xattn_post_norm, xattn_wq=xattn_wq,
               xattn_wk=xattn_wk, xattn_wv=xattn_wv, xattn_wo=xattn_wo, ffn_pre_norm=ffn_pre_norm,
               ffn_post_norm=ffn_post_norm, ffn_w_up=ffn_w_up, ffn_conv_w=ffn_conv_w, ffn_w_down=ffn_w_down)
    mom_m = dict(mem_norm=m_mem_norm, mix_pre_norm=m_mix_pre_norm, mix_post_norm=m_mix_post_norm, w_in=m_w_in,
                 pool_maps=m_pool_maps, pool_scale=m_pool_scale, conf_dw_w=m_conf_dw_w, conf_dw_b=m_conf_dw_b,
                 conf_ln_g=m_conf_ln_g, conf_ln_b=m_conf_ln_b, sconv_w=m_sconv_w, w_out=m_w_out,
                 xattn_pre_norm=m_xattn_pre_norm, xattn_post_norm=m_xattn_post_norm, xattn_wq=m_xattn_wq,
                 xattn_wk=m_xattn_wk, xattn_wv=m_xattn_wv, xattn_wo=m_xattn_wo, ffn_pre_norm=m_ffn_pre_norm,
                 ffn_post_norm=m_ffn_post_norm, ffn_w_up=m_ffn_w_up, ffn_conv_w=m_ffn_conv_w,
                 ffn_w_down=m_ffn_w_down)
    mom_v = dict(mem_norm=v_mem_norm, mix_pre_norm=v_mix_pre_norm, mix_post_norm=v_mix_post_norm, w_in=v_w_in,
                 pool_maps=v_pool_maps, pool_scale=v_pool_scale, conf_dw_w=v_conf_dw_w, conf_dw_b=v_conf_dw_b,
                 conf_ln_g=v_conf_ln_g, conf_ln_b=v_conf_ln_b, sconv_w=v_sconv_w, w_out=v_w_out,
                 xattn_pre_norm=v_xattn_pre_norm, xattn_post_norm=v_xattn_post_norm, xattn_wq=v_xattn_wq,
                 xattn_wk=v_xattn_wk, xattn_wv=v_xattn_wv, xattn_wo=v_xattn_wo, ffn_pre_norm=v_ffn_pre_norm,
                 ffn_post_norm=v_ffn_post_norm, ffn_w_up=v_ffn_w_up, ffn_conv_w=v_ffn_conv_w,
                 ffn_w_down=v_ffn_w_down)
    depth = w_in.shape[0]
    chip = 2 * lax.axis_index("x") + lax.axis_index("y")

    stacked = {}
    for name, (key, transposed) in _BIG.items():
        w = wts[name]
        wb = _cast_bf16(w.reshape(-1, w.shape[-1])).reshape(w.shape)
        stacked[key] = wb.transpose(0, 2, 1) if transposed else wb
    shards = [{key: stacked[key][l] for key in _KEYS} for l in range(depth)]
    first = _comm_only(_Gather([shards[0][key] for key in _GATHER_EARLY]))
    big0 = {key: g.reshape(-1, D) for key, g in zip(_GATHER_EARLY, first)}

    conv_shapes = [wts[n].shape for n in _CHANNEL_SHARDED]
    conv_all = _allgather_devices(_pack([wts[n] for n in _CHANNEL_SHARDED]))
    per_chip = [_unpack(conv_all[2 * j], conv_shapes) for j in range(N_CHIPS)]
    small = {n: wts[n] for n in _SMALL}
    for i, n in enumerate(_CHANNEL_SHARDED):
        small[n] = jnp.concatenate([per_chip[j][i] for j in range(N_CHIPS)], axis=-1)

    loss_blk, dx, landings, small_grads = _local_step(x[0], mem[0], loss_target[0], small, [big0], shards)

    paired = _pair_halves([_sum_blocks(buf) for pair in landings for buf in pair])
    layout = {}
    for which, order in ((0, _SCATTER_LATE), (1, _SCATTER_NEXT)):
        off = 0
        for key in order:
            h = stacked[key].shape[1] // 2
            layout[key] = (which, off, h)
            off += h
    grads = {}
    for name, (key, transposed) in _BIG.items():
        which, off, h = layout[key]
        g = jnp.stack([paired[2 * l + which][:, off:off + h].reshape(2 * h, D) for l in range(depth)])
        grads[name] = g.transpose(0, 2, 1) if transposed else g

    small_shapes = [(128,)] + [small[n].shape for n in _SMALL]
    partial = _pack([loss_blk[0]] + [small_grads[n] for n in _SMALL])
    total = _unpack(_sum_devices(_allgather_devices(partial)), small_shapes)
    loss = total[0][0]
    for n, g in zip(_SMALL, total[1:]):
        if n in _CHANNEL_SHARDED:
            width = wts[n].shape[-1]
            g = lax.dynamic_slice_in_dim(g, chip * width, width, axis=-1)
        grads[n] = g

    delta, new_m, new_v = {}, {}, {}
    for name in _BIG:
        shp = wts[name].shape
        flat = lambda a: a.reshape(-1, shp[-1])
        d, nm, nv = _adamw(flat(wts[name]), flat(grads[name]), flat(mom_m[name]), flat(mom_v[name]))
        delta[name], new_m[name], new_v[name] = d.reshape(shp), nm.reshape(shp), nv.reshape(shp)
    shapes = [wts[n].shape for n in _SMALL]
    d, nm, nv = _adamw(_pack([wts[n] for n in _SMALL]), _pack([grads[n] for n in _SMALL]),
                       _pack([mom_m[n] for n in _SMALL]), _pack([mom_v[n] for n in _SMALL]))
    for out, packed in ((delta, d), (new_m, nm), (new_v, nv)):
        for n, a in zip(_SMALL, _unpack(packed, shapes)):
            out[n] = a

    return (loss, dx[None], *[grads[n] for n in _WEIGHTS], *[delta[n] for n in _WEIGHTS],
            *[new_m[n] for n in _WEIGHTS], *[new_v[n] for n in _WEIGHTS])
```

```python
import jax
import jax.numpy as jnp
from jax import lax
from jax.experimental import pallas as pl
from jax.experimental.pallas import tpu as pltpu

F32 = jnp.float32
BF16 = jnp.bfloat16

EPS = 1e-6
D = 1024
D_POOL, D_CONF, D_SCONV = 256, 384, 384
D_IN = D_POOL + 2 * D_CONF + 3 * D_SCONV
D_FF = 2816
FF_CHUNK = 1408
HEADS, HEAD_DIM = 4, 256
CONF_K = 31
POOL_WINDOWS = (2, 4, 8, 16)
POOL_GROUP = 64
SUBLANES = 8
HALO = 32
PHASE_ROWS = HALO - SUBLANES
TAP_ROWS = 64
FHALO = 16
TM_FWD = 512
TM_BWD = 256
N_CHIPS = 4
N_DEV = 8
MESH_ID = pl.DeviceIdType.MESH
VMEM_LIMIT = 56 << 20

ADAM_LR, ADAM_B1, ADAM_B2, ADAM_EPS, ADAM_WD, ADAM_STEP = 0.001, 0.9, 0.999, 1e-08, 0.01, 10

C_P = (0, 256)
C_A = (256, 640)
C_G = (640, 1024)
C_B = (1024, 1408)
C_C = (1408, 1792)
C_X = (1792, 2176)


def _nn(a, b):
    return jnp.dot(a, b, preferred_element_type=F32)


def _nt(a, b):
    return lax.dot_general(a, b, (((1,), (1,)), ((), ())), preferred_element_type=F32)


def _tn(a, b):
    return lax.dot_general(a, b, (((0,), (0,)), ((), ())), preferred_element_type=F32)


def _sigmoid(v):
    return 1.0 / (1.0 + jnp.exp(-v))


def _rms(v, g):
    r = lax.rsqrt(jnp.mean(v * v, axis=-1, keepdims=True) + EPS)
    vh = v * r
    return vh * g, vh, r


def _rms_bwd(vh, r, g, dy):
    dvh = dy * g
    dv = r * (dvh - vh * jnp.mean(dvh * vh, axis=-1, keepdims=True))
    return dv, jnp.sum(dy * vh, axis=0, keepdims=True)


def _colsum(v):
    return jnp.sum(v, axis=0, keepdims=True)


def _rows(tm, n, nt=None):
    if nt is None:
        return pl.BlockSpec((tm, n), lambda i: (i, 0))
    return pl.BlockSpec((tm, n), lambda i: (nt - 1 - i, 0))


def _const(shape):
    nd = len(shape)
    return pl.BlockSpec(shape, lambda i: (0,) * nd, pipeline_mode=pl.Buffered(1))


def _acc(shape):
    nd = len(shape)
    return pl.BlockSpec(shape, lambda i: (0,) * nd)


def _sds(shape, dtype):
    return jax.ShapeDtypeStruct(shape, dtype)


_HBM = pl.BlockSpec(memory_space=pltpu.HBM)


def _mesh_pos():
    return lax.axis_index("x"), lax.axis_index("y"), lax.axis_index("c")


def _chip_peers():
    x, y, c = _mesh_pos()
    flips = [(1 - x, y), (x, 1 - y), (1 - x, 1 - y)]
    return 2 * x + y, [((px, py, c), 2 * px + py) for px, py in flips]


def _remote(src, dst, send_sems, recv_sems, idx, dev):
    return pltpu.make_async_remote_copy(src_ref=src, dst_ref=dst, send_sem=send_sems.at[idx],
                                        recv_sem=recv_sems.at[idx], device_id=dev, device_id_type=MESH_ID)


class _Plan:
    def __init__(self, arrays):
        self.arrays = list(arrays)

    def scratch(self):
        n = len(self.arrays) * self.n_peers
        return [pltpu.SemaphoreType.DMA((n,)), pltpu.SemaphoreType.DMA((n,)),
                pltpu.SemaphoreType.DMA((len(self.arrays),))]

    def _copies(self, ins, outs, sems):
        send_sems, recv_sems, local_sems = sems
        me, peers = self.peers()
        own, sends, recvs = [], [], []
        for k in range(len(ins)):
            own.append(pltpu.make_async_copy(self.src(ins, k, me), self.dst(outs, k, me), local_sems.at[k]))
            for j, (dev, who) in enumerate(peers):
                idx = self.n_peers * k + j
                sends.append(_remote(self.src(ins, k, who), self.dst(outs, k, me), send_sems, recv_sems, idx, dev))
                recvs.append(_remote(self.src(ins, k, me), self.dst(outs, k, who), send_sems, recv_sems, idx, dev))
        return own, sends, recvs

    def start(self, ins, outs, sems):
        own, sends, _ = self._copies(ins, outs, sems)
        for cp in own + sends:
            cp.start()

    def wait(self, ins, outs, sems):
        own, sends, recvs = self._copies(ins, outs, sems)
        for cp in recvs:
            cp.wait_recv()
        for cp in sends:
            cp.wait_send()
        for cp in own:
            cp.wait()


class _Gather(_Plan):
    tag = "gather"
    n_peers = N_CHIPS - 1

    def peers(self):
        return _chip_peers()

    def out_shapes(self):
        return [_sds((N_CHIPS,) + a.shape, a.dtype) for a in self.arrays]

    def src(self, ins, k, chip):
        return ins[k]

    def dst(self, outs, k, chip):
        return outs[k].at[chip]


class _Scatter(_Plan):
    tag = "scatter"
    n_peers = N_DEV - 1

    def __init__(self, arrays):
        super().__init__(arrays)
        self.rows = [a.shape[0] // N_DEV for a in self.arrays]
        self.offs = [sum(self.rows[:k]) for k in range(len(self.rows))]

    def peers(self):
        x, y, c = _mesh_pos()
        flip = lambda v, on: 1 - v if on else v
        others = [(flip(x, m & 4), flip(y, m & 2), flip(c, m & 1)) for m in range(1, N_DEV)]
        return 4 * x + 2 * y + c, [(dev, 4 * dev[0] + 2 * dev[1] + dev[2]) for dev in others]

    def out_shapes(self):
        a = self.arrays[0]
        return [_sds((N_DEV, sum(self.rows), a.shape[1]), a.dtype)]

    def src(self, ins, k, dev):
        r = self.rows[k]
        return ins[k].at[pl.ds(pl.multiple_of(dev * r, 16), r), :]

    def dst(self, outs, k, dev):
        return outs[0].at[dev, pl.ds(self.offs[k], self.rows[k]), :]


def _fused_call(name, body, grid, in_specs, out_specs, out_shape, scratch, args, comm=None, sem=("arbitrary",)):
    n_in, n_out, n_scr = len(in_specs), len(out_specs), len(scratch)
    c_in = comm.arrays if comm else []
    c_out = comm.out_shapes() if comm else []
    c_scr = comm.scratch() if comm else []

    def kernel_fn(*refs):
        ins, cins = refs[:n_in], refs[n_in:n_in + len(c_in)]
        o0 = n_in + len(c_in)
        outs, couts = refs[o0:o0 + n_out], refs[o0 + n_out:o0 + n_out + len(c_out)]
        s0 = o0 + n_out + len(c_out)
        scr, csems = refs[s0:s0 + n_scr], refs[s0 + n_scr:]
        if comm:
            body(ins, outs, scr, lambda: comm.start(cins, couts, csems), lambda: comm.wait(cins, couts, csems))
        else:
            body(ins, outs, scr, None, None)

    res = pl.pallas_call(
        kernel_fn, grid=grid, name=name + ("_" + comm.tag if comm else ""),
        in_specs=list(in_specs) + [_HBM] * len(c_in), out_specs=list(out_specs) + [_HBM] * len(c_out),
        out_shape=list(out_shape) + c_out, scratch_shapes=list(scratch) + c_scr,
        compiler_params=pltpu.CompilerParams(dimension_semantics=sem, vmem_limit_bytes=VMEM_LIMIT),
    )(*args, *c_in)
    return res[:n_out], res[n_out:]


def _bracket(start, wait, first, last):
    if start is not None:
        pl.when(first)(start)

    def finish():
        if wait is not None:
            pl.when(last)(wait)
    return finish


def _comm_only(plan):
    n_in, n_out = len(plan.arrays), len(plan.out_shapes())

    def body(*refs):
        ins, outs, sems = refs[:n_in], refs[n_in:n_in + n_out], refs[n_in + n_out:]
        plan.start(ins, outs, sems)
        plan.wait(ins, outs, sems)

    return pl.pallas_call(
        body, name=plan.tag + "_chips", in_specs=[_HBM] * n_in, out_specs=[_HBM] * n_out,
        out_shape=plan.out_shapes(), scratch_shapes=plan.scratch(),
    )(*plan.arrays)


def _pair_halves(bufs):
    n = len(bufs)

    def body(*refs):
        outs = refs[n:2 * n]
        send_sems, recv_sems = refs[2 * n:]
        x, y, c = _mesh_pos()
        sibling = (x, y, 1 - c)
        sends = [_remote(outs[k].at[c], outs[k].at[c], send_sems, recv_sems, k, sibling) for k in range(n)]
        for cp in sends:
            cp.start()
        for k in range(n):
            _remote(outs[k].at[c], outs[k].at[1 - c], send_sems, recv_sems, k, sibling).wait_recv()
        for cp in sends:
            cp.wait_send()

    return pl.pallas_call(
        body, name="pair_halves", in_specs=[_HBM] * n, out_specs=[_HBM] * n,
        out_shape=[_sds(a.shape, a.dtype) for a in bufs], input_output_aliases={k: k for k in range(n)},
        scratch_shapes=[pltpu.SemaphoreType.DMA((n,)), pltpu.SemaphoreType.DMA((n,))],
    )(*bufs)


def _allgather_devices(v):
    m_per, n = v.shape

    def body(v_ref, out_ref, send_sems, recv_sems, local_sem):
        x, y, c = _mesh_pos()
        me, sibling = (x, y, c), (x, y, 1 - c)
        chips = [(1 - x, y), (x, 1 - y), (1 - x, 1 - y)]

        def rows(px, py, pc):
            return out_ref.at[4 * px + 2 * py + pc]

        def copy(k, block, to, src=None):
            return _remote(rows(*block) if src is None else src, rows(*block), send_sems, recv_sems, k, to)

        mine = pltpu.make_async_copy(v_ref, rows(*me), local_sem)
        mine.start()
        first = [copy(0, me, sibling, src=v_ref)]
        first += [copy(1 + j, me, (*chip, c), src=v_ref) for j, chip in enumerate(chips)]
        for cp in first:
            cp.start()
        passed = [copy(4 + j, (*chip, c), sibling) for j, chip in enumerate(chips)]
        for j, chip in enumerate(chips):
            copy(1 + j, (*chip, c), me).wait_recv()
            passed[j].start()
        copy(0, sibling, me).wait_recv()
        for j, chip in enumerate(chips):
            copy(4 + j, (*chip, 1 - c), me).wait_recv()
        for cp in first + passed:
            cp.wait_send()
        mine.wait()

    return pl.pallas_call(
        body, name="allgather_devices", out_shape=_sds((N_DEV, m_per, n), v.dtype),
        in_specs=[pl.BlockSpec(memory_space=pltpu.VMEM)], out_specs=pl.BlockSpec(memory_space=pltpu.VMEM),
        scratch_shapes=[pltpu.SemaphoreType.DMA((7,)), pltpu.SemaphoreType.DMA((7,)), pltpu.SemaphoreType.DMA],
        compiler_params=pltpu.CompilerParams(vmem_limit_bytes=VMEM_LIMIT),
    )(v)


def _pool_lane():
    return lax.broadcasted_iota(jnp.int32, (1, D_POOL), 1)


def _pool_count(t0, tm):
    lane = _pool_lane()
    w = jnp.where(lane < 64, 2, jnp.where(lane < 128, 4, jnp.where(lane < 192, 8, 16)))
    pos1 = lax.broadcasted_iota(jnp.int32, (tm, D_POOL), 0) + (t0 + 1)
    return jnp.minimum(pos1, w).astype(F32)


def _fill_bands(band_ref, tm, causal):
    r = lax.broadcasted_iota(jnp.int32, (tm, tm + HALO), 0)
    s = lax.broadcasted_iota(jnp.int32, (tm, tm + HALO), 1)
    d = (r + HALO - s) if causal else (s - r)
    for g, w in enumerate(POOL_WINDOWS):
        band_ref[g] = jnp.where((d >= 0) & (d < w), 1.0, 0.0).astype(BF16)


def _window_sums(band_ref, operand, width):
    lane = _pool_lane()
    res = None
    for g in range(len(POOL_WINDOWS)):
        r = _nn(band_ref[g], operand)
        acc = r[:, 0:width]
        for c0 in range(width, r.shape[1], width):
            acc = acc + r[:, c0:c0 + width]
        res = acc if res is None else jnp.where(lane >= POOL_GROUP * g, acc, res)
    return res


def _phase_copies(src, phases, tm):
    for b in range(1, SUBLANES):
        phases[b - 1] = src[b:b + tm + PHASE_ROWS, :]


def _tap(src, phases, off, rows, r0=0):
    a, b = divmod(off, SUBLANES)
    lo = SUBLANES * a + r0
    if b == 0:
        return src[lo:lo + rows, :]
    return phases[b - 1, lo:lo + rows, :]


def _layer_norm_stats(v1):
    mu = jnp.mean(v1, axis=-1, keepdims=True)
    xc = v1 - mu
    rs = lax.rsqrt(jnp.mean(xc * xc, axis=-1, keepdims=True) + EPS)
    return xc * rs, rs


def _mix_fwd(x, g1, wint, maps_bd, scale, dww, dwb, lng, lnb, sw, wout, g2, comm=None):
    t_len = x.shape[0]
    tm = min(TM_FWD, t_len)
    nt = t_len // tm
    h0 = HALO

    def body(ins, outs, scr, start, wait):
        (x_ref, g1_ref, wint_ref, maps_ref, scale_ref, dww_ref, dwb_ref, lng_ref, lnb_ref, sw_ref, wout_ref,
         g2_ref) = ins
        z_ref, ycat_ref, y_ref, x1_ref, v1_ref, cv_ref, pooled_ref = outs
        pbuf, vbuf, sbuf, phases, band = scr
        i = pl.program_id(0)
        finish = _bracket(start, wait, i == 0, i == nt - 1)

        @pl.when(i == 0)
        def _():
            pbuf[0:h0, :] = jnp.zeros((h0, D_POOL), F32)
            vbuf[0:h0, :] = jnp.zeros((h0, D_CONF), F32)
            sbuf[0:h0, :] = jnp.zeros((h0, D_SCONV), F32)
            _fill_bands(band, tm, True)

        @pl.when(i > 0)
        def _():
            pbuf[0:h0, :] = pbuf[tm:tm + h0, :]
            vbuf[0:h0, :] = vbuf[tm:tm + h0, :]
            sbuf[0:h0, :] = sbuf[tm:tm + h0, :]

        xv = x_ref[...]
        h, _, _ = _rms(xv, g1_ref[...])
        z = _nt(h.astype(BF16), wint_ref[...])
        z_ref[...] = z.astype(BF16)
        zp = z[:, C_P[0]:C_P[1]]
        pbuf[h0:h0 + tm, :] = zp
        vbuf[h0:h0 + tm, :] = z[:, C_A[0]:C_A[1]] * _sigmoid(z[:, C_G[0]:C_G[1]])
        sbuf[h0:h0 + tm, :] = z[:, C_C[0]:C_C[1]] * z[:, C_X[0]:C_X[1]]

        pv = pbuf[...]
        hi = pv.astype(BF16)
        lo = (pv - hi.astype(F32)).astype(BF16)
        sums = _window_sums(band, jnp.concatenate([hi, lo], axis=1), D_POOL)
        pooled = (sums / _pool_count(i * tm, tm) - zp).astype(BF16)
        pooled_ref[...] = pooled
        ycat_ref[:, 0:D_POOL] = (_nn(pooled, maps_ref[...]) * scale_ref[...]).astype(BF16)

        _phase_copies(vbuf, phases, tm)
        base = h0 - (CONF_K - 1)
        for r0 in range(0, tm, TAP_ROWS):
            v1 = dww_ref[0:1, :] * _tap(vbuf, phases, base, TAP_ROWS, r0)
            for j in range(1, CONF_K):
                v1 = v1 + dww_ref[j:j + 1, :] * _tap(vbuf, phases, base + j, TAP_ROWS, r0)
            v1 = v1 + dwb_ref[...]
            v1_ref[r0:r0 + TAP_ROWS, :] = v1
            vh, _ = _layer_norm_stats(v1)
            v2 = vh * lng_ref[...] + lnb_ref[...]
            ycat_ref[r0:r0 + TAP_ROWS, D_POOL:D_POOL + D_CONF] = (v2 * _sigmoid(v2)).astype(BF16)

        cv = (sw_ref[0:1, :] * sbuf[h0 - 2:h0 - 2 + tm, :] + sw_ref[1:2, :] * sbuf[h0 - 1:h0 - 1 + tm, :]
              + sw_ref[2:3, :] * sbuf[h0:h0 + tm, :])
        cv_ref[...] = cv
        ycat_ref[:, D_POOL + D_CONF:D] = (z[:, C_B[0]:C_B[1]] * cv).astype(BF16)

        yb = _nn(ycat_ref[...], wout_ref[...]).astype(BF16)
        y_ref[...] = yb
        yn, _, _ = _rms(yb.astype(F32), g2_ref[...])
        x1_ref[...] = xv + yn
        finish()

    return _fused_call(
        "mix_fwd", body, (nt,),
        [_rows(tm, D), _const((1, D)), _const((D_IN, D)), _const((D_POOL, D_POOL)), _const((1, D_POOL)),
         _const((CONF_K, D_CONF)), _const((1, D_CONF)), _const((1, D_CONF)), _const((1, D_CONF)),
         _const((3, D_SCONV)), _const((D, D)), _const((1, D))],
        [_rows(tm, D_IN), _rows(tm, D), _rows(tm, D), _rows(tm, D), _rows(tm, D_CONF), _rows(tm, D_SCONV),
         _rows(tm, D_POOL)],
        [_sds((t_len, D_IN), BF16), _sds((t_len, D), BF16), _sds((t_len, D), BF16), _sds((t_len, D), F32),
         _sds((t_len, D_CONF), F32), _sds((t_len, D_SCONV), F32), _sds((t_len, D_POOL), BF16)],
        [pltpu.VMEM((h0 + tm, D_POOL), F32), pltpu.VMEM((h0 + tm, D_CONF), F32),
         pltpu.VMEM((h0 + tm, D_SCONV), F32), pltpu.VMEM((SUBLANES - 1, tm + PHASE_ROWS, D_CONF), F32),
         pltpu.VMEM((len(POOL_WINDOWS), tm, tm + h0), BF16)],
        (x, g1, wint, maps_bd, scale, dww, dwb, lng, lnb, sw, wout, g2), comm)


def _mix_bwd(dx1, y, x, z, ycat, v1, cv, pooled, g1, wint, maps_bd, scale, dww, lng, lnb, sw, wout, g2, comm=None):
    t_len = x.shape[0]
    tm = min(TM_BWD, t_len)
    nt = t_len // tm
    h0 = HALO

    def body(ins, outs, scr, start, wait):
        (dx1_ref, y_ref, x_ref, z_ref, ycat_ref, v1_ref, cv_ref, pooled_ref, g1_ref, wint_ref, maps_ref, scale_ref,
         dww_ref, lng_ref, lnb_ref, sw_ref, wout_ref, g2_ref) = ins
        dx_ref, dwin_ref, dwout_ref, dg1_ref, dg2_ref, dmaps_ref, dscale_ref, ddww_ref, misc_ref = outs
        ebuf, dvbuf, dcbuf, phases, band, dzbuf, acc_in, acc_out, ddacc = scr
        s = pl.program_id(0)
        ti = nt - 1 - s
        finish = _bracket(start, wait, s == 0, s == nt - 1)

        @pl.when(s == 0)
        def _():
            ebuf[tm:tm + h0, :] = jnp.zeros((h0, D_POOL), F32)
            dvbuf[tm:tm + h0, :] = jnp.zeros((h0, D_CONF), F32)
            dcbuf[tm:tm + h0, :] = jnp.zeros((h0, D_SCONV), F32)
            _fill_bands(band, tm, False)
            acc_in[...] = jnp.zeros_like(acc_in)
            acc_out[...] = jnp.zeros_like(acc_out)
            ddacc[...] = jnp.zeros_like(ddacc)
            dg1_ref[...] = jnp.zeros_like(dg1_ref)
            dg2_ref[...] = jnp.zeros_like(dg2_ref)
            dmaps_ref[...] = jnp.zeros_like(dmaps_ref)
            dscale_ref[...] = jnp.zeros_like(dscale_ref)
            ddww_ref[...] = jnp.zeros_like(ddww_ref)
            misc_ref[...] = jnp.zeros_like(misc_ref)

        @pl.when(s > 0)
        def _():
            ebuf[tm:tm + h0, :] = ebuf[0:h0, :]
            dvbuf[tm:tm + h0, :] = dvbuf[0:h0, :]
            dcbuf[tm:tm + h0, :] = dcbuf[0:h0, :]

        yv = y_ref[...].astype(F32)
        _, yh, yr = _rms(yv, g2_ref[...])
        dy, dg2 = _rms_bwd(yh, yr, g2_ref[...], dx1_ref[...])
        dg2_ref[...] += dg2
        dyb = dy.astype(BF16)
        acc_out[...] += _tn(ycat_ref[...], dyb)
        dycat = _nt(dyb, wout_ref[...])
        dya = dycat[:, 0:D_POOL]
        dyb2 = dycat[:, D_POOL:D_POOL + D_CONF]
        dyc = dycat[:, D_POOL + D_CONF:D]

        pooled_v = pooled_ref[...]
        pm = _nn(pooled_v, maps_ref[...])
        dscale_ref[...] += _colsum(dya * pm)
        dq = (dya * scale_ref[...]).astype(BF16)
        dmaps_ref[...] += _tn(pooled_v, dq)
        dpooled = _nt(dq, maps_ref[...])
        ebuf[0:tm, :] = dpooled / _pool_count(ti * tm, tm)
        dzbuf[:, C_P[0]:C_P[1]] = (_window_sums(band, ebuf[...].astype(BF16), D_POOL) - dpooled).astype(BF16)

        vh, rs = _layer_norm_stats(v1_ref[...])
        v2 = vh * lng_ref[...] + lnb_ref[...]
        s2 = _sigmoid(v2)
        dv2 = dyb2 * (s2 * (1.0 + v2 * (1.0 - s2)))
        misc_ref[1:2, :] += _colsum(dv2 * vh)
        misc_ref[2:3, :] += _colsum(dv2)
        dvh = dv2 * lng_ref[...]
        dv1 = rs * (dvh - jnp.mean(dvh, axis=-1, keepdims=True) - vh * jnp.mean(dvh * vh, axis=-1, keepdims=True))
        misc_ref[0:1, :] += _colsum(dv1)
        dvbuf[0:tm, :] = dv1
        _phase_copies(dvbuf, phases, tm)
        for r0 in range(0, tm, TAP_ROWS):
            blk = slice(r0, r0 + TAP_ROWS)
            za = z_ref[blk, C_A[0]:C_A[1]].astype(F32)
            sg = _sigmoid(z_ref[blk, C_G[0]:C_G[1]].astype(F32))
            v0 = za * sg
            dv0 = None
            for k in range(CONF_K):
                j = CONF_K - 1 - k
                dk = _tap(dvbuf, phases, k, TAP_ROWS, r0)
                prod = v0 * dk
                part = prod[0:SUBLANES, :]
                for q in range(SUBLANES, TAP_ROWS, SUBLANES):
                    part = part + prod[q:q + SUBLANES, :]
                ddacc[SUBLANES * j:SUBLANES * (j + 1), :] += part
                term = dww_ref[j:j + 1, :] * dk
                dv0 = term if dv0 is None else dv0 + term
            dzbuf[blk, C_A[0]:C_A[1]] = (dv0 * sg).astype(BF16)
            dzbuf[blk, C_G[0]:C_G[1]] = (dv0 * za * sg * (1.0 - sg)).astype(BF16)

        zb = z_ref[:, C_B[0]:C_B[1]].astype(F32)
        zc = z_ref[:, C_C[0]:C_C[1]].astype(F32)
        zx = z_ref[:, C_X[0]:C_X[1]].astype(F32)
        pv = zc * zx
        dzbuf[:, C_B[0]:C_B[1]] = (dyc * cv_ref[...]).astype(BF16)
        dcbuf[0:tm, :] = dyc * zb
        dp = None
        for k in range(3):
            j = 2 - k
            dk = dcbuf[k:k + tm, :]
            misc_ref[3 + j:4 + j, :] += _colsum(pv * dk)
            term = sw_ref[j:j + 1, :] * dk
            dp = term if dp is None else dp + term
        dzbuf[:, C_C[0]:C_C[1]] = (dp * zx).astype(BF16)
        dzbuf[:, C_X[0]:C_X[1]] = (dp * zc).astype(BF16)

        xv = x_ref[...]
        h, xh, xr = _rms(xv, g1_ref[...])
        dz = dzbuf[...]
        acc_in[...] += _tn(dz, h.astype(BF16))
        dh = _nn(dz, wint_ref[...])
        dxn, dg1 = _rms_bwd(xh, xr, g1_ref[...], dh)
        dg1_ref[...] += dg1
        dx_ref[...] = dx1_ref[...] + dxn

        @pl.when(s == nt - 1)
        def _():
            dwin_ref[...] = acc_in[...].astype(BF16)
            dwout_ref[...] = acc_out[...].astype(BF16)
            for j in range(CONF_K):
                ddww_ref[j:j + 1, :] = _colsum(ddacc[SUBLANES * j:SUBLANES * (j + 1), :])

        finish()

    return _fused_call(
        "mix_bwd", body, (nt,),
        [_rows(tm, D, nt), _rows(tm, D, nt), _rows(tm, D, nt), _rows(tm, D_IN, nt), _rows(tm, D, nt),
         _rows(tm, D_CONF, nt), _rows(tm, D_SCONV, nt), _rows(tm, D_POOL, nt), _const((1, D)), _const((D_IN, D)),
         _const((D_POOL, D_POOL)), _const((1, D_POOL)), _const((CONF_K, D_CONF)), _const((1, D_CONF)),
         _const((1, D_CONF)), _const((3, D_SCONV)), _const((D, D)), _const((1, D))],
        [_rows(tm, D, nt), _acc((D_IN, D)), _acc((D, D)), _acc((1, D)), _acc((1, D)), _acc((D_POOL, D_POOL)),
         _acc((1, D_POOL)), _acc((32, D_CONF)), _acc((8, D_CONF))],
        [_sds((t_len, D), F32), _sds((D_IN, D), BF16), _sds((D, D), BF16), _sds((1, D), F32), _sds((1, D), F32),
         _sds((D_POOL, D_POOL), F32), _sds((1, D_POOL), F32), _sds((32, D_CONF), F32), _sds((8, D_CONF), F32)],
        [pltpu.VMEM((tm + h0, D_POOL), F32), pltpu.VMEM((tm + h0, D_CONF), F32),
         pltpu.VMEM((tm + h0, D_SCONV), F32), pltpu.VMEM((SUBLANES - 1, tm + PHASE_ROWS, D_CONF), F32),
         pltpu.VMEM((len(POOL_WINDOWS), tm, tm + h0), BF16), pltpu.VMEM((tm, D_IN), BF16),
         pltpu.VMEM((D_IN, D), F32), pltpu.VMEM((D, D), F32), pltpu.VMEM((SUBLANES * CONF_K, D_CONF), F32)],
        (dx1, y, x, z, ycat, v1, cv, pooled, g1, wint, maps_bd, scale, dww, lng, lnb, sw, wout, g2), comm)


def _kv_fwd(mem, gmem, wk, wv):
    def body(mem_ref, g_ref, wk_ref, wv_ref, k_ref, v_ref):
        mn, _, _ = _rms(mem_ref[...], g_ref[...])
        mnb = mn.astype(BF16)
        k_ref[...] = _nn(mnb, wk_ref[...]).astype(BF16)
        v_ref[...] = _nn(mnb, wv_ref[...]).astype(BF16)

    n = mem.shape[0]
    return pl.pallas_call(
        body, name="kv_fwd", out_shape=[_sds((n, D), BF16), _sds((n, D), BF16)],
        compiler_params=pltpu.CompilerParams(vmem_limit_bytes=VMEM_LIMIT),
    )(mem, gmem, wk, wv)


def _kv_bwd(mem, gmem, dk, dv, wk, wv):
    def body(mem_ref, g_ref, dk_ref, dv_ref, wk_ref, wv_ref, dwk_ref, dwv_ref, dg_ref):
        mn, mh, _ = _rms(mem_ref[...], g_ref[...])
        mnb = mn.astype(BF16)
        dkb = dk_ref[...].astype(BF16)
        dvb = dv_ref[...].astype(BF16)
        dwk_ref[...] = _tn(mnb, dkb).astype(BF16)
        dwv_ref[...] = _tn(mnb, dvb).astype(BF16)
        dmn = _nt(dkb, wk_ref[...]) + _nt(dvb, wv_ref[...])
        dg_ref[...] = _colsum(dmn * mh)

    return pl.pallas_call(
        body, name="kv_bwd", out_shape=[_sds((D, D), BF16), _sds((D, D), BF16), _sds((1, D), F32)],
        compiler_params=pltpu.CompilerParams(vmem_limit_bytes=VMEM_LIMIT),
    )(mem, gmem, dk, dv, wk, wv)


def _softmax_rows(s):
    e = jnp.exp(s - jnp.max(s, axis=-1, keepdims=True))
    return e / jnp.sum(e, axis=-1, keepdims=True)


def _xattn_fwd(x1, g3, wq, k, v, wo, g4, comm=None):
    t_len = x1.shape[0]
    tm = min(2 * TM_FWD, t_len)
    nt = t_len // tm
    n_mem = k.shape[0]
    sc = HEAD_DIM ** -0.5

    def body(ins, outs, scr, start, wait):
        x_ref, g3_ref, wq_ref, k_ref, v_ref, wo_ref, g4_ref = ins
        q_ref, p_ref, o_ref, y_ref, x2_ref = outs
        i = pl.program_id(0)
        finish = _bracket(start, wait, i == 0, i == nt - 1)
        xv = x_ref[...]
        h, _, _ = _rms(xv, g3_ref[...])
        qb = _nn(h.astype(BF16), wq_ref[...]).astype(BF16)
        q_ref[...] = qb
        for hd in range(HEADS):
            sl = slice(hd * HEAD_DIM, (hd + 1) * HEAD_DIM)
            pb = _softmax_rows(_nt(qb[:, sl], k_ref[:, sl]) * sc).astype(BF16)
            p_ref[:, hd * n_mem:(hd + 1) * n_mem] = pb
            o_ref[:, sl] = _nn(pb, v_ref[:, sl]).astype(BF16)
        yb = _nn(o_ref[...], wo_ref[...]).astype(BF16)
        y_ref[...] = yb
        yn, _, _ = _rms(yb.astype(F32), g4_ref[...])
        x2_ref[...] = xv + yn
        finish()

    return _fused_call(
        "xattn_fwd", body, (nt,),
        [_rows(tm, D), _const((1, D)), _const((D, D)), _const((n_mem, D)), _const((n_mem, D)), _const((D, D)),
         _const((1, D))],
        [_rows(tm, D), _rows(tm, HEADS * n_mem), _rows(tm, D), _rows(tm, D), _rows(tm, D)],
        [_sds((t_len, D), BF16), _sds((t_len, HEADS * n_mem), BF16), _sds((t_len, D), BF16), _sds((t_len, D), BF16),
         _sds((t_len, D), F32)],
        [], (x1, g3, wq, k, v, wo, g4), comm)


def _xattn_bwd(dx2, y, x1, q, p, o, g3, wq, k, v, wo, g4):
    t_len = x1.shape[0]
    tm = min(TM_FWD, t_len)
    nt = t_len // tm
    n_mem = k.shape[0]
    sc = HEAD_DIM ** -0.5

    def body(ins, outs, scr, start, wait):
        dx2_ref, y_ref, x_ref, q_ref, p_ref, o_ref, g3_ref, wq_ref, k_ref, v_ref, wo_ref, g4_ref = ins
        dx_ref, dwq_ref, dwo_ref, dk_ref, dv_ref, dg3_ref, dg4_ref = outs
        dqbuf, acc_q, acc_o = scr
        s = pl.program_id(0)

        @pl.when(s == 0)
        def _():
            acc_q[...] = jnp.zeros_like(acc_q)
            acc_o[...] = jnp.zeros_like(acc_o)
            dk_ref[...] = jnp.zeros_like(dk_ref)
            dv_ref[...] = jnp.zeros_like(dv_ref)
            dg3_ref[...] = jnp.zeros_like(dg3_ref)
            dg4_ref[...] = jnp.zeros_like(dg4_ref)

        yv = y_ref[...].astype(F32)
        _, yh, yr = _rms(yv, g4_ref[...])
        dy, dg4 = _rms_bwd(yh, yr, g4_ref[...], dx2_ref[...])
        dg4_ref[...] += dg4
        dyb = dy.astype(BF16)
        acc_o[...] += _tn(o_ref[...], dyb)
        do = _nt(dyb, wo_ref[...])
        qb = q_ref[...]
        for hd in range(HEADS):
            sl = slice(hd * HEAD_DIM, (hd + 1) * HEAD_DIM)
            pb = p_ref[:, hd * n_mem:(hd + 1) * n_mem]
            p = pb.astype(F32)
            dob = do[:, sl].astype(BF16)
            dp = _nt(dob, v_ref[:, sl])
            dv_ref[:, sl] += _tn(pb, dob)
            ds = (p * (dp - jnp.sum(dp * p, axis=-1, keepdims=True)) * sc).astype(BF16)
            dqbuf[:, sl] = _nn(ds, k_ref[:, sl]).astype(BF16)
            dk_ref[:, sl] += _tn(ds, qb[:, sl])
        xv = x_ref[...]
        h, xh, xr = _rms(xv, g3_ref[...])
        dq = dqbuf[...]
        acc_q[...] += _tn(h.astype(BF16), dq)
        dh = _nt(dq, wq_ref[...])
        dxn, dg3 = _rms_bwd(xh, xr, g3_ref[...], dh)
        dg3_ref[...] += dg3
        dx_ref[...] = dx2_ref[...] + dxn

        @pl.when(s == nt - 1)
        def _():
            dwq_ref[...] = acc_q[...].astype(BF16)
            dwo_ref[...] = acc_o[...].astype(BF16)

    outs, _ = _fused_call(
        "xattn_bwd", body, (nt,),
        [_rows(tm, D), _rows(tm, D), _rows(tm, D), _rows(tm, D), _rows(tm, HEADS * n_mem), _rows(tm, D), _const((1, D)),
         _const((D, D)), _const((n_mem, D)), _const((n_mem, D)), _const((D, D)), _const((1, D))],
        [_rows(tm, D), _acc((D, D)), _acc((D, D)), _acc((n_mem, D)), _acc((n_mem, D)), _acc((1, D)), _acc((1, D))],
        [_sds((t_len, D), F32), _sds((D, D), BF16), _sds((D, D), BF16), _sds((n_mem, D), F32),
         _sds((n_mem, D), F32), _sds((1, D), F32), _sds((1, D), F32)],
        [pltpu.VMEM((tm, D), BF16), pltpu.VMEM((D, D), F32), pltpu.VMEM((D, D), F32)],
        (dx2, y, x1, q, p, o, g3, wq, k, v, wo, g4))
    return outs


def _ffn_cols(half, part):
    c0 = part * D_FF + half * FF_CHUNK
    return c0, c0 + FF_CHUNK


def _fill_shifts(ref, tm, step):
    t = lax.broadcasted_iota(jnp.int32, (tm, tm), 0)
    s = lax.broadcasted_iota(jnp.int32, (tm, tm), 1)
    ref[0:tm, :] = jnp.where(s == t + step, 1.0, 0.0).astype(BF16)
    ref[tm:2 * tm, :] = jnp.where(s == t + 2 * step, 1.0, 0.0).astype(BF16)


def _edge_terms(near, far, edge, inner):
    r = lax.broadcasted_iota(jnp.int32, (16, near.shape[1]), 0)
    return jnp.where(r == edge, near, 0.0), jnp.where(r == edge, far, jnp.where(r == inner, near, 0.0))


def _ffn_fwd(x2, g5, wupt, wc, wdown, g6, comm=None):
    t_len = x2.shape[0]
    tm = min(TM_BWD, t_len)
    nt = t_len // tm

    def body(ins, outs, scr, start, wait):
        x_ref, g5_ref, wupt_ref, wc_ref, wdown_ref, g6_ref = ins
        u_ref, c_ref, a_ref, y_ref, x3_ref = outs
        carry, shift = scr
        i = pl.program_id(0)
        finish = _bracket(start, wait, i == 0, i == nt - 1)

        @pl.when(i == 0)
        def _():
            carry[...] = jnp.zeros_like(carry)
            _fill_shifts(shift, tm, -1)

        xv = x_ref[...]
        h, _, _ = _rms(xv, g5_ref[...])
        hb = h.astype(BF16)
        for half in range(2):
            conv = []
            for part in range(2):
                c0, c1 = _ffn_cols(half, part)
                w0, w1, w2 = wc_ref[0:1, c0:c1], wc_ref[1:2, c0:c1], wc_ref[2:3, c0:c1]
                u = _nt(hb, wupt_ref[c0:c1, :])
                ub = u.astype(BF16)
                u_ref[:, c0:c1] = ub
                sh = _nn(shift[...], ub)
                cb = w0 * sh[tm:2 * tm, :] + w1 * sh[0:tm, :] + w2 * u
                c_ref[:, c0:c1] = cb.astype(BF16)
                m1, m2 = _edge_terms(carry[SUBLANES - 1:SUBLANES, c0:c1], carry[SUBLANES - 2:SUBLANES - 1, c0:c1], 0, 1)
                c_ref[0:16, c0:c1] = (cb[0:16, :] + w1 * m1 + w0 * m2).astype(BF16)
                carry[:, c0:c1] = u[tm - SUBLANES:tm, :].astype(BF16).astype(F32)
                conv.append(c_ref[:, c0:c1].astype(F32))
            a = (conv[0] * _sigmoid(conv[0]) * conv[1]).astype(BF16)
            a_ref[:, half * FF_CHUNK:(half + 1) * FF_CHUNK] = a
        yb = _nn(a_ref[...], wdown_ref[...]).astype(BF16)
        y_ref[...] = yb
        yn, _, _ = _rms(yb.astype(F32), g6_ref[...])
        x3_ref[...] = xv + yn
        finish()

    return _fused_call(
        "ffn_fwd", body, (nt,),
        [_rows(tm, D), _const((1, D)), _const((2 * D_FF, D)), _const((3, 2 * D_FF)), _const((D_FF, D)),
         _const((1, D))],
        [_rows(tm, 2 * D_FF), _rows(tm, 2 * D_FF), _rows(tm, D_FF), _rows(tm, D), _rows(tm, D)],
        [_sds((t_len, 2 * D_FF), BF16), _sds((t_len, 2 * D_FF), BF16), _sds((t_len, D_FF), BF16),
         _sds((t_len, D), BF16), _sds((t_len, D), F32)],
        [pltpu.VMEM((SUBLANES, 2 * D_FF), F32), pltpu.VMEM((2 * tm, tm), BF16)],
        (x2, g5, wupt, wc, wdown, g6), comm)


def _ffn_bwd(dx3, y, x2, u, c, g5, wupt, wc, wdown, g6, comm=None):
    t_len = x2.shape[0]
    tm = min(TM_BWD, t_len)
    nt = t_len // tm

    def body(ins, outs, scr, start, wait):
        dx3_ref, y_ref, x_ref, u_ref, c_ref, g5_ref, wupt_ref, wc_ref, wdown_ref, g6_ref = ins
        dx_ref, du_ref, dyo_ref, h_ref, dg5_ref, dg6_ref, dwc_ref = outs
        carry, shift = scr
        s = pl.program_id(0)
        finish = _bracket(start, wait, s == 0, s == nt - 1)

        @pl.when(s == 0)
        def _():
            carry[...] = jnp.zeros_like(carry)
            _fill_shifts(shift, tm, 1)
            dg5_ref[...] = jnp.zeros_like(dg5_ref)
            dg6_ref[...] = jnp.zeros_like(dg6_ref)
            dwc_ref[...] = jnp.zeros_like(dwc_ref)

        yv = y_ref[...].astype(F32)
        _, yh, yr = _rms(yv, g6_ref[...])
        dy, dg6 = _rms_bwd(yh, yr, g6_ref[...], dx3_ref[...])
        dg6_ref[...] += dg6
        dyb = dy.astype(BF16)
        dyo_ref[...] = dyb
        xv = x_ref[...]
        h, xh, xr = _rms(xv, g5_ref[...])
        h_ref[...] = h.astype(BF16)

        for half in range(2):
            g0, g1 = _ffn_cols(half, 0)
            v0, v1 = _ffn_cols(half, 1)
            gt = c_ref[:, g0:g1].astype(F32)
            vl = c_ref[:, v0:v1].astype(F32)
            sg = _sigmoid(gt)
            sil = gt * sg
            da = _nt(dyb, wdown_ref[half * FF_CHUNK:(half + 1) * FF_CHUNK, :])
            dcs = (da * vl * (sg * (1.0 + gt * (1.0 - sg))), da * sil)
            for part in range(2):
                c0, c1 = _ffn_cols(half, part)
                w0, w1, w2 = wc_ref[0:1, c0:c1], wc_ref[1:2, c0:c1], wc_ref[2:3, c0:c1]
                dc = dcs[part]
                sh = _nn(shift[...], dc.astype(BF16))
                d1, d2 = sh[0:tm, :], sh[tm:2 * tm, :]
                m1, m2 = _edge_terms(carry[0:1, c0:c1], carry[1:2, c0:c1], 15, 14)
                carry[:, c0:c1] = dc[0:SUBLANES, :]
                uu = u_ref[:, c0:c1].astype(F32)
                ut = u_ref[tm - 16:tm, c0:c1].astype(F32)
                dwc_ref[2:3, c0:c1] += _colsum(uu * dc)
                dwc_ref[1:2, c0:c1] += _colsum(uu * d1) + _colsum(ut * m1)
                dwc_ref[0:1, c0:c1] += _colsum(uu * d2) + _colsum(ut * m2)
                du = w2 * dc + w1 * d1 + w0 * d2
                du_ref[:, c0:c1] = du.astype(BF16)
                du_ref[tm - 16:tm, c0:c1] = (du[tm - 16:tm, :] + w1 * m1 + w0 * m2).astype(BF16)
        dh = _nn(du_ref[...], wupt_ref[...])
        dxn, dg5 = _rms_bwd(xh, xr, g5_ref[...], dh)
        dg5_ref[...] += dg5
        dx_ref[...] = dx3_ref[...] + dxn
        finish()

    return _fused_call(
        "ffn_bwd", body, (nt,),
        [_rows(tm, D, nt), _rows(tm, D, nt), _rows(tm, D, nt), _rows(tm, 2 * D_FF, nt), _rows(tm, 2 * D_FF, nt),
         _const((1, D)), _const((2 * D_FF, D)), _const((3, 2 * D_FF)), _const((D_FF, D)), _const((1, D))],
        [_rows(tm, D, nt), _rows(tm, 2 * D_FF, nt), _rows(tm, D, nt), _rows(tm, D, nt), _acc((1, D)), _acc((1, D)),
         _acc((8, 2 * D_FF))],
        [_sds((t_len, D), F32), _sds((t_len, 2 * D_FF), BF16), _sds((t_len, D), BF16), _sds((t_len, D), BF16),
         _sds((1, D), F32), _sds((1, D), F32), _sds((8, 2 * D_FF), F32)],
        [pltpu.VMEM((SUBLANES, 2 * D_FF), F32), pltpu.VMEM((2 * tm, tm), BF16)],
        (dx3, y, x2, u, c, g5, wupt, wc, wdown, g6), comm)


def _tn_matmul(a, b):
    t_len, m = a.shape
    bm = FF_CHUNK
    bt = min(2 * TM_FWD, t_len)
    nt = t_len // bt

    def body(a_ref, b_ref, o_ref, acc):
        t = pl.program_id(1)

        @pl.when(t == 0)
        def _():
            acc[...] = jnp.zeros_like(acc)

        acc[...] += _tn(a_ref[...], b_ref[...])

        @pl.when(t == nt - 1)
        def _():
            o_ref[...] = acc[...].astype(BF16)

    return pl.pallas_call(
        body, grid=(m // bm, nt), name="tn_matmul",
        in_specs=[pl.BlockSpec((bt, bm), lambda i, t: (t, i)), pl.BlockSpec((bt, D), lambda i, t: (t, 0))],
        out_specs=pl.BlockSpec((bm, D), lambda i, t: (i, 0)),
        out_shape=_sds((m, D), BF16),
        scratch_shapes=[pltpu.VMEM((bm, D), F32)],
        compiler_params=pltpu.CompilerParams(dimension_semantics=("parallel", "arbitrary"),
                                             vmem_limit_bytes=VMEM_LIMIT),
    )(a, b)


def _loss_grad(xf, target):
    t_len = xf.shape[0]
    tm = min(TM_FWD, t_len)
    nt = t_len // tm

    def body(ins, outs, scr, start, wait):
        x_ref, t_ref = ins
        dx_ref, loss_ref = outs

        @pl.when(pl.program_id(0) == 0)
        def _():
            loss_ref[...] = jnp.zeros_like(loss_ref)

        err = x_ref[...] - t_ref[...]
        dx_ref[...] = err * (1.0 / D)
        part = 0.5 * _colsum(jnp.mean(err * err, axis=-1, keepdims=True))
        loss_ref[...] += jnp.broadcast_to(part, loss_ref.shape)

    outs, _ = _fused_call(
        "loss_grad", body, (nt,), [_rows(tm, D), _rows(tm, D)], [_rows(tm, D), _acc((8, 128))],
        [_sds((t_len, D), F32), _sds((8, 128), F32)], [], (xf, target))
    return outs


BLOCK_BYTES = 1 << 20


def _row_block(rows, cols):
    limit = max(16, BLOCK_BYTES // (4 * cols))
    best = None
    for rb in range(16, min(rows, limit) + 1, 16):
        if rows % rb == 0:
            best = rb
    return best or rows


def _elementwise(name, fn, ins, out_dtypes):
    rows, cols = ins[0].shape
    rb = _row_block(rows, cols)
    n_in = len(ins)

    def body(*refs):
        res = fn(*[r[...] for r in refs[:n_in]])
        for o_ref, r in zip(refs[n_in:], res):
            o_ref[...] = r

    spec = pl.BlockSpec((rb, cols), lambda i: (i, 0))
    return pl.pallas_call(
        body, grid=(rows // rb,), name=name, in_specs=[spec] * n_in, out_specs=[spec] * len(out_dtypes),
        out_shape=[_sds((rows, cols), dt) for dt in out_dtypes],
        compiler_params=pltpu.CompilerParams(dimension_semantics=("parallel",), vmem_limit_bytes=VMEM_LIMIT),
    )(*ins)


def _cast_bf16(w):
    return _elementwise("cast_bf16", lambda v: (v.astype(BF16),), [w], [BF16])[0]


def _adam_math(w, g, m, v):
    nm = ADAM_B1 * m + (1.0 - ADAM_B1) * g
    nv = ADAM_B2 * v + (1.0 - ADAM_B2) * (g * g)
    m_hat = nm / (1.0 - ADAM_B1 ** ADAM_STEP)
    v_hat = nv / (1.0 - ADAM_B2 ** ADAM_STEP)
    return -ADAM_LR * (m_hat / (jnp.sqrt(v_hat) + ADAM_EPS) + ADAM_WD * w), nm, nv


def _adamw(w, g, m, v):
    return _elementwise("adamw", _adam_math, [w, g, m, v], [F32, F32, F32])


def _sum_blocks(parts, slot):
    n, rows, cols = parts.shape
    rb = _row_block(rows, cols)

    def body(slot_ref, p_ref, o_ref):
        acc = p_ref[0].astype(F32)
        for j in range(1, n):
            acc = acc + p_ref[j].astype(F32)
        o_ref[...] = acc

    return pl.pallas_call(
        body, name="sum_blocks", out_shape=_sds((2, rows, cols), F32),
        grid_spec=pltpu.PrefetchScalarGridSpec(
            num_scalar_prefetch=1, grid=(rows // rb,),
            in_specs=[pl.BlockSpec((n, rb, cols), lambda i, s: (0, i, 0))],
            out_specs=pl.BlockSpec((None, rb, cols), lambda i, s: (s[0], i, 0))),
        compiler_params=pltpu.CompilerParams(dimension_semantics=("parallel",), vmem_limit_bytes=VMEM_LIMIT),
    )(slot.reshape(1).astype(jnp.int32), parts)


def _sum_devices(parts):
    n, rows, cols = parts.shape

    def body(p_ref, o_ref):
        acc = p_ref[0]
        for j in range(1, n):
            acc = acc + p_ref[j]
        o_ref[...] = acc

    return pl.pallas_call(
        body, name="sum_devices", out_shape=_sds((rows, cols), F32),
        compiler_params=pltpu.CompilerParams(vmem_limit_bytes=VMEM_LIMIT),
    )(parts)


def _pack(parts):
    flat = jnp.concatenate([p.reshape(-1) for p in parts])
    rows = -(-flat.shape[0] // 1024) * 8
    return jnp.pad(flat, (0, rows * 128 - flat.shape[0])).reshape(rows, 128)


def _unpack(packed, shapes):
    flat = packed.reshape(-1)
    out, off = [], 0
    for shp in shapes:
        size = 1
        for d in shp:
            size *= d
        out.append(flat[off:off + size].reshape(shp))
        off += size
    return out


_BIG = {'w_in': ('wint', True), 'w_out': ('wout', False), 'xattn_wq': ('wq', False), 'xattn_wk': ('wk', False),
        'xattn_wv': ('wv', False), 'xattn_wo': ('wo', False), 'ffn_w_up': ('wupt', True),
        'ffn_w_down': ('wdown', False)}
_KEYS = [key for key, _ in _BIG.values()]
_GATHER_EARLY = ("wint", "wout", "wq", "wk", "wv", "wo")
_GATHER_WITH = {"mix": ("wupt",), "xattn": ("wdown",), "ffn": _GATHER_EARLY}
_SCATTER_LATE = ("wupt", "wdown", "wq", "wk", "wv", "wo")
_SCATTER_NEXT = ("wint", "wout")


def _block_diag(maps):
    out = jnp.zeros((D_POOL, D_POOL), maps.dtype)
    for g in range(len(POOL_WINDOWS)):
        out = lax.dynamic_update_slice(out, maps[g], (g * POOL_GROUP, g * POOL_GROUP))
    return out


def _local_step(x, mem, target, small, big, shards=None):
    distributed = shards is not None
    depth = len(shards) if distributed else len(big)
    big = list(big)
    row = lambda a: a.reshape(1, -1)
    gmem = row(small["mem_norm"])
    saved = []
    for l in range(depth):
        w = big[l]
        nxt = {}

        def plan(stage):
            layer = l + 1 if stage == "ffn" else l
            if distributed and layer < depth:
                return _Gather([shards[layer][key] for key in _GATHER_WITH[stage]])
            return None

        def landed(stage, outs):
            into = nxt if stage == "ffn" else w
            for key, g in zip(_GATHER_WITH[stage], outs):
                into[key] = g.reshape(-1, D)

        sp = dict(
            g1=row(small["mix_pre_norm"][l]), g2=row(small["mix_post_norm"][l]),
            maps=_block_diag(small["pool_maps"][l]).astype(BF16), scale=row(small["pool_scale"][l]),
            dww=small["conf_dw_w"][l], dwb=row(small["conf_dw_b"][l]), lng=row(small["conf_ln_g"][l]),
            lnb=row(small["conf_ln_b"][l]), sw=small["sconv_w"][l],
            g3=row(small["xattn_pre_norm"][l]), g4=row(small["xattn_post_norm"][l]),
            g5=row(small["ffn_pre_norm"][l]), g6=row(small["ffn_post_norm"][l]), wc=small["ffn_conv_w"][l])
        (z, ycat, y1, x1, v1, cv, pooled), got = _mix_fwd(
            x, sp["g1"], w["wint"], sp["maps"], sp["scale"], sp["dww"], sp["dwb"], sp["lng"], sp["lnb"], sp["sw"],
            w["wout"], sp["g2"], plan("mix"))
        landed("mix", got)
        k, v = _kv_fwd(mem, gmem, w["wk"], w["wv"])
        (q, p, o, y2, x2), got = _xattn_fwd(x1, sp["g3"], w["wq"], k, v, w["wo"], sp["g4"], plan("xattn"))
        landed("xattn", got)
        (u, c, a, y3, x3), got = _ffn_fwd(x2, sp["g5"], w["wupt"], sp["wc"], w["wdown"], sp["g6"], plan("ffn"))
        landed("ffn", got)
        if nxt:
            big.append(nxt)
        saved.append(dict(sp=sp, x=x, z=z, ycat=ycat, y1=y1, x1=x1, v1=v1, cv=cv, pooled=pooled, k=k, v=v, q=q, p=p, o=o,
                          y2=y2, x2=x2, u=u, c=c, a=a, y3=y3))
        x = x3

    dx, loss_blk = _loss_grad(x, target)
    big_grads = [None] * depth
    sg = {n: [None] * depth for n in ("mix_pre_norm", "mix_post_norm", "pool_maps", "pool_scale", "conf_dw_w",
                                      "conf_dw_b", "conf_ln_g", "conf_ln_b", "sconv_w", "xattn_pre_norm",
                                      "xattn_post_norm", "ffn_pre_norm", "ffn_post_norm", "ffn_conv_w")}
    dgmem = None
    pending = None
    for l in reversed(range(depth)):
        w, s = big[l], saved[l]
        sp = s["sp"]
        comm = _Scatter(pending) if distributed and pending is not None else None
        (dx, du, dy3, h3, dg5, dg6, dwc), got = _ffn_bwd(dx, s["y3"], s["x2"], s["u"], s["c"], sp["g5"], w["wupt"],
                                                        sp["wc"], w["wdown"], sp["g6"], comm)
        if comm is not None:
            big_grads[l + 1] = (big_grads[l + 1], got[0])
        g = dict(wupt=_tn_matmul(du, h3), wdown=_tn_matmul(s["a"], dy3))
        dx, g["wq"], g["wo"], dk, dv, dg3, dg4 = _xattn_bwd(dx, s["y2"], s["x1"], s["q"], s["p"], s["o"], sp["g3"], w["wq"],
                                                            s["k"], s["v"], w["wo"], sp["g4"])
        g["wk"], g["wv"], dgm = _kv_bwd(mem, gmem, dk, dv, w["wk"], w["wv"])
        dgmem = dgm if dgmem is None else dgmem + dgm
        comm = _Scatter([g[key] for key in _SCATTER_LATE]) if distributed else None
        (dx, g["wint"], g["wout"], dg1, dg2, dmaps, dscale, ddww, misc), got = _mix_bwd(
            dx, s["y1"], s["x"], s["z"], s["ycat"], s["v1"], s["cv"], s["pooled"], sp["g1"], w["wint"], sp["maps"],
            sp["scale"], sp["dww"], sp["lng"], sp["lnb"], sp["sw"], w["wout"], sp["g2"], comm)
        if distributed:
            big_grads[l] = got[0]
            pending = [g[key] for key in _SCATTER_NEXT]
        else:
            big_grads[l] = g
        sg["mix_pre_norm"][l] = dg1[0]
        sg["mix_post_norm"][l] = dg2[0]
        sg["pool_maps"][l] = jnp.stack([dmaps[i * 64:(i + 1) * 64, i * 64:(i + 1) * 64] for i in range(4)])
        sg["pool_scale"][l] = dscale[0]
        sg["conf_dw_w"][l] = ddww[0:CONF_K]
        sg["conf_dw_b"][l] = misc[0]
        sg["conf_ln_g"][l] = misc[1]
        sg["conf_ln_b"][l] = misc[2]
        sg["sconv_w"][l] = misc[3:6]
        sg["xattn_pre_norm"][l] = dg3[0]
        sg["xattn_post_norm"][l] = dg4[0]
        sg["ffn_pre_norm"][l] = dg5[0]
        sg["ffn_post_norm"][l] = dg6[0]
        sg["ffn_conv_w"][l] = dwc[0:3]
    if distributed:
        big_grads[0] = (big_grads[0], _comm_only(_Scatter(pending))[0])
    small_grads = {n: jnp.stack(vs) for n, vs in sg.items()}
    small_grads["mem_norm"] = dgmem[0]
    return loss_blk, dx, big_grads, small_grads


_WEIGHTS = ['mem_norm', 'mix_pre_norm', 'mix_post_norm', 'w_in', 'pool_maps', 'pool_scale', 'conf_dw_w', 'conf_dw_b',
            'conf_ln_g', 'conf_ln_b', 'sconv_w', 'w_out', 'xattn_pre_norm', 'xattn_post_norm', 'xattn_wq',
            'xattn_wk', 'xattn_wv', 'xattn_wo', 'ffn_pre_norm', 'ffn_post_norm', 'ffn_w_up', 'ffn_conv_w',
            'ffn_w_down']
_CHANNEL_SHARDED = ('conf_dw_w', 'sconv_w', 'ffn_conv_w')
_SMALL = [n for n in _WEIGHTS if n not in _BIG]


def kernel(x, mem, mem_norm, mix_pre_norm, mix_post_norm, w_in, pool_maps, pool_scale, conf_dw_w, conf_dw_b, conf_ln_g, conf_ln_b, sconv_w, w_out, xattn_pre_norm, xattn_post_norm, xattn_wq, xattn_wk, xattn_wv, xattn_wo, ffn_pre_norm, ffn_post_norm, ffn_w_up, ffn_conv_w, ffn_w_down, loss_target, m_mem_norm, m_mix_pre_norm, m_mix_post_norm, m_w_in, m_pool_maps, m_pool_scale, m_conf_dw_w, m_conf_dw_b, m_conf_ln_g, m_conf_ln_b, m_sconv_w, m_w_out, m_xattn_pre_norm, m_xattn_post_norm, m_xattn_wq, m_xattn_wk, m_xattn_wv, m_xattn_wo, m_ffn_pre_norm, m_ffn_post_norm, m_ffn_w_up, m_ffn_conv_w, m_ffn_w_down, v_mem_norm, v_mix_pre_norm, v_mix_post_norm, v_w_in, v_pool_maps, v_pool_scale, v_conf_dw_w, v_conf_dw_b, v_conf_ln_g, v_conf_ln_b, v_sconv_w, v_w_out, v_xattn_pre_norm, v_xattn_post_norm, v_xattn_wq, v_xattn_wk, v_xattn_wv, v_xattn_wo, v_ffn_pre_norm, v_ffn_post_norm, v_ffn_w_up, v_ffn_conv_w, v_ffn_w_down):
    wts = dict(mem_norm=mem_norm, mix_pre_norm=mix_pre_norm, mix_post_norm=mix_post_norm, w_in=w_in,
               pool_maps=pool_maps, pool_scale=pool_scale, conf_dw_w=conf_dw_w, conf_dw_b=conf_dw_b,
               conf_ln_g=conf_ln_g, conf_ln_b=conf_ln_b, sconv_w=sconv_w, w_out=w_out,
               xattn_pre_norm=xattn_pre_norm, xattn_post_norm=xattn_post_norm, xattn_wq=xattn_wq,
               xattn_wk=xattn_wk, xattn_wv=xattn_wv, xattn_wo=xattn_wo, ffn_pre_norm=ffn_pre_norm,
               ffn_post_norm=ffn_post_norm, ffn_w_up=ffn_w_up, ffn_conv_w=ffn_conv_w, ffn_w_down=ffn_w_down)
    mom_m = dict(mem_norm=m_mem_norm, mix_pre_norm=m_mix_pre_norm, mix_post_norm=m_mix_post_norm, w_in=m_w_in,
                 pool_maps=m_pool_maps, pool_scale=m_pool_scale, conf_dw_w=m_conf_dw_w, conf_dw_b=m_conf_dw_b,
                 conf_ln_g=m_conf_ln_g, conf_ln_b=m_conf_ln_b, sconv_w=m_sconv_w, w_out=m_w_out,
                 xattn_pre_norm=m_xattn_pre_norm, xattn_post_norm=m_xattn_post_norm, xattn_wq=m_xattn_wq,
                 xattn_wk=m_xattn_wk, xattn_wv=m_xattn_wv, xattn_wo=m_xattn_wo, ffn_pre_norm=m_ffn_pre_norm,
                 ffn_post_norm=m_ffn_post_norm, ffn_w_up=m_ffn_w_up, ffn_conv_w=m_ffn_conv_w,
                 ffn_w_down=m_ffn_w_down)
    mom_v = dict(mem_norm=v_mem_norm, mix_pre_norm=v_mix_pre_norm, mix_post_norm=v_mix_post_norm, w_in=v_w_in,
                 pool_maps=v_pool_maps, pool_scale=v_pool_scale, conf_dw_w=v_conf_dw_w, conf_dw_b=v_conf_dw_b,
                 conf_ln_g=v_conf_ln_g, conf_ln_b=v_conf_ln_b, sconv_w=v_sconv_w, w_out=v_w_out,
                 xattn_pre_norm=v_xattn_pre_norm, xattn_post_norm=v_xattn_post_norm, xattn_wq=v_xattn_wq,
                 xattn_wk=v_xattn_wk, xattn_wv=v_xattn_wv, xattn_wo=v_xattn_wo, ffn_pre_norm=v_ffn_pre_norm,
                 ffn_post_norm=v_ffn_post_norm, ffn_w_up=v_ffn_w_up, ffn_conv_w=v_ffn_conv_w,
                 ffn_w_down=v_ffn_w_down)
    depth = w_in.shape[0]
    chip = 2 * lax.axis_index("x") + lax.axis_index("y")

    stacked = {}
    for name, (key, transposed) in _BIG.items():
        w = wts[name]
        wb = _cast_bf16(w.reshape(-1, w.shape[-1])).reshape(w.shape)
        stacked[key] = wb.transpose(0, 2, 1) if transposed else wb
    shards = [{key: stacked[key][l] for key in _KEYS} for l in range(depth)]
    first = _comm_only(_Gather([shards[0][key] for key in _GATHER_EARLY]))
    big0 = {key: g.reshape(-1, D) for key, g in zip(_GATHER_EARLY, first)}

    conv_shapes = [wts[n].shape for n in _CHANNEL_SHARDED]
    conv_all = _allgather_devices(_pack([wts[n] for n in _CHANNEL_SHARDED]))
    per_chip = [_unpack(conv_all[2 * j], conv_shapes) for j in range(N_CHIPS)]
    small = {n: wts[n] for n in _SMALL}
    for i, n in enumerate(_CHANNEL_SHARDED):
        small[n] = jnp.concatenate([per_chip[j][i] for j in range(N_CHIPS)], axis=-1)

    loss_blk, dx, landings, small_grads = _local_step(x[0], mem[0], loss_target[0], small, [big0], shards)

    core = lax.axis_index("c")
    paired = _pair_halves([_sum_blocks(buf, core) for pair in landings for buf in pair])
    layout = {}
    for which, order in ((0, _SCATTER_LATE), (1, _SCATTER_NEXT)):
        off = 0
        for key in order:
            h = stacked[key].shape[1] // 2
            layout[key] = (which, off, h)
            off += h
    grads = {}
    for name, (key, transposed) in _BIG.items():
        which, off, h = layout[key]
        g = jnp.stack([paired[2 * l + which][:, off:off + h].reshape(2 * h, D) for l in range(depth)])
        grads[name] = g.transpose(0, 2, 1) if transposed else g

    small_shapes = [(128,)] + [small[n].shape for n in _SMALL]
    partial = _pack([loss_blk[0]] + [small_grads[n] for n in _SMALL])
    total = _unpack(_sum_devices(_allgather_devices(partial)), small_shapes)
    loss = total[0][0]
    for n, g in zip(_SMALL, total[1:]):
        if n in _CHANNEL_SHARDED:
            width = wts[n].shape[-1]
            g = lax.dynamic_slice_in_dim(g, chip * width, width, axis=-1)
        grads[n] = g

    delta, new_m, new_v = {}, {}, {}
    for name in _BIG:
        shp = wts[name].shape
        flat = lambda a: a.reshape(-1, shp[-1])
        d, nm, nv = _adamw(flat(wts[name]), flat(grads[name]), flat(mom_m[name]), flat(mom_v[name]))
        delta[name], new_m[name], new_v[name] = d.reshape(shp), nm.reshape(shp), nv.reshape(shp)
    shapes = [wts[n].shape for n in _SMALL]
    d, nm, nv = _adamw(_pack([wts[n] for n in _SMALL]), _pack([grads[n] for n in _SMALL]),
                       _pack([mom_m[n] for n in _SMALL]), _pack([mom_v[n] for n in _SMALL]))
    for out, packed in ((delta, d), (new_m, nm), (new_v, nv)):
        for n, a in zip(_SMALL, _unpack(packed, shapes)):
            out[n] = a

    return (loss, dx[None], *[grads[n] for n in _WEIGHTS], *[delta[n] for n in _WEIGHTS],
            *[new_m[n] for n in _WEIGHTS], *[new_v[n] for n in _WEIGHTS])
```

```python
import jax
import jax.numpy as jnp
from jax import lax
from jax.experimental import pallas as pl
from jax.experimental.pallas import tpu as pltpu

F32 = jnp.float32
BF16 = jnp.bfloat16

EPS = 1e-6
D = 1024
D_POOL, D_CONF, D_SCONV = 256, 384, 384
D_IN = D_POOL + 2 * D_CONF + 3 * D_SCONV
D_FF = 2816
FF_CHUNK = 1408
HEADS, HEAD_DIM = 4, 256
CONF_K = 31
POOL_WINDOWS = (2, 4, 8, 16)
POOL_GROUP = 64
SUBLANES = 8
HALO = 32
PHASE_ROWS = HALO - SUBLANES
TAP_ROWS = 64
FHALO = 16
TM_FWD = 512
TM_BWD = 256
N_CHIPS = 4
N_DEV = 8
MESH_ID = pl.DeviceIdType.MESH
VMEM_LIMIT = 56 << 20

ADAM_LR, ADAM_B1, ADAM_B2, ADAM_EPS, ADAM_WD, ADAM_STEP = 0.001, 0.9, 0.999, 1e-08, 0.01, 10

C_P = (0, 256)
C_A = (256, 640)
C_G = (640, 1024)
C_B = (1024, 1408)
C_C = (1408, 1792)
C_X = (1792, 2176)


def _nn(a, b):
    return jnp.dot(a, b, preferred_element_type=F32)


def _nt(a, b):
    return lax.dot_general(a, b, (((1,), (1,)), ((), ())), preferred_element_type=F32)


def _tn(a, b):
    return lax.dot_general(a, b, (((0,), (0,)), ((), ())), preferred_element_type=F32)


def _sigmoid(v):
    return 1.0 / (1.0 + jnp.exp(-v))


def _rms(v, g):
    r = lax.rsqrt(jnp.mean(v * v, axis=-1, keepdims=True) + EPS)
    vh = v * r
    return vh * g, vh, r


def _rms_bwd(vh, r, g, dy):
    dvh = dy * g
    dv = r * (dvh - vh * jnp.mean(dvh * vh, axis=-1, keepdims=True))
    return dv, jnp.sum(dy * vh, axis=0, keepdims=True)


def _colsum(v):
    return jnp.sum(v, axis=0, keepdims=True)


def _rows(tm, n, nt=None):
    if nt is None:
        return pl.BlockSpec((tm, n), lambda i: (i, 0))
    return pl.BlockSpec((tm, n), lambda i: (nt - 1 - i, 0))


def _const(shape):
    nd = len(shape)
    return pl.BlockSpec(shape, lambda i: (0,) * nd, pipeline_mode=pl.Buffered(1))


def _acc(shape):
    nd = len(shape)
    return pl.BlockSpec(shape, lambda i: (0,) * nd)


def _sds(shape, dtype):
    return jax.ShapeDtypeStruct(shape, dtype)


_HBM = pl.BlockSpec(memory_space=pltpu.HBM)


def _mesh_pos():
    return lax.axis_index("x"), lax.axis_index("y"), lax.axis_index("c")


def _chip_peers():
    x, y, c = _mesh_pos()
    flips = [(1 - x, y), (x, 1 - y), (1 - x, 1 - y)]
    return 2 * x + y, [((px, py, c), 2 * px + py) for px, py in flips]


def _remote(src, dst, send_sems, recv_sems, idx, dev):
    return pltpu.make_async_remote_copy(src_ref=src, dst_ref=dst, send_sem=send_sems.at[idx],
                                        recv_sem=recv_sems.at[idx], device_id=dev, device_id_type=MESH_ID)


class _Plan:
    def __init__(self, arrays):
        self.arrays = list(arrays)

    def scratch(self):
        n = len(self.arrays) * self.n_peers
        return [pltpu.SemaphoreType.DMA((n,)), pltpu.SemaphoreType.DMA((n,)),
                pltpu.SemaphoreType.DMA((len(self.arrays),))]

    def _copies(self, ins, outs, sems):
        send_sems, recv_sems, local_sems = sems
        me, peers = self.peers()
        own, sends, recvs = [], [], []
        for k in range(len(ins)):
            own.append(pltpu.make_async_copy(self.src(ins, k, me), self.dst(outs, k, me), local_sems.at[k]))
            for j, (dev, who) in enumerate(peers):
                idx = self.n_peers * k + j
                sends.append(_remote(self.src(ins, k, who), self.dst(outs, k, me), send_sems, recv_sems, idx, dev))
                recvs.append(_remote(self.src(ins, k, me), self.dst(outs, k, who), send_sems, recv_sems, idx, dev))
        return own, sends, recvs

    def start(self, ins, outs, sems):
        own, sends, _ = self._copies(ins, outs, sems)
        for cp in own + sends:
            cp.start()

    def wait(self, ins, outs, sems):
        own, sends, recvs = self._copies(ins, outs, sems)
        for cp in recvs:
            cp.wait_recv()
        for cp in sends:
            cp.wait_send()
        for cp in own:
            cp.wait()


class _Gather(_Plan):
    tag = "gather"
    n_peers = N_CHIPS - 1

    def peers(self):
        return _chip_peers()

    def out_shapes(self):
        return [_sds((N_CHIPS,) + a.shape, a.dtype) for a in self.arrays]

    def src(self, ins, k, chip):
        return ins[k]

    def dst(self, outs, k, chip):
        return outs[k].at[chip]


class _Scatter(_Plan):
    tag = "scatter"
    n_peers = N_DEV - 1

    def __init__(self, arrays):
        super().__init__(arrays)
        self.rows = [a.shape[0] // N_DEV for a in self.arrays]
        self.offs = [sum(self.rows[:k]) for k in range(len(self.rows))]

    def peers(self):
        x, y, c = _mesh_pos()
        flip = lambda v, on: 1 - v if on else v
        others = [(flip(x, m & 4), flip(y, m & 2), flip(c, m & 1)) for m in range(1, N_DEV)]
        return 4 * x + 2 * y + c, [(dev, 4 * dev[0] + 2 * dev[1] + dev[2]) for dev in others]

    def out_shapes(self):
        a = self.arrays[0]
        return [_sds((N_DEV, sum(self.rows), a.shape[1]), a.dtype)]

    def src(self, ins, k, dev):
        r = self.rows[k]
        return ins[k].at[pl.ds(pl.multiple_of(dev * r, 16), r), :]

    def dst(self, outs, k, dev):
        return outs[0].at[dev, pl.ds(self.offs[k], self.rows[k]), :]


def _fused_call(name, body, grid, in_specs, out_specs, out_shape, scratch, args, comm=None, sem=("arbitrary",)):
    n_in, n_out, n_scr = len(in_specs), len(out_specs), len(scratch)
    c_in = comm.arrays if comm else []
    c_out = comm.out_shapes() if comm else []
    c_scr = comm.scratch() if comm else []

    def kernel_fn(*refs):
        ins, cins = refs[:n_in], refs[n_in:n_in + len(c_in)]
        o0 = n_in + len(c_in)
        outs, couts = refs[o0:o0 + n_out], refs[o0 + n_out:o0 + n_out + len(c_out)]
        s0 = o0 + n_out + len(c_out)
        scr, csems = refs[s0:s0 + n_scr], refs[s0 + n_scr:]
        if comm:
            body(ins, outs, scr, lambda: comm.start(cins, couts, csems), lambda: comm.wait(cins, couts, csems))
        else:
            body(ins, outs, scr, None, None)

    res = pl.pallas_call(
        kernel_fn, grid=grid, name=name + ("_" + comm.tag if comm else ""),
        in_specs=list(in_specs) + [_HBM] * len(c_in), out_specs=list(out_specs) + [_HBM] * len(c_out),
        out_shape=list(out_shape) + c_out, scratch_shapes=list(scratch) + c_scr,
        compiler_params=pltpu.CompilerParams(dimension_semantics=sem, vmem_limit_bytes=VMEM_LIMIT),
    )(*args, *c_in)
    return res[:n_out], res[n_out:]


def _bracket(start, wait, first, last):
    if start is not None:
        pl.when(first)(start)

    def finish():
        if wait is not None:
            pl.when(last)(wait)
    return finish


def _gather_two_level(shards):
    n = len(shards)
    per = 2 * N_CHIPS - 1

    def body(*refs):
        ins, outs = refs[:n], refs[n:2 * n]
        send_sems, recv_sems = refs[2 * n:]
        x, y, c = _mesh_pos()
        me, peers = _chip_peers()
        sibling = (x, y, 1 - c)

        def half(ref, k, core):
            h = shards[k].shape[0] // 2
            return ref.at[pl.ds(pl.multiple_of(core * h, 16), h), :]

        first, passed = [], []
        for k in range(n):
            first.append(_remote(ins[k], outs[k].at[me], send_sems, recv_sems, per * k + 3, sibling))
            for j, (dev, chip) in enumerate(peers):
                first.append(_remote(half(ins[k], k, c), half(outs[k].at[me], k, c), send_sems, recv_sems,
                                     per * k + j, dev))
        for cp in first:
            cp.start()
        for k in range(n):
            for j, (dev, chip) in enumerate(peers):
                landed = half(outs[k].at[chip], k, c)
                _remote(landed, landed, send_sems, recv_sems, per * k + j, dev).wait_recv()
                fwd = _remote(landed, landed, send_sems, recv_sems, per * k + 4 + j, sibling)
                fwd.start()
                passed.append(fwd)
        for k in range(n):
            _remote(ins[k], outs[k].at[me], send_sems, recv_sems, per * k + 3, sibling).wait_recv()
            for j, (dev, chip) in enumerate(peers):
                other = half(outs[k].at[chip], k, 1 - c)
                _remote(other, other, send_sems, recv_sems, per * k + 4 + j, sibling).wait_recv()
        for cp in first + passed:
            cp.wait_send()

    return pl.pallas_call(
        body, name="gather_two_level", in_specs=[_HBM] * n, out_specs=[_HBM] * n,
        out_shape=[_sds((N_CHIPS,) + s.shape, s.dtype) for s in shards],
        scratch_shapes=[pltpu.SemaphoreType.DMA((per * n,)), pltpu.SemaphoreType.DMA((per * n,))],
    )(*shards)


def _pair_halves(bufs):
    n = len(bufs)

    def body(*refs):
        outs = refs[n:2 * n]
        send_sems, recv_sems = refs[2 * n:]
        x, y, c = _mesh_pos()
        sibling = (x, y, 1 - c)
        sends = [_remote(outs[k].at[c], outs[k].at[c], send_sems, recv_sems, k, sibling) for k in range(n)]
        for cp in sends:
            cp.start()
        for k in range(n):
            _remote(outs[k].at[c], outs[k].at[1 - c], send_sems, recv_sems, k, sibling).wait_recv()
        for cp in sends:
            cp.wait_send()

    return pl.pallas_call(
        body, name="pair_halves", in_specs=[_HBM] * n, out_specs=[_HBM] * n,
        out_shape=[_sds(a.shape, a.dtype) for a in bufs], input_output_aliases={k: k for k in range(n)},
        scratch_shapes=[pltpu.SemaphoreType.DMA((n,)), pltpu.SemaphoreType.DMA((n,))],
    )(*bufs)


def _allgather_devices(v):
    m_per, n = v.shape

    def body(v_ref, out_ref, send_sems, recv_sems, local_sem):
        x, y, c = _mesh_pos()
        me, sibling = (x, y, c), (x, y, 1 - c)
        chips = [(1 - x, y), (x, 1 - y), (1 - x, 1 - y)]

        def rows(px, py, pc):
            return out_ref.at[4 * px + 2 * py + pc]

        def copy(k, block, to, src=None):
            return _remote(rows(*block) if src is None else src, rows(*block), send_sems, recv_sems, k, to)

        mine = pltpu.make_async_copy(v_ref, rows(*me), local_sem)
        mine.start()
        first = [copy(0, me, sibling, src=v_ref)]
        first += [copy(1 + j, me, (*chip, c), src=v_ref) for j, chip in enumerate(chips)]
        for cp in first:
            cp.start()
        passed = [copy(4 + j, (*chip, c), sibling) for j, chip in enumerate(chips)]
        for j, chip in enumerate(chips):
            copy(1 + j, (*chip, c), me).wait_recv()
            passed[j].start()
        copy(0, sibling, me).wait_recv()
        for j, chip in enumerate(chips):
            copy(4 + j, (*chip, 1 - c), me).wait_recv()
        for cp in first + passed:
            cp.wait_send()
        mine.wait()

    return pl.pallas_call(
        body, name="allgather_devices", out_shape=_sds((N_DEV, m_per, n), v.dtype),
        in_specs=[pl.BlockSpec(memory_space=pltpu.VMEM)], out_specs=pl.BlockSpec(memory_space=pltpu.VMEM),
        scratch_shapes=[pltpu.SemaphoreType.DMA((7,)), pltpu.SemaphoreType.DMA((7,)), pltpu.SemaphoreType.DMA],
        compiler_params=pltpu.CompilerParams(vmem_limit_bytes=VMEM_LIMIT),
    )(v)


def _pool_lane():
    return lax.broadcasted_iota(jnp.int32, (1, D_POOL), 1)


def _pool_count(t0, tm):
    lane = _pool_lane()
    w = jnp.where(lane < 64, 2, jnp.where(lane < 128, 4, jnp.where(lane < 192, 8, 16)))
    pos1 = lax.broadcasted_iota(jnp.int32, (tm, D_POOL), 0) + (t0 + 1)
    return jnp.minimum(pos1, w).astype(F32)


def _fill_bands(band_ref, tm, causal):
    r = lax.broadcasted_iota(jnp.int32, (tm, tm + HALO), 0)
    s = lax.broadcasted_iota(jnp.int32, (tm, tm + HALO), 1)
    d = (r + HALO - s) if causal else (s - r)
    for g, w in enumerate(POOL_WINDOWS):
        band_ref[g] = jnp.where((d >= 0) & (d < w), 1.0, 0.0).astype(BF16)


def _window_sums(band_ref, operand, width):
    lane = _pool_lane()
    res = None
    for g in range(len(POOL_WINDOWS)):
        r = _nn(band_ref[g], operand)
        acc = r[:, 0:width]
        for c0 in range(width, r.shape[1], width):
            acc = acc + r[:, c0:c0 + width]
        res = acc if res is None else jnp.where(lane >= POOL_GROUP * g, acc, res)
    return res


def _phase_copies(src, phases, tm):
    for b in range(1, SUBLANES):
        phases[b - 1] = src[b:b + tm + PHASE_ROWS, :]


def _tap(src, phases, off, rows, r0=0):
    a, b = divmod(off, SUBLANES)
    lo = SUBLANES * a + r0
    if b == 0:
        return src[lo:lo + rows, :]
    return phases[b - 1, lo:lo + rows, :]


def _layer_norm_stats(v1):
    mu = jnp.mean(v1, axis=-1, keepdims=True)
    xc = v1 - mu
    rs = lax.rsqrt(jnp.mean(xc * xc, axis=-1, keepdims=True) + EPS)
    return xc * rs, rs


def _mix_fwd(x, g1, wint, maps_bd, scale, dww, dwb, lng, lnb, sw, wout, g2, comm=None):
    t_len = x.shape[0]
    tm = min(TM_FWD, t_len)
    nt = t_len // tm
    h0 = HALO

    def body(ins, outs, scr, start, wait):
        (x_ref, g1_ref, wint_ref, maps_ref, scale_ref, dww_ref, dwb_ref, lng_ref, lnb_ref, sw_ref, wout_ref,
         g2_ref) = ins
        z_ref, ycat_ref, y_ref, x1_ref, v1_ref, cv_ref, pooled_ref = outs
        pbuf, vbuf, sbuf, phases, band = scr
        i = pl.program_id(0)
        finish = _bracket(start, wait, i == 0, i == nt - 1)

        @pl.when(i == 0)
        def _():
            pbuf[0:h0, :] = jnp.zeros((h0, D_POOL), F32)
            vbuf[0:h0, :] = jnp.zeros((h0, D_CONF), F32)
            sbuf[0:h0, :] = jnp.zeros((h0, D_SCONV), F32)
            _fill_bands(band, tm, True)

        @pl.when(i > 0)
        def _():
            pbuf[0:h0, :] = pbuf[tm:tm + h0, :]
            vbuf[0:h0, :] = vbuf[tm:tm + h0, :]
            sbuf[0:h0, :] = sbuf[tm:tm + h0, :]

        xv = x_ref[...]
        h, _, _ = _rms(xv, g1_ref[...])
        z = _nt(h.astype(BF16), wint_ref[...])
        z_ref[...] = z.astype(BF16)
        zp = z[:, C_P[0]:C_P[1]]
        pbuf[h0:h0 + tm, :] = zp
        vbuf[h0:h0 + tm, :] = z[:, C_A[0]:C_A[1]] * _sigmoid(z[:, C_G[0]:C_G[1]])
        sbuf[h0:h0 + tm, :] = z[:, C_C[0]:C_C[1]] * z[:, C_X[0]:C_X[1]]

        pv = pbuf[...]
        hi = pv.astype(BF16)
        lo = (pv - hi.astype(F32)).astype(BF16)
        sums = _window_sums(band, jnp.concatenate([hi, lo], axis=1), D_POOL)
        pooled = (sums / _pool_count(i * tm, tm) - zp).astype(BF16)
        pooled_ref[...] = pooled
        ycat_ref[:, 0:D_POOL] = (_nn(pooled, maps_ref[...]) * scale_ref[...]).astype(BF16)

        _phase_copies(vbuf, phases, tm)
        base = h0 - (CONF_K - 1)
        for r0 in range(0, tm, TAP_ROWS):
            v1 = dww_ref[0:1, :] * _tap(vbuf, phases, base, TAP_ROWS, r0)
            for j in range(1, CONF_K):
                v1 = v1 + dww_ref[j:j + 1, :] * _tap(vbuf, phases, base + j, TAP_ROWS, r0)
            v1 = v1 + dwb_ref[...]
            v1_ref[r0:r0 + TAP_ROWS, :] = v1
            vh, _ = _layer_norm_stats(v1)
            v2 = vh * lng_ref[...] + lnb_ref[...]
            ycat_ref[r0:r0 + TAP_ROWS, D_POOL:D_POOL + D_CONF] = (v2 * _sigmoid(v2)).astype(BF16)

        cv = (sw_ref[0:1, :] * sbuf[h0 - 2:h0 - 2 + tm, :] + sw_ref[1:2, :] * sbuf[h0 - 1:h0 - 1 + tm, :]
              + sw_ref[2:3, :] * sbuf[h0:h0 + tm, :])
        cv_ref[...] = cv
        ycat_ref[:, D_POOL + D_CONF:D] = (z[:, C_B[0]:C_B[1]] * cv).astype(BF16)

        yb = _nn(ycat_ref[...], wout_ref[...]).astype(BF16)
        y_ref[...] = yb
        yn, _, _ = _rms(yb.astype(F32), g2_ref[...])
        x1_ref[...] = xv + yn
        finish()

    return _fused_call(
        "mix_fwd", body, (nt,),
        [_rows(tm, D), _const((1, D)), _const((D_IN, D)), _const((D_POOL, D_POOL)), _const((1, D_POOL)),
         _const((CONF_K, D_CONF)), _const((1, D_CONF)), _const((1, D_CONF)), _const((1, D_CONF)),
         _const((3, D_SCONV)), _const((D, D)), _const((1, D))],
        [_rows(tm, D_IN), _rows(tm, D), _rows(tm, D), _rows(tm, D), _rows(tm, D_CONF), _rows(tm, D_SCONV),
         _rows(tm, D_POOL)],
        [_sds((t_len, D_IN), BF16), _sds((t_len, D), BF16), _sds((t_len, D), BF16), _sds((t_len, D), F32),
         _sds((t_len, D_CONF), F32), _sds((t_len, D_SCONV), F32), _sds((t_len, D_POOL), BF16)],
        [pltpu.VMEM((h0 + tm, D_POOL), F32), pltpu.VMEM((h0 + tm, D_CONF), F32),
         pltpu.VMEM((h0 + tm, D_SCONV), F32), pltpu.VMEM((SUBLANES - 1, tm + PHASE_ROWS, D_CONF), F32),
         pltpu.VMEM((len(POOL_WINDOWS), tm, tm + h0), BF16)],
        (x, g1, wint, maps_bd, scale, dww, dwb, lng, lnb, sw, wout, g2), comm)


def _mix_bwd(dx1, y, x, z, ycat, v1, cv, pooled, g1, wint, maps_bd, scale, dww, lng, lnb, sw, wout, g2, comm=None):
    t_len = x.shape[0]
    tm = min(TM_BWD, t_len)
    nt = t_len // tm
    h0 = HALO

    def body(ins, outs, scr, start, wait):
        (dx1_ref, y_ref, x_ref, z_ref, ycat_ref, v1_ref, cv_ref, pooled_ref, g1_ref, wint_ref, maps_ref, scale_ref,
         dww_ref, lng_ref, lnb_ref, sw_ref, wout_ref, g2_ref) = ins
        dx_ref, dwin_ref, dwout_ref, dg1_ref, dg2_ref, dmaps_ref, dscale_ref, ddww_ref, misc_ref = outs
        ebuf, dvbuf, dcbuf, phases, band, dzbuf, acc_in, acc_out, ddacc = scr
        s = pl.program_id(0)
        ti = nt - 1 - s
        finish = _bracket(start, wait, s == 0, s == nt - 1)

        @pl.when(s == 0)
        def _():
            ebuf[tm:tm + h0, :] = jnp.zeros((h0, D_POOL), F32)
            dvbuf[tm:tm + h0, :] = jnp.zeros((h0, D_CONF), F32)
            dcbuf[tm:tm + h0, :] = jnp.zeros((h0, D_SCONV), F32)
            _fill_bands(band, tm, False)
            acc_in[...] = jnp.zeros_like(acc_in)
            acc_out[...] = jnp.zeros_like(acc_out)
            ddacc[...] = jnp.zeros_like(ddacc)
            dg1_ref[...] = jnp.zeros_like(dg1_ref)
            dg2_ref[...] = jnp.zeros_like(dg2_ref)
            dmaps_ref[...] = jnp.zeros_like(dmaps_ref)
            dscale_ref[...] = jnp.zeros_like(dscale_ref)
            ddww_ref[...] = jnp.zeros_like(ddww_ref)
            misc_ref[...] = jnp.zeros_like(misc_ref)

        @pl.when(s > 0)
        def _():
            ebuf[tm:tm + h0, :] = ebuf[0:h0, :]
            dvbuf[tm:tm + h0, :] = dvbuf[0:h0, :]
            dcbuf[tm:tm + h0, :] = dcbuf[0:h0, :]

        yv = y_ref[...].astype(F32)
        _, yh, yr = _rms(yv, g2_ref[...])
        dy, dg2 = _rms_bwd(yh, yr, g2_ref[...], dx1_ref[...])
        dg2_ref[...] += dg2
        dyb = dy.astype(BF16)
        acc_out[...] += _tn(ycat_ref[...], dyb)
        dycat = _nt(dyb, wout_ref[...])
        dya = dycat[:, 0:D_POOL]
        dyb2 = dycat[:, D_POOL:D_POOL + D_CONF]
        dyc = dycat[:, D_POOL + D_CONF:D]

        pooled_v = pooled_ref[...]
        pm = _nn(pooled_v, maps_ref[...])
        dscale_ref[...] += _colsum(dya * pm)
        dq = (dya * scale_ref[...]).astype(BF16)
        dmaps_ref[...] += _tn(pooled_v, dq)
        dpooled = _nt(dq, maps_ref[...])
        ebuf[0:tm, :] = dpooled / _pool_count(ti * tm, tm)
        dzbuf[:, C_P[0]:C_P[1]] = (_window_sums(band, ebuf[...].astype(BF16), D_POOL) - dpooled).astype(BF16)

        vh, rs = _layer_norm_stats(v1_ref[...])
        v2 = vh * lng_ref[...] + lnb_ref[...]
        s2 = _sigmoid(v2)
        dv2 = dyb2 * (s2 * (1.0 + v2 * (1.0 - s2)))
        misc_ref[1:2, :] += _colsum(dv2 * vh)
        misc_ref[2:3, :] += _colsum(dv2)
        dvh = dv2 * lng_ref[...]
        dv1 = rs * (dvh - jnp.mean(dvh, axis=-1, keepdims=True) - vh * jnp.mean(dvh * vh, axis=-1, keepdims=True))
        misc_ref[0:1, :] += _colsum(dv1)
        dvbuf[0:tm, :] = dv1
        _phase_copies(dvbuf, phases, tm)
        for r0 in range(0, tm, TAP_ROWS):
            blk = slice(r0, r0 + TAP_ROWS)
            za = z_ref[blk, C_A[0]:C_A[1]].astype(F32)
            sg = _sigmoid(z_ref[blk, C_G[0]:C_G[1]].astype(F32))
            v0 = za * sg
            dv0 = None
            for k in range(CONF_K):
                j = CONF_K - 1 - k
                dk = _tap(dvbuf, phases, k, TAP_ROWS, r0)
                prod = v0 * dk
                part = prod[0:SUBLANES, :]
                for q in range(SUBLANES, TAP_ROWS, SUBLANES):
                    part = part + prod[q:q + SUBLANES, :]
                ddacc[SUBLANES * j:SUBLANES * (j + 1), :] += part
                term = dww_ref[j:j + 1, :] * dk
                dv0 = term if dv0 is None else dv0 + term
            dzbuf[blk, C_A[0]:C_A[1]] = (dv0 * sg).astype(BF16)
            dzbuf[blk, C_G[0]:C_G[1]] = (dv0 * za * sg * (1.0 - sg)).astype(BF16)

        zb = z_ref[:, C_B[0]:C_B[1]].astype(F32)
        zc = z_ref[:, C_C[0]:C_C[1]].astype(F32)
        zx = z_ref[:, C_X[0]:C_X[1]].astype(F32)
        pv = zc * zx
        dzbuf[:, C_B[0]:C_B[1]] = (dyc * cv_ref[...]).astype(BF16)
        dcbuf[0:tm, :] = dyc * zb
        dp = None
        for k in range(3):
            j = 2 - k
            dk = dcbuf[k:k + tm, :]
            misc_ref[3 + j:4 + j, :] += _colsum(pv * dk)
            term = sw_ref[j:j + 1, :] * dk
            dp = term if dp is None else dp + term
        dzbuf[:, C_C[0]:C_C[1]] = (dp * zx).astype(BF16)
        dzbuf[:, C_X[0]:C_X[1]] = (dp * zc).astype(BF16)

        xv = x_ref[...]
        h, xh, xr = _rms(xv, g1_ref[...])
        dz = dzbuf[...]
        acc_in[...] += _tn(dz, h.astype(BF16))
        dh = _nn(dz, wint_ref[...])
        dxn, dg1 = _rms_bwd(xh, xr, g1_ref[...], dh)
        dg1_ref[...] += dg1
        dx_ref[...] = dx1_ref[...] + dxn

        @pl.when(s == nt - 1)
        def _():
            dwin_ref[...] = acc_in[...].astype(BF16)
            dwout_ref[...] = acc_out[...].astype(BF16)
            for j in range(CONF_K):
                ddww_ref[j:j + 1, :] = _colsum(ddacc[SUBLANES * j:SUBLANES * (j + 1), :])

        finish()

    return _fused_call(
        "mix_bwd", body, (nt,),
        [_rows(tm, D, nt), _rows(tm, D, nt), _rows(tm, D, nt), _rows(tm, D_IN, nt), _rows(tm, D, nt),
         _rows(tm, D_CONF, nt), _rows(tm, D_SCONV, nt), _rows(tm, D_POOL, nt), _const((1, D)), _const((D_IN, D)),
         _const((D_POOL, D_POOL)), _const((1, D_POOL)), _const((CONF_K, D_CONF)), _const((1, D_CONF)),
         _const((1, D_CONF)), _const((3, D_SCONV)), _const((D, D)), _const((1, D))],
        [_rows(tm, D, nt), _acc((D_IN, D)), _acc((D, D)), _acc((1, D)), _acc((1, D)), _acc((D_POOL, D_POOL)),
         _acc((1, D_POOL)), _acc((32, D_CONF)), _acc((8, D_CONF))],
        [_sds((t_len, D), F32), _sds((D_IN, D), BF16), _sds((D, D), BF16), _sds((1, D), F32), _sds((1, D), F32),
         _sds((D_POOL, D_POOL), F32), _sds((1, D_POOL), F32), _sds((32, D_CONF), F32), _sds((8, D_CONF), F32)],
        [pltpu.VMEM((tm + h0, D_POOL), F32), pltpu.VMEM((tm + h0, D_CONF), F32),
         pltpu.VMEM((tm + h0, D_SCONV), F32), pltpu.VMEM((SUBLANES - 1, tm + PHASE_ROWS, D_CONF), F32),
         pltpu.VMEM((len(POOL_WINDOWS), tm, tm + h0), BF16), pltpu.VMEM((tm, D_IN), BF16),
         pltpu.VMEM((D_IN, D), F32), pltpu.VMEM((D, D), F32), pltpu.VMEM((SUBLANES * CONF_K, D_CONF), F32)],
        (dx1, y, x, z, ycat, v1, cv, pooled, g1, wint, maps_bd, scale, dww, lng, lnb, sw, wout, g2), comm)


def _kv_fwd(mem, gmem, wk, wv):
    def body(mem_ref, g_ref, wk_ref, wv_ref, k_ref, v_ref):
        mn, _, _ = _rms(mem_ref[...], g_ref[...])
        mnb = mn.astype(BF16)
        k_ref[...] = _nn(mnb, wk_ref[...]).astype(BF16)
        v_ref[...] = _nn(mnb, wv_ref[...]).astype(BF16)

    n = mem.shape[0]
    return pl.pallas_call(
        body, name="kv_fwd", out_shape=[_sds((n, D), BF16), _sds((n, D), BF16)],
        compiler_params=pltpu.CompilerParams(vmem_limit_bytes=VMEM_LIMIT),
    )(mem, gmem, wk, wv)


def _kv_bwd(mem, gmem, dk, dv, wk, wv):
    def body(mem_ref, g_ref, dk_ref, dv_ref, wk_ref, wv_ref, dwk_ref, dwv_ref, dg_ref):
        mn, mh, _ = _rms(mem_ref[...], g_ref[...])
        mnb = mn.astype(BF16)
        dkb = dk_ref[...].astype(BF16)
        dvb = dv_ref[...].astype(BF16)
        dwk_ref[...] = _tn(mnb, dkb).astype(BF16)
        dwv_ref[...] = _tn(mnb, dvb).astype(BF16)
        dmn = _nt(dkb, wk_ref[...]) + _nt(dvb, wv_ref[...])
        dg_ref[...] = _colsum(dmn * mh)

    return pl.pallas_call(
        body, name="kv_bwd", out_shape=[_sds((D, D), BF16), _sds((D, D), BF16), _sds((1, D), F32)],
        compiler_params=pltpu.CompilerParams(vmem_limit_bytes=VMEM_LIMIT),
    )(mem, gmem, dk, dv, wk, wv)


def _softmax_rows(s):
    e = jnp.exp(s - jnp.max(s, axis=-1, keepdims=True))
    return e / jnp.sum(e, axis=-1, keepdims=True)


def _xattn_fwd(x1, g3, wq, k, v, wo, g4, comm=None):
    t_len = x1.shape[0]
    tm = min(2 * TM_FWD, t_len)
    nt = t_len // tm
    n_mem = k.shape[0]
    sc = HEAD_DIM ** -0.5

    def body(ins, outs, scr, start, wait):
        x_ref, g3_ref, wq_ref, k_ref, v_ref, wo_ref, g4_ref = ins
        q_ref, p_ref, o_ref, y_ref, x2_ref = outs
        i = pl.program_id(0)
        finish = _bracket(start, wait, i == 0, i == nt - 1)
        xv = x_ref[...]
        h, _, _ = _rms(xv, g3_ref[...])
        qb = _nn(h.astype(BF16), wq_ref[...]).astype(BF16)
        q_ref[...] = qb
        for hd in range(HEADS):
            sl = slice(hd * HEAD_DIM, (hd + 1) * HEAD_DIM)
            pb = _softmax_rows(_nt(qb[:, sl], k_ref[:, sl]) * sc).astype(BF16)
            p_ref[:, hd * n_mem:(hd + 1) * n_mem] = pb
            o_ref[:, sl] = _nn(pb, v_ref[:, sl]).astype(BF16)
        yb = _nn(o_ref[...], wo_ref[...]).astype(BF16)
        y_ref[...] = yb
        yn, _, _ = _rms(yb.astype(F32), g4_ref[...])
        x2_ref[...] = xv + yn
        finish()

    return _fused_call(
        "xattn_fwd", body, (nt,),
        [_rows(tm, D), _const((1, D)), _const((D, D)), _const((n_mem, D)), _const((n_mem, D)), _const((D, D)),
         _const((1, D))],
        [_rows(tm, D), _rows(tm, HEADS * n_mem), _rows(tm, D), _rows(tm, D), _rows(tm, D)],
        [_sds((t_len, D), BF16), _sds((t_len, HEADS * n_mem), BF16), _sds((t_len, D), BF16), _sds((t_len, D), BF16),
         _sds((t_len, D), F32)],
        [], (x1, g3, wq, k, v, wo, g4), comm)


def _xattn_bwd(dx2, y, x1, q, p, o, g3, wq, k, v, wo, g4):
    t_len = x1.shape[0]
    tm = min(TM_FWD, t_len)
    nt = t_len // tm
    n_mem = k.shape[0]
    sc = HEAD_DIM ** -0.5

    def body(ins, outs, scr, start, wait):
        dx2_ref, y_ref, x_ref, q_ref, p_ref, o_ref, g3_ref, wq_ref, k_ref, v_ref, wo_ref, g4_ref = ins
        dx_ref, dwq_ref, dwo_ref, dk_ref, dv_ref, dg3_ref, dg4_ref = outs
        dqbuf, acc_q, acc_o = scr
        s = pl.program_id(0)

        @pl.when(s == 0)
        def _():
            acc_q[...] = jnp.zeros_like(acc_q)
            acc_o[...] = jnp.zeros_like(acc_o)
            dk_ref[...] = jnp.zeros_like(dk_ref)
            dv_ref[...] = jnp.zeros_like(dv_ref)
            dg3_ref[...] = jnp.zeros_like(dg3_ref)
            dg4_ref[...] = jnp.zeros_like(dg4_ref)

        yv = y_ref[...].astype(F32)
        _, yh, yr = _rms(yv, g4_ref[...])
        dy, dg4 = _rms_bwd(yh, yr, g4_ref[...], dx2_ref[...])
        dg4_ref[...] += dg4
        dyb = dy.astype(BF16)
        acc_o[...] += _tn(o_ref[...], dyb)
        do = _nt(dyb, wo_ref[...])
        qb = q_ref[...]
        for hd in range(HEADS):
            sl = slice(hd * HEAD_DIM, (hd + 1) * HEAD_DIM)
            pb = p_ref[:, hd * n_mem:(hd + 1) * n_mem]
            p = pb.astype(F32)
            dob = do[:, sl].astype(BF16)
            dp = _nt(dob, v_ref[:, sl])
            dv_ref[:, sl] += _tn(pb, dob)
            ds = (p * (dp - jnp.sum(dp * p, axis=-1, keepdims=True)) * sc).astype(BF16)
            dqbuf[:, sl] = _nn(ds, k_ref[:, sl]).astype(BF16)
            dk_ref[:, sl] += _tn(ds, qb[:, sl])
        xv = x_ref[...]
        h, xh, xr = _rms(xv, g3_ref[...])
        dq = dqbuf[...]
        acc_q[...] += _tn(h.astype(BF16), dq)
        dh = _nt(dq, wq_ref[...])
        dxn, dg3 = _rms_bwd(xh, xr, g3_ref[...], dh)
        dg3_ref[...] += dg3
        dx_ref[...] = dx2_ref[...] + dxn

        @pl.when(s == nt - 1)
        def _():
            dwq_ref[...] = acc_q[...].astype(BF16)
            dwo_ref[...] = acc_o[...].astype(BF16)

    outs, _ = _fused_call(
        "xattn_bwd", body, (nt,),
        [_rows(tm, D), _rows(tm, D), _rows(tm, D), _rows(tm, D), _rows(tm, HEADS * n_mem), _rows(tm, D), _const((1, D)),
         _const((D, D)), _const((n_mem, D)), _const((n_mem, D)), _const((D, D)), _const((1, D))],
        [_rows(tm, D), _acc((D, D)), _acc((D, D)), _acc((n_mem, D)), _acc((n_mem, D)), _acc((1, D)), _acc((1, D))],
        [_sds((t_len, D), F32), _sds((D, D), BF16), _sds((D, D), BF16), _sds((n_mem, D), F32),
         _sds((n_mem, D), F32), _sds((1, D), F32), _sds((1, D), F32)],
        [pltpu.VMEM((tm, D), BF16), pltpu.VMEM((D, D), F32), pltpu.VMEM((D, D), F32)],
        (dx2, y, x1, q, p, o, g3, wq, k, v, wo, g4))
    return outs


def _ffn_cols(half, part):
    c0 = part * D_FF + half * FF_CHUNK
    return c0, c0 + FF_CHUNK


def _fill_shifts(ref, tm, step):
    t = lax.broadcasted_iota(jnp.int32, (tm, tm), 0)
    s = lax.broadcasted_iota(jnp.int32, (tm, tm), 1)
    ref[0:tm, :] = jnp.where(s == t + step, 1.0, 0.0).astype(BF16)
    ref[tm:2 * tm, :] = jnp.where(s == t + 2 * step, 1.0, 0.0).astype(BF16)


def _edge_terms(near, far, edge, inner):
    r = lax.broadcasted_iota(jnp.int32, (16, near.shape[1]), 0)
    return jnp.where(r == edge, near, 0.0), jnp.where(r == edge, far, jnp.where(r == inner, near, 0.0))


def _ffn_fwd(x2, g5, wupt, wc, wdown, g6, comm=None, target=None):
    t_len = x2.shape[0]
    tm = min(TM_BWD, t_len)
    nt = t_len // tm

    def body(ins, outs, scr, start, wait):
        x_ref, g5_ref, wupt_ref, wc_ref, wdown_ref, g6_ref = ins[:6]
        u_ref, c_ref, a_ref, y_ref, x3_ref = outs[:5]
        carry, shift, tbuf = scr
        i = pl.program_id(0)
        finish = _bracket(start, wait, i == 0, i == nt - 1)

        @pl.when(i == 0)
        def _():
            carry[...] = jnp.zeros_like(carry)
            _fill_shifts(shift, tm, -1)
            if target is not None:
                outs[5][...] = jnp.zeros_like(outs[5])

        xv = x_ref[...]
        h, _, _ = _rms(xv, g5_ref[...])
        hb = h.astype(BF16)
        for half in range(2):
            conv = []
            for part in range(2):
                c0, c1 = _ffn_cols(half, part)
                w0, w1, w2 = wc_ref[0:1, c0:c1], wc_ref[1:2, c0:c1], wc_ref[2:3, c0:c1]
                u = _nt(hb, wupt_ref[c0:c1, :])
                ub = u.astype(BF16)
                u_ref[:, c0:c1] = ub
                if part == 0:
                    sh = _nn(shift[...], ub)
                    cb = w0 * sh[tm:2 * tm, :] + w1 * sh[0:tm, :] + w2 * u
                    c_ref[:, c0:c1] = cb.astype(BF16)
                    m1, m2 = _edge_terms(carry[SUBLANES - 1:SUBLANES, c0:c1], carry[SUBLANES - 2:SUBLANES - 1, c0:c1],
                                         0, 1)
                    c_ref[0:16, c0:c1] = (cb[0:16, :] + w1 * m1 + w0 * m2).astype(BF16)
                else:
                    tbuf[SUBLANES:2 * SUBLANES, :] = carry[:, c0:c1]
                    tbuf[2 * SUBLANES:2 * SUBLANES + tm, :] = ub.astype(F32)
                    cb = (w0 * tbuf[2 * SUBLANES - 2:2 * SUBLANES - 2 + tm, :]
                          + w1 * tbuf[2 * SUBLANES - 1:2 * SUBLANES - 1 + tm, :] + w2 * u)
                    c_ref[:, c0:c1] = cb.astype(BF16)
                carry[:, c0:c1] = u[tm - SUBLANES:tm, :].astype(BF16).astype(F32)
                conv.append(c_ref[:, c0:c1].astype(F32))
            a = (conv[0] * _sigmoid(conv[0]) * conv[1]).astype(BF16)
            a_ref[:, half * FF_CHUNK:(half + 1) * FF_CHUNK] = a
        yb = _nn(a_ref[...], wdown_ref[...]).astype(BF16)
        y_ref[...] = yb
        yn, _, _ = _rms(yb.astype(F32), g6_ref[...])
        if target is None:
            x3_ref[...] = xv + yn
        else:
            err = xv + yn - ins[6][...]
            x3_ref[...] = err * (1.0 / D)
            part = 0.5 * _colsum(jnp.mean(err * err, axis=-1, keepdims=True))
            outs[5][...] += jnp.broadcast_to(part, outs[5].shape)
        finish()

    last = target is not None
    return _fused_call(
        "ffn_fwd_loss" if last else "ffn_fwd", body, (nt,),
        [_rows(tm, D), _const((1, D)), _const((2 * D_FF, D)), _const((3, 2 * D_FF)), _const((D_FF, D)),
         _const((1, D))] + ([_rows(tm, D)] if last else []),
        [_rows(tm, 2 * D_FF), _rows(tm, 2 * D_FF), _rows(tm, D_FF), _rows(tm, D), _rows(tm, D)]
        + ([_acc((8, 128))] if last else []),
        [_sds((t_len, 2 * D_FF), BF16), _sds((t_len, 2 * D_FF), BF16), _sds((t_len, D_FF), BF16),
         _sds((t_len, D), BF16), _sds((t_len, D), F32)] + ([_sds((8, 128), F32)] if last else []),
        [pltpu.VMEM((SUBLANES, 2 * D_FF), F32), pltpu.VMEM((2 * tm, tm), BF16),
         pltpu.VMEM((2 * SUBLANES + tm, FF_CHUNK), F32)],
        (x2, g5, wupt, wc, wdown, g6) + ((target,) if last else ()), comm)


def _ffn_bwd(dx3, y, x2, u, c, g5, wupt, wc, wdown, g6, comm=None):
    t_len = x2.shape[0]
    tm = min(TM_BWD, t_len)
    nt = t_len // tm

    def body(ins, outs, scr, start, wait):
        dx3_ref, y_ref, x_ref, u_ref, c_ref, g5_ref, wupt_ref, wc_ref, wdown_ref, g6_ref = ins
        dx_ref, du_ref, dyo_ref, h_ref, dg5_ref, dg6_ref, dwc_ref = outs
        carry, shift = scr
        s = pl.program_id(0)
        finish = _bracket(start, wait, s == 0, s == nt - 1)

        @pl.when(s == 0)
        def _():
            carry[...] = jnp.zeros_like(carry)
            _fill_shifts(shift, tm, 1)
            dg5_ref[...] = jnp.zeros_like(dg5_ref)
            dg6_ref[...] = jnp.zeros_like(dg6_ref)
            dwc_ref[...] = jnp.zeros_like(dwc_ref)

        yv = y_ref[...].astype(F32)
        _, yh, yr = _rms(yv, g6_ref[...])
        dy, dg6 = _rms_bwd(yh, yr, g6_ref[...], dx3_ref[...])
        dg6_ref[...] += dg6
        dyb = dy.astype(BF16)
        dyo_ref[...] = dyb
        xv = x_ref[...]
        h, xh, xr = _rms(xv, g5_ref[...])
        h_ref[...] = h.astype(BF16)

        for half in range(2):
            g0, g1 = _ffn_cols(half, 0)
            v0, v1 = _ffn_cols(half, 1)
            gt = c_ref[:, g0:g1].astype(F32)
            vl = c_ref[:, v0:v1].astype(F32)
            sg = _sigmoid(gt)
            sil = gt * sg
            da = _nt(dyb, wdown_ref[half * FF_CHUNK:(half + 1) * FF_CHUNK, :])
            dcs = (da * vl * (sg * (1.0 + gt * (1.0 - sg))), da * sil)
            for part in range(2):
                c0, c1 = _ffn_cols(half, part)
                w0, w1, w2 = wc_ref[0:1, c0:c1], wc_ref[1:2, c0:c1], wc_ref[2:3, c0:c1]
                dc = dcs[part]
                sh = _nn(shift[...], dc.astype(BF16))
                d1, d2 = sh[0:tm, :], sh[tm:2 * tm, :]
                m1, m2 = _edge_terms(carry[0:1, c0:c1], carry[1:2, c0:c1], 15, 14)
                carry[:, c0:c1] = dc[0:SUBLANES, :]
                uu = u_ref[:, c0:c1].astype(F32)
                ut = u_ref[tm - 16:tm, c0:c1].astype(F32)
                dwc_ref[2:3, c0:c1] += _colsum(uu * dc)
                dwc_ref[1:2, c0:c1] += _colsum(uu * d1) + _colsum(ut * m1)
                dwc_ref[0:1, c0:c1] += _colsum(uu * d2) + _colsum(ut * m2)
                du = w2 * dc + w1 * d1 + w0 * d2
                du_ref[:, c0:c1] = du.astype(BF16)
                du_ref[tm - 16:tm, c0:c1] = (du[tm - 16:tm, :] + w1 * m1 + w0 * m2).astype(BF16)
        dh = _nn(du_ref[...], wupt_ref[...])
        dxn, dg5 = _rms_bwd(xh, xr, g5_ref[...], dh)
        dg5_ref[...] += dg5
        dx_ref[...] = dx3_ref[...] + dxn
        finish()

    return _fused_call(
        "ffn_bwd", body, (nt,),
        [_rows(tm, D, nt), _rows(tm, D, nt), _rows(tm, D, nt), _rows(tm, 2 * D_FF, nt), _rows(tm, 2 * D_FF, nt),
         _const((1, D)), _const((2 * D_FF, D)), _const((3, 2 * D_FF)), _const((D_FF, D)), _const((1, D))],
        [_rows(tm, D, nt), _rows(tm, 2 * D_FF, nt), _rows(tm, D, nt), _rows(tm, D, nt), _acc((1, D)), _acc((1, D)),
         _acc((8, 2 * D_FF))],
        [_sds((t_len, D), F32), _sds((t_len, 2 * D_FF), BF16), _sds((t_len, D), BF16), _sds((t_len, D), BF16),
         _sds((1, D), F32), _sds((1, D), F32), _sds((8, 2 * D_FF), F32)],
        [pltpu.VMEM((SUBLANES, 2 * D_FF), F32), pltpu.VMEM((2 * tm, tm), BF16)],
        (dx3, y, x2, u, c, g5, wupt, wc, wdown, g6), comm)


def _tn_matmul(a, b):
    t_len, m = a.shape
    bm = FF_CHUNK
    bt = min(2 * TM_FWD, t_len)
    nt = t_len // bt

    def body(a_ref, b_ref, o_ref, acc):
        t = pl.program_id(1)

        @pl.when(t == 0)
        def _():
            acc[...] = jnp.zeros_like(acc)

        acc[...] += _tn(a_ref[...], b_ref[...])

        @pl.when(t == nt - 1)
        def _():
            o_ref[...] = acc[...].astype(BF16)

    return pl.pallas_call(
        body, grid=(m // bm, nt), name="tn_matmul",
        in_specs=[pl.BlockSpec((bt, bm), lambda i, t: (t, i)), pl.BlockSpec((bt, D), lambda i, t: (t, 0))],
        out_specs=pl.BlockSpec((bm, D), lambda i, t: (i, 0)),
        out_shape=_sds((m, D), BF16),
        scratch_shapes=[pltpu.VMEM((bm, D), F32)],
        compiler_params=pltpu.CompilerParams(dimension_semantics=("parallel", "arbitrary"),
                                             vmem_limit_bytes=VMEM_LIMIT),
    )(a, b)


BLOCK_BYTES = 1 << 20


def _row_block(rows, cols):
    limit = max(16, BLOCK_BYTES // (4 * cols))
    best = None
    for rb in range(16, min(rows, limit) + 1, 16):
        if rows % rb == 0:
            best = rb
    return best or rows


def _elementwise(name, fn, ins, out_dtypes, comm=None):
    rows, cols = ins[0].shape
    rb = _row_block(rows, cols)
    nt = rows // rb

    def body(in_refs, out_refs, scr, start, wait):
        i = pl.program_id(0)
        finish = _bracket(start, wait, i == 0, i == nt - 1)
        res = fn(*[r[...] for r in in_refs])
        for o_ref, r in zip(out_refs, res):
            o_ref[...] = r
        finish()

    spec = pl.BlockSpec((rb, cols), lambda i: (i, 0))
    outs, extra = _fused_call(
        name, body, (nt,), [spec] * len(ins), [spec] * len(out_dtypes),
        [_sds((rows, cols), dt) for dt in out_dtypes], [], tuple(ins), comm,
        sem=("arbitrary",) if comm else ("parallel",))
    return list(outs) + list(extra)


def _cast_bf16(w):
    return _elementwise("cast_bf16", lambda v: (v.astype(BF16),), [w], [BF16])[0]


def _adam_math(w, g, m, v):
    nm = ADAM_B1 * m + (1.0 - ADAM_B1) * g
    nv = ADAM_B2 * v + (1.0 - ADAM_B2) * (g * g)
    m_hat = nm / (1.0 - ADAM_B1 ** ADAM_STEP)
    v_hat = nv / (1.0 - ADAM_B2 ** ADAM_STEP)
    return -ADAM_LR * (m_hat / (jnp.sqrt(v_hat) + ADAM_EPS) + ADAM_WD * w), nm, nv


def _adamw(w, g, m, v, comm=None):
    return _elementwise("adamw", _adam_math, [w, g, m, v], [F32, F32, F32], comm)


def _sum_blocks(parts, slot):
    n, rows, cols = parts.shape
    rb = _row_block(rows, cols)

    def body(slot_ref, p_ref, o_ref):
        acc = p_ref[0].astype(F32)
        for j in range(1, n):
            acc = acc + p_ref[j].astype(F32)
        o_ref[...] = acc

    return pl.pallas_call(
        body, name="sum_blocks", out_shape=_sds((2, rows, cols), F32),
        grid_spec=pltpu.PrefetchScalarGridSpec(
            num_scalar_prefetch=1, grid=(rows // rb,),
            in_specs=[pl.BlockSpec((n, rb, cols), lambda i, s: (0, i, 0))],
            out_specs=pl.BlockSpec((None, rb, cols), lambda i, s: (s[0], i, 0))),
        compiler_params=pltpu.CompilerParams(dimension_semantics=("parallel",), vmem_limit_bytes=VMEM_LIMIT),
    )(slot.reshape(1).astype(jnp.int32), parts)


def _sum_devices(parts):
    n, rows, cols = parts.shape

    def body(p_ref, o_ref):
        acc = p_ref[0]
        for j in range(1, n):
            acc = acc + p_ref[j]
        o_ref[...] = acc

    return pl.pallas_call(
        body, name="sum_devices", out_shape=_sds((rows, cols), F32),
        compiler_params=pltpu.CompilerParams(vmem_limit_bytes=VMEM_LIMIT),
    )(parts)


def _pack(parts):
    flat = jnp.concatenate([p.reshape(-1) for p in parts])
    rows = -(-flat.shape[0] // 1024) * 8
    return jnp.pad(flat, (0, rows * 128 - flat.shape[0])).reshape(rows, 128)


def _unpack(packed, shapes):
    flat = packed.reshape(-1)
    out, off = [], 0
    for shp in shapes:
        size = 1
        for d in shp:
            size *= d
        out.append(flat[off:off + size].reshape(shp))
        off += size
    return out


_BIG = {'w_in': ('wint', True), 'w_out': ('wout', False), 'xattn_wq': ('wq', False), 'xattn_wk': ('wk', False),
        'xattn_wv': ('wv', False), 'xattn_wo': ('wo', False), 'ffn_w_up': ('wupt', True),
        'ffn_w_down': ('wdown', False)}
_KEYS = [key for key, _ in _BIG.values()]
_GATHER_EARLY = ("wint", "wout", "wq", "wk", "wv", "wo")
_GATHER_WITH = {"mix": ("wupt",), "xattn": ("wdown",), "ffn": _GATHER_EARLY}
_SCATTER_LATE = ("wupt", "wdown", "wq", "wk", "wv", "wo")
_SCATTER_NEXT = ("wint", "wout")


def _block_diag(maps):
    out = jnp.zeros((D_POOL, D_POOL), maps.dtype)
    for g in range(len(POOL_WINDOWS)):
        out = lax.dynamic_update_slice(out, maps[g], (g * POOL_GROUP, g * POOL_GROUP))
    return out


def _local_step(x, mem, target, small, big, shards=None):
    distributed = shards is not None
    depth = len(shards) if distributed else len(big)
    big = list(big)
    row = lambda a: a.reshape(1, -1)
    gmem = row(small["mem_norm"])
    saved = []
    for l in range(depth):
        w = big[l]
        nxt = {}

        def plan(stage):
            layer = l + 1 if stage == "ffn" else l
            if distributed and layer < depth:
                return _Gather([shards[layer][key] for key in _GATHER_WITH[stage]])
            return None

        def landed(stage, outs):
            into = nxt if stage == "ffn" else w
            for key, g in zip(_GATHER_WITH[stage], outs):
                into[key] = g.reshape(-1, D)

        sp = dict(
            g1=row(small["mix_pre_norm"][l]), g2=row(small["mix_post_norm"][l]),
            maps=_block_diag(small["pool_maps"][l]).astype(BF16), scale=row(small["pool_scale"][l]),
            dww=small["conf_dw_w"][l], dwb=row(small["conf_dw_b"][l]), lng=row(small["conf_ln_g"][l]),
            lnb=row(small["conf_ln_b"][l]), sw=small["sconv_w"][l],
            g3=row(small["xattn_pre_norm"][l]), g4=row(small["xattn_post_norm"][l]),
            g5=row(small["ffn_pre_norm"][l]), g6=row(small["ffn_post_norm"][l]), wc=small["ffn_conv_w"][l])
        (z, ycat, y1, x1, v1, cv, pooled), got = _mix_fwd(
            x, sp["g1"], w["wint"], sp["maps"], sp["scale"], sp["dww"], sp["dwb"], sp["lng"], sp["lnb"], sp["sw"],
            w["wout"], sp["g2"], plan("mix"))
        landed("mix", got)
        k, v = _kv_fwd(mem, gmem, w["wk"], w["wv"])
        (q, p, o, y2, x2), got = _xattn_fwd(x1, sp["g3"], w["wq"], k, v, w["wo"], sp["g4"], plan("xattn"))
        landed("xattn", got)
        ffn_out, got = _ffn_fwd(x2, sp["g5"], w["wupt"], sp["wc"], w["wdown"], sp["g6"], plan("ffn"),
                                target if l == depth - 1 else None)
        u, c, a, y3, x3 = ffn_out[:5]
        landed("ffn", got)
        if nxt:
            big.append(nxt)
        saved.append(dict(sp=sp, x=x, z=z, ycat=ycat, y1=y1, x1=x1, v1=v1, cv=cv, pooled=pooled, k=k, v=v, q=q, p=p, o=o,
                          y2=y2, x2=x2, u=u, c=c, a=a, y3=y3))
        x = x3

    dx, loss_blk = x, ffn_out[5]
    big_grads = [None] * depth
    sg = {n: [None] * depth for n in ("mix_pre_norm", "mix_post_norm", "pool_maps", "pool_scale", "conf_dw_w",
                                      "conf_dw_b", "conf_ln_g", "conf_ln_b", "sconv_w", "xattn_pre_norm",
                                      "xattn_post_norm", "ffn_pre_norm", "ffn_post_norm", "ffn_conv_w")}
    dgmem = None
    pending = None
    for l in reversed(range(depth)):
        w, s = big[l], saved[l]
        sp = s["sp"]
        comm = _Scatter(pending) if distributed and pending is not None else None
        (dx, du, dy3, h3, dg5, dg6, dwc), got = _ffn_bwd(dx, s["y3"], s["x2"], s["u"], s["c"], sp["g5"], w["wupt"],
                                                        sp["wc"], w["wdown"], sp["g6"], comm)
        if comm is not None:
            big_grads[l + 1] = (big_grads[l + 1], got[0])
        g = dict(wupt=_tn_matmul(du, h3), wdown=_tn_matmul(s["a"], dy3))
        dx, g["wq"], g["wo"], dk, dv, dg3, dg4 = _xattn_bwd(dx, s["y2"], s["x1"], s["q"], s["p"], s["o"], sp["g3"], w["wq"],
                                                            s["k"], s["v"], w["wo"], sp["g4"])
        g["wk"], g["wv"], dgm = _kv_bwd(mem, gmem, dk, dv, w["wk"], w["wv"])
        dgmem = dgm if dgmem is None else dgmem + dgm
        comm = _Scatter([g[key] for key in _SCATTER_LATE]) if distributed else None
        (dx, g["wint"], g["wout"], dg1, dg2, dmaps, dscale, ddww, misc), got = _mix_bwd(
            dx, s["y1"], s["x"], s["z"], s["ycat"], s["v1"], s["cv"], s["pooled"], sp["g1"], w["wint"], sp["maps"],
            sp["scale"], sp["dww"], sp["lng"], sp["lnb"], sp["sw"], w["wout"], sp["g2"], comm)
        if distributed:
            big_grads[l] = got[0]
            pending = [g[key] for key in _SCATTER_NEXT]
        else:
            big_grads[l] = g
        sg["mix_pre_norm"][l] = dg1[0]
        sg["mix_post_norm"][l] = dg2[0]
        sg["pool_maps"][l] = jnp.stack([dmaps[i * 64:(i + 1) * 64, i * 64:(i + 1) * 64] for i in range(4)])
        sg["pool_scale"][l] = dscale[0]
        sg["conf_dw_w"][l] = ddww[0:CONF_K]
        sg["conf_dw_b"][l] = misc[0]
        sg["conf_ln_g"][l] = misc[1]
        sg["conf_ln_b"][l] = misc[2]
        sg["sconv_w"][l] = misc[3:6]
        sg["xattn_pre_norm"][l] = dg3[0]
        sg["xattn_post_norm"][l] = dg4[0]
        sg["ffn_pre_norm"][l] = dg5[0]
        sg["ffn_post_norm"][l] = dg6[0]
        sg["ffn_conv_w"][l] = dwc[0:3]
    if distributed:
        big_grads[0] = (big_grads[0], pending)
    small_grads = {n: jnp.stack(vs) for n, vs in sg.items()}
    small_grads["mem_norm"] = dgmem[0]
    return loss_blk, dx, big_grads, small_grads


_WEIGHTS = ['mem_norm', 'mix_pre_norm', 'mix_post_norm', 'w_in', 'pool_maps', 'pool_scale', 'conf_dw_w', 'conf_dw_b',
            'conf_ln_g', 'conf_ln_b', 'sconv_w', 'w_out', 'xattn_pre_norm', 'xattn_post_norm', 'xattn_wq',
            'xattn_wk', 'xattn_wv', 'xattn_wo', 'ffn_pre_norm', 'ffn_post_norm', 'ffn_w_up', 'ffn_conv_w',
            'ffn_w_down']
_CHANNEL_SHARDED = ('conf_dw_w', 'sconv_w', 'ffn_conv_w')
_SMALL = [n for n in _WEIGHTS if n not in _BIG]


def kernel(x, mem, mem_norm, mix_pre_norm, mix_post_norm, w_in, pool_maps, pool_scale, conf_dw_w, conf_dw_b, conf_ln_g, conf_ln_b, sconv_w, w_out, xattn_pre_norm, xattn_post_norm, xattn_wq, xattn_wk, xattn_wv, xattn_wo, ffn_pre_norm, ffn_post_norm, ffn_w_up, ffn_conv_w, ffn_w_down, loss_target, m_mem_norm, m_mix_pre_norm, m_mix_post_norm, m_w_in, m_pool_maps, m_pool_scale, m_conf_dw_w, m_conf_dw_b, m_conf_ln_g, m_conf_ln_b, m_sconv_w, m_w_out, m_xattn_pre_norm, m_xattn_post_norm, m_xattn_wq, m_xattn_wk, m_xattn_wv, m_xattn_wo, m_ffn_pre_norm, m_ffn_post_norm, m_ffn_w_up, m_ffn_conv_w, m_ffn_w_down, v_mem_norm, v_mix_pre_norm, v_mix_post_norm, v_w_in, v_pool_maps, v_pool_scale, v_conf_dw_w, v_conf_dw_b, v_conf_ln_g, v_conf_ln_b, v_sconv_w, v_w_out, v_xattn_pre_norm, v_xattn_post_norm, v_xattn_wq, v_xattn_wk, v_xattn_wv, v_xattn_wo, v_ffn_pre_norm, v_ffn_post_norm, v_ffn_w_up, v_ffn_conv_w, v_ffn_w_down):
    wts = dict(mem_norm=mem_norm, mix_pre_norm=mix_pre_norm, mix_post_norm=mix_post_norm, w_in=w_in,
               pool_maps=pool_maps, pool_scale=pool_scale, conf_dw_w=conf_dw_w, conf_dw_b=conf_dw_b,
               conf_ln_g=conf_ln_g, conf_ln_b=conf_ln_b, sconv_w=sconv_w, w_out=w_out,
               xattn_pre_norm=xattn_pre_norm, xattn_post_norm=xattn_post_norm, xattn_wq=xattn_wq,
               xattn_wk=xattn_wk, xattn_wv=xattn_wv, xattn_wo=xattn_wo, ffn_pre_norm=ffn_pre_norm,
               ffn_post_norm=ffn_post_norm, ffn_w_up=ffn_w_up, ffn_conv_w=ffn_conv_w, ffn_w_down=ffn_w_down)
    mom_m = dict(mem_norm=m_mem_norm, mix_pre_norm=m_mix_pre_norm, mix_post_norm=m_mix_post_norm, w_in=m_w_in,
                 pool_maps=m_pool_maps, pool_scale=m_pool_scale, conf_dw_w=m_conf_dw_w, conf_dw_b=m_conf_dw_b,
                 conf_ln_g=m_conf_ln_g, conf_ln_b=m_conf_ln_b, sconv_w=m_sconv_w, w_out=m_w_out,
                 xattn_pre_norm=m_xattn_pre_norm, xattn_post_norm=m_xattn_post_norm, xattn_wq=m_xattn_wq,
                 xattn_wk=m_xattn_wk, xattn_wv=m_xattn_wv, xattn_wo=m_xattn_wo, ffn_pre_norm=m_ffn_pre_norm,
                 ffn_post_norm=m_ffn_post_norm, ffn_w_up=m_ffn_w_up, ffn_conv_w=m_ffn_conv_w,
                 ffn_w_down=m_ffn_w_down)
    mom_v = dict(mem_norm=v_mem_norm, mix_pre_norm=v_mix_pre_norm, mix_post_norm=v_mix_post_norm, w_in=v_w_in,
                 pool_maps=v_pool_maps, pool_scale=v_pool_scale, conf_dw_w=v_conf_dw_w, conf_dw_b=v_conf_dw_b,
                 conf_ln_g=v_conf_ln_g, conf_ln_b=v_conf_ln_b, sconv_w=v_sconv_w, w_out=v_w_out,
                 xattn_pre_norm=v_xattn_pre_norm, xattn_post_norm=v_xattn_post_norm, xattn_wq=v_xattn_wq,
                 xattn_wk=v_xattn_wk, xattn_wv=v_xattn_wv, xattn_wo=v_xattn_wo, ffn_pre_norm=v_ffn_pre_norm,
                 ffn_post_norm=v_ffn_post_norm, ffn_w_up=v_ffn_w_up, ffn_conv_w=v_ffn_conv_w,
                 ffn_w_down=v_ffn_w_down)
    depth = w_in.shape[0]
    chip = 2 * lax.axis_index("x") + lax.axis_index("y")

    stacked = {}
    for name, (key, transposed) in _BIG.items():
        w = wts[name]
        wb = _cast_bf16(w.reshape(-1, w.shape[-1])).reshape(w.shape)
        stacked[key] = wb.transpose(0, 2, 1) if transposed else wb
    shards = [{key: stacked[key][l] for key in _KEYS} for l in range(depth)]
    first = _gather_two_level([shards[0][key] for key in _GATHER_EARLY])
    big0 = {key: g.reshape(-1, D) for key, g in zip(_GATHER_EARLY, first)}

    conv_shapes = [wts[n].shape for n in _CHANNEL_SHARDED]
    conv_all = _allgather_devices(_pack([wts[n] for n in _CHANNEL_SHARDED]))
    per_chip = [_unpack(conv_all[2 * j], conv_shapes) for j in range(N_CHIPS)]
    small = {n: wts[n] for n in _SMALL}
    for i, n in enumerate(_CHANNEL_SHARDED):
        small[n] = jnp.concatenate([per_chip[j][i] for j in range(N_CHIPS)], axis=-1)

    loss_blk, dx, landings, small_grads = _local_step(x[0], mem[0], loss_target[0], small, [big0], shards)

    core = lax.axis_index("c")
    grads, delta, new_m, new_v = {}, {}, {}, {}

    def finish_grads(bufs, order):
        paired = _pair_halves([_sum_blocks(buf, core) for buf in bufs])
        off = 0
        for key in order:
            name = next(n for n, (k, _) in _BIG.items() if k == key)
            h = stacked[key].shape[1] // 2
            g = jnp.stack([p[:, off:off + h].reshape(2 * h, D) for p in paired])
            grads[name] = g.transpose(0, 2, 1) if _BIG[name][1] else g
            off += h

    def adam_step(name, comm=None):
        shp = wts[name].shape
        flat = lambda a: a.reshape(-1, shp[-1])
        res = _adamw(flat(wts[name]), flat(grads[name]), flat(mom_m[name]), flat(mom_v[name]), comm)
        delta[name], new_m[name], new_v[name] = [a.reshape(shp) for a in res[:3]]
        return res[3:]

    finish_grads([pair[0] for pair in landings], _SCATTER_LATE)
    first_landing = adam_step("ffn_w_up", _Scatter(landings[0][1]))[0]
    finish_grads([first_landing] + [pair[1] for pair in landings[1:]], _SCATTER_NEXT)
    for name in _BIG:
        if name != "ffn_w_up":
            adam_step(name)

    small_shapes = [(128,)] + [small[n].shape for n in _SMALL]
    partial = _pack([loss_blk[0]] + [small_grads[n] for n in _SMALL])
    total = _unpack(_sum_devices(_allgather_devices(partial)), small_shapes)
    loss = total[0][0]
    for n, g in zip(_SMALL, total[1:]):
        if n in _CHANNEL_SHARDED:
            width = wts[n].shape[-1]
            g = lax.dynamic_slice_in_dim(g, chip * width, width, axis=-1)
        grads[n] = g

    shapes = [wts[n].shape for n in _SMALL]
    d, nm, nv = _adamw(_pack([wts[n] for n in _SMALL]), _pack([grads[n] for n in _SMALL]),
                       _pack([mom_m[n] for n in _SMALL]), _pack([mom_v[n] for n in _SMALL]))
    for out, packed in ((delta, d), (new_m, nm), (new_v, nv)):
        for n, a in zip(_SMALL, _unpack(packed, shapes)):
            out[n] = a

    return (loss, dx[None], *[grads[n] for n in _WEIGHTS], *[delta[n] for n in _WEIGHTS],
            *[new_m[n] for n in _WEIGHTS], *[new_v[n] for n in _WEIGHTS])
```

```python
import jax
import jax.numpy as jnp
from jax import lax
from jax.experimental import pallas as pl
from jax.experimental.pallas import tpu as pltpu

F32 = jnp.float32
BF16 = jnp.bfloat16

EPS = 1e-6
D = 1024
D_POOL, D_CONF, D_SCONV = 256, 384, 384
D_IN = D_POOL + 2 * D_CONF + 3 * D_SCONV
D_FF = 2816
FF_CHUNK = 1408
HEADS, HEAD_DIM = 4, 256
CONF_K = 31
POOL_WINDOWS = (2, 4, 8, 16)
POOL_GROUP = 64
SUBLANES = 8
HALO = 32
PHASE_ROWS = HALO - SUBLANES
TAP_ROWS = 64
FHALO = 16
TM_FWD = 512
TM_BWD = 256
N_CHIPS = 4
N_DEV = 8
MESH_ID = pl.DeviceIdType.MESH
VMEM_LIMIT = 56 << 20

ADAM_LR, ADAM_B1, ADAM_B2, ADAM_EPS, ADAM_WD, ADAM_STEP = 0.001, 0.9, 0.999, 1e-08, 0.01, 10

C_P = (0, 256)
C_A = (256, 640)
C_G = (640, 1024)
C_B = (1024, 1408)
C_C = (1408, 1792)
C_X = (1792, 2176)


def _nn(a, b):
    return jnp.dot(a, b, preferred_element_type=F32)


def _nt(a, b):
    return lax.dot_general(a, b, (((1,), (1,)), ((), ())), preferred_element_type=F32)


def _tn(a, b):
    return lax.dot_general(a, b, (((0,), (0,)), ((), ())), preferred_element_type=F32)


def _sigmoid(v):
    return 1.0 / (1.0 + jnp.exp(-v))


def _rms(v, g):
    r = lax.rsqrt(jnp.mean(v * v, axis=-1, keepdims=True) + EPS)
    vh = v * r
    return vh * g, vh, r


def _rms_bwd(vh, r, g, dy):
    dvh = dy * g
    dv = r * (dvh - vh * jnp.mean(dvh * vh, axis=-1, keepdims=True))
    return dv, jnp.sum(dy * vh, axis=0, keepdims=True)


def _colsum(v):
    return jnp.sum(v, axis=0, keepdims=True)


def _rows(tm, n, nt=None):
    if nt is None:
        return pl.BlockSpec((tm, n), lambda i: (i, 0))
    return pl.BlockSpec((tm, n), lambda i: (nt - 1 - i, 0))


def _const(shape):
    nd = len(shape)
    return pl.BlockSpec(shape, lambda i: (0,) * nd, pipeline_mode=pl.Buffered(1))


def _acc(shape):
    nd = len(shape)
    return pl.BlockSpec(shape, lambda i: (0,) * nd)


def _sds(shape, dtype):
    return jax.ShapeDtypeStruct(shape, dtype)


_HBM = pl.BlockSpec(memory_space=pltpu.HBM)


def _mesh_pos():
    return lax.axis_index("x"), lax.axis_index("y"), lax.axis_index("c")


def _chip_peers():
    x, y, c = _mesh_pos()
    flips = [(1 - x, y), (x, 1 - y), (1 - x, 1 - y)]
    return 2 * x + y, [((px, py, c), 2 * px + py) for px, py in flips]


def _remote(src, dst, send_sems, recv_sems, idx, dev):
    return pltpu.make_async_remote_copy(src_ref=src, dst_ref=dst, send_sem=send_sems.at[idx],
                                        recv_sem=recv_sems.at[idx], device_id=dev, device_id_type=MESH_ID)


class _Plan:
    def __init__(self, arrays):
        self.arrays = list(arrays)

    def scratch(self):
        n = len(self.arrays) * self.n_peers
        return [pltpu.SemaphoreType.DMA((n,)), pltpu.SemaphoreType.DMA((n,)),
                pltpu.SemaphoreType.DMA((len(self.arrays),))]

    def _copies(self, ins, outs, sems):
        send_sems, recv_sems, local_sems = sems
        me, peers = self.peers()
        own, sends, recvs = [], [], []
        for k in range(len(ins)):
            own.append(pltpu.make_async_copy(self.src(ins, k, me), self.dst(outs, k, me), local_sems.at[k]))
            for j, (dev, who) in enumerate(peers):
                idx = self.n_peers * k + j
                sends.append(_remote(self.src(ins, k, who), self.dst(outs, k, me), send_sems, recv_sems, idx, dev))
                recvs.append(_remote(self.src(ins, k, me), self.dst(outs, k, who), send_sems, recv_sems, idx, dev))
        return own, sends, recvs

    def start(self, ins, outs, sems):
        own, sends, _ = self._copies(ins, outs, sems)
        for cp in own + sends:
            cp.start()

    def wait(self, ins, outs, sems):
        own, sends, recvs = self._copies(ins, outs, sems)
        for cp in recvs:
            cp.wait_recv()
        for cp in sends:
            cp.wait_send()
        for cp in own:
            cp.wait()


class _Gather(_Plan):
    tag = "gather"
    n_peers = N_CHIPS - 1

    def peers(self):
        return _chip_peers()

    def out_shapes(self):
        return [_sds((N_CHIPS,) + a.shape, a.dtype) for a in self.arrays]

    def src(self, ins, k, chip):
        return ins[k]

    def dst(self, outs, k, chip):
        return outs[k].at[chip]


class _Scatter(_Plan):
    tag = "scatter"
    n_peers = N_DEV - 1

    def __init__(self, arrays):
        super().__init__(arrays)
        self.rows = [a.shape[0] // N_DEV for a in self.arrays]
        self.offs = [sum(self.rows[:k]) for k in range(len(self.rows))]

    def peers(self):
        x, y, c = _mesh_pos()
        flip = lambda v, on: 1 - v if on else v
        others = [(flip(x, m & 4), flip(y, m & 2), flip(c, m & 1)) for m in range(1, N_DEV)]
        return 4 * x + 2 * y + c, [(dev, 4 * dev[0] + 2 * dev[1] + dev[2]) for dev in others]

    def out_shapes(self):
        a = self.arrays[0]
        return [_sds((N_DEV, sum(self.rows), a.shape[1]), a.dtype)]

    def src(self, ins, k, dev):
        r = self.rows[k]
        return ins[k].at[pl.ds(pl.multiple_of(dev * r, 16), r), :]

    def dst(self, outs, k, dev):
        return outs[0].at[dev, pl.ds(self.offs[k], self.rows[k]), :]


def _fused_call(name, body, grid, in_specs, out_specs, out_shape, scratch, args, comm=None, sem=("arbitrary",)):
    n_in, n_out, n_scr = len(in_specs), len(out_specs), len(scratch)
    c_in = comm.arrays if comm else []
    c_out = comm.out_shapes() if comm else []
    c_scr = comm.scratch() if comm else []

    def kernel_fn(*refs):
        ins, cins = refs[:n_in], refs[n_in:n_in + len(c_in)]
        o0 = n_in + len(c_in)
        outs, couts = refs[o0:o0 + n_out], refs[o0 + n_out:o0 + n_out + len(c_out)]
        s0 = o0 + n_out + len(c_out)
        scr, csems = refs[s0:s0 + n_scr], refs[s0 + n_scr:]
        if comm:
            body(ins, outs, scr, lambda: comm.start(cins, couts, csems), lambda: comm.wait(cins, couts, csems))
        else:
            body(ins, outs, scr, None, None)

    res = pl.pallas_call(
        kernel_fn, grid=grid, name=name + ("_" + comm.tag if comm else ""),
        in_specs=list(in_specs) + [_HBM] * len(c_in), out_specs=list(out_specs) + [_HBM] * len(c_out),
        out_shape=list(out_shape) + c_out, scratch_shapes=list(scratch) + c_scr,
        compiler_params=pltpu.CompilerParams(dimension_semantics=sem, vmem_limit_bytes=VMEM_LIMIT),
    )(*args, *c_in)
    return res[:n_out], res[n_out:]


def _bracket(start, wait, first, last):
    if start is not None:
        pl.when(first)(start)

    def finish():
        if wait is not None:
            pl.when(last)(wait)
    return finish


def _gather_two_level(shards):
    n = len(shards)
    per = 2 * N_CHIPS - 1

    def body(*refs):
        ins, outs = refs[:n], refs[n:2 * n]
        send_sems, recv_sems = refs[2 * n:]
        x, y, c = _mesh_pos()
        me, peers = _chip_peers()
        sibling = (x, y, 1 - c)

        def half(ref, k, core):
            h = shards[k].shape[0] // 2
            return ref.at[pl.ds(pl.multiple_of(core * h, 16), h), :]

        first, passed = [], []
        for k in range(n):
            first.append(_remote(ins[k], outs[k].at[me], send_sems, recv_sems, per * k + 3, sibling))
            for j, (dev, chip) in enumerate(peers):
                first.append(_remote(half(ins[k], k, c), half(outs[k].at[me], k, c), send_sems, recv_sems,
                                     per * k + j, dev))
        for cp in first:
            cp.start()
        for k in range(n):
            for j, (dev, chip) in enumerate(peers):
                landed = half(outs[k].at[chip], k, c)
                _remote(landed, landed, send_sems, recv_sems, per * k + j, dev).wait_recv()
                fwd = _remote(landed, landed, send_sems, recv_sems, per * k + 4 + j, sibling)
                fwd.start()
                passed.append(fwd)
        for k in range(n):
            _remote(ins[k], outs[k].at[me], send_sems, recv_sems, per * k + 3, sibling).wait_recv()
            for j, (dev, chip) in enumerate(peers):
                other = half(outs[k].at[chip], k, 1 - c)
                _remote(other, other, send_sems, recv_sems, per * k + 4 + j, sibling).wait_recv()
        for cp in first + passed:
            cp.wait_send()

    return pl.pallas_call(
        body, name="gather_two_level", in_specs=[_HBM] * n, out_specs=[_HBM] * n,
        out_shape=[_sds((N_CHIPS,) + s.shape, s.dtype) for s in shards],
        scratch_shapes=[pltpu.SemaphoreType.DMA((per * n,)), pltpu.SemaphoreType.DMA((per * n,))],
    )(*shards)


def _pair_halves(bufs):
    n = len(bufs)

    def body(*refs):
        outs = refs[n:2 * n]
        send_sems, recv_sems = refs[2 * n:]
        x, y, c = _mesh_pos()
        sibling = (x, y, 1 - c)
        sends = [_remote(outs[k].at[c], outs[k].at[c], send_sems, recv_sems, k, sibling) for k in range(n)]
        for cp in sends:
            cp.start()
        for k in range(n):
            _remote(outs[k].at[c], outs[k].at[1 - c], send_sems, recv_sems, k, sibling).wait_recv()
        for cp in sends:
            cp.wait_send()

    return pl.pallas_call(
        body, name="pair_halves", in_specs=[_HBM] * n, out_specs=[_HBM] * n,
        out_shape=[_sds(a.shape, a.dtype) for a in bufs], input_output_aliases={k: k for k in range(n)},
        scratch_shapes=[pltpu.SemaphoreType.DMA((n,)), pltpu.SemaphoreType.DMA((n,))],
    )(*bufs)


def _allgather_devices(v):
    m_per, n = v.shape

    def body(v_ref, out_ref, send_sems, recv_sems, local_sem):
        x, y, c = _mesh_pos()
        me, sibling = (x, y, c), (x, y, 1 - c)
        chips = [(1 - x, y), (x, 1 - y), (1 - x, 1 - y)]

        def rows(px, py, pc):
            return out_ref.at[4 * px + 2 * py + pc]

        def copy(k, block, to, src=None):
            return _remote(rows(*block) if src is None else src, rows(*block), send_sems, recv_sems, k, to)

        mine = pltpu.make_async_copy(v_ref, rows(*me), local_sem)
        mine.start()
        first = [copy(0, me, sibling, src=v_ref)]
        first += [copy(1 + j, me, (*chip, c), src=v_ref) for j, chip in enumerate(chips)]
        for cp in first:
            cp.start()
        passed = [copy(4 + j, (*chip, c), sibling) for j, chip in enumerate(chips)]
        for j, chip in enumerate(chips):
            copy(1 + j, (*chip, c), me).wait_recv()
            passed[j].start()
        copy(0, sibling, me).wait_recv()
        for j, chip in enumerate(chips):
            copy(4 + j, (*chip, 1 - c), me).wait_recv()
        for cp in first + passed:
            cp.wait_send()
        mine.wait()

    return pl.pallas_call(
        body, name="allgather_devices", out_shape=_sds((N_DEV, m_per, n), v.dtype),
        in_specs=[pl.BlockSpec(memory_space=pltpu.VMEM)], out_specs=pl.BlockSpec(memory_space=pltpu.VMEM),
        scratch_shapes=[pltpu.SemaphoreType.DMA((7,)), pltpu.SemaphoreType.DMA((7,)), pltpu.SemaphoreType.DMA],
        compiler_params=pltpu.CompilerParams(vmem_limit_bytes=VMEM_LIMIT),
    )(v)


def _pool_lane():
    return lax.broadcasted_iota(jnp.int32, (1, D_POOL), 1)


def _pool_count(t0, tm):
    lane = _pool_lane()
    w = jnp.where(lane < 64, 2, jnp.where(lane < 128, 4, jnp.where(lane < 192, 8, 16)))
    pos1 = lax.broadcasted_iota(jnp.int32, (tm, D_POOL), 0) + (t0 + 1)
    return jnp.minimum(pos1, w).astype(F32)


def _fill_bands(band_ref, tm, causal):
    r = lax.broadcasted_iota(jnp.int32, (tm, tm + HALO), 0)
    s = lax.broadcasted_iota(jnp.int32, (tm, tm + HALO), 1)
    d = (r + HALO - s) if causal else (s - r)
    for g, w in enumerate(POOL_WINDOWS):
        band_ref[g] = jnp.where((d >= 0) & (d < w), 1.0, 0.0).astype(BF16)


def _window_sums(band_ref, operand, width):
    lane = _pool_lane()
    res = None
    for g in range(len(POOL_WINDOWS)):
        r = _nn(band_ref[g], operand)
        acc = r[:, 0:width]
        for c0 in range(width, r.shape[1], width):
            acc = acc + r[:, c0:c0 + width]
        res = acc if res is None else jnp.where(lane >= POOL_GROUP * g, acc, res)
    return res


def _phase_copies(src, phases, tm):
    for b in range(1, SUBLANES):
        phases[b - 1] = src[b:b + tm + PHASE_ROWS, :]


def _tap(src, phases, off, rows, r0=0):
    a, b = divmod(off, SUBLANES)
    lo = SUBLANES * a + r0
    if b == 0:
        return src[lo:lo + rows, :]
    return phases[b - 1, lo:lo + rows, :]


def _layer_norm_stats(v1):
    mu = jnp.mean(v1, axis=-1, keepdims=True)
    xc = v1 - mu
    rs = lax.rsqrt(jnp.mean(xc * xc, axis=-1, keepdims=True) + EPS)
    return xc * rs, rs


def _mix_fwd(x, g1, wint, maps_bd, scale, dww, dwb, lng, lnb, sw, wout, g2, comm=None):
    t_len = x.shape[0]
    tm = min(TM_FWD, t_len)
    nt = t_len // tm
    h0 = HALO

    def body(ins, outs, scr, start, wait):
        (x_ref, g1_ref, wint_ref, maps_ref, scale_ref, dww_ref, dwb_ref, lng_ref, lnb_ref, sw_ref, wout_ref,
         g2_ref) = ins
        z_ref, ycat_ref, y_ref, x1_ref, v1_ref, cv_ref, pooled_ref = outs
        pbuf, vbuf, sbuf, phases, band = scr
        i = pl.program_id(0)
        finish = _bracket(start, wait, i == 0, i == nt - 1)

        @pl.when(i == 0)
        def _():
            pbuf[0:h0, :] = jnp.zeros((h0, D_POOL), F32)
            vbuf[0:h0, :] = jnp.zeros((h0, D_CONF), F32)
            sbuf[0:h0, :] = jnp.zeros((h0, D_SCONV), F32)
            _fill_bands(band, tm, True)

        @pl.when(i > 0)
        def _():
            pbuf[0:h0, :] = pbuf[tm:tm + h0, :]
            vbuf[0:h0, :] = vbuf[tm:tm + h0, :]
            sbuf[0:h0, :] = sbuf[tm:tm + h0, :]

        xv = x_ref[...]
        h, _, _ = _rms(xv, g1_ref[...])
        z = _nt(h.astype(BF16), wint_ref[...])
        z_ref[...] = z.astype(BF16)
        zp = z[:, C_P[0]:C_P[1]]
        pbuf[h0:h0 + tm, :] = zp
        vbuf[h0:h0 + tm, :] = z[:, C_A[0]:C_A[1]] * _sigmoid(z[:, C_G[0]:C_G[1]])
        sbuf[h0:h0 + tm, :] = z[:, C_C[0]:C_C[1]] * z[:, C_X[0]:C_X[1]]

        pv = pbuf[...]
        hi = pv.astype(BF16)
        lo = (pv - hi.astype(F32)).astype(BF16)
        sums = _window_sums(band, jnp.concatenate([hi, lo], axis=1), D_POOL)
        pooled = (sums / _pool_count(i * tm, tm) - zp).astype(BF16)
        pooled_ref[...] = pooled
        ycat_ref[:, 0:D_POOL] = (_nn(pooled, maps_ref[...]) * scale_ref[...]).astype(BF16)

        _phase_copies(vbuf, phases, tm)
        base = h0 - (CONF_K - 1)
        for r0 in range(0, tm, TAP_ROWS):
            v1 = dww_ref[0:1, :] * _tap(vbuf, phases, base, TAP_ROWS, r0)
            for j in range(1, CONF_K):
                v1 = v1 + dww_ref[j:j + 1, :] * _tap(vbuf, phases, base + j, TAP_ROWS, r0)
            v1 = v1 + dwb_ref[...]
            v1_ref[r0:r0 + TAP_ROWS, :] = v1
            vh, _ = _layer_norm_stats(v1)
            v2 = vh * lng_ref[...] + lnb_ref[...]
            ycat_ref[r0:r0 + TAP_ROWS, D_POOL:D_POOL + D_CONF] = (v2 * _sigmoid(v2)).astype(BF16)

        cv = (sw_ref[0:1, :] * sbuf[h0 - 2:h0 - 2 + tm, :] + sw_ref[1:2, :] * sbuf[h0 - 1:h0 - 1 + tm, :]
              + sw_ref[2:3, :] * sbuf[h0:h0 + tm, :])
        cv_ref[...] = cv
        ycat_ref[:, D_POOL + D_CONF:D] = (z[:, C_B[0]:C_B[1]] * cv).astype(BF16)

        yb = _nn(ycat_ref[...], wout_ref[...]).astype(BF16)
        y_ref[...] = yb
        yn, _, _ = _rms(yb.astype(F32), g2_ref[...])
        x1_ref[...] = xv + yn
        finish()

    return _fused_call(
        "mix_fwd", body, (nt,),
        [_rows(tm, D), _const((1, D)), _const((D_IN, D)), _const((D_POOL, D_POOL)), _const((1, D_POOL)),
         _const((CONF_K, D_CONF)), _const((1, D_CONF)), _const((1, D_CONF)), _const((1, D_CONF)),
         _const((3, D_SCONV)), _const((D, D)), _const((1, D))],
        [_rows(tm, D_IN), _rows(tm, D), _rows(tm, D), _rows(tm, D), _rows(tm, D_CONF), _rows(tm, D_SCONV),
         _rows(tm, D_POOL)],
        [_sds((t_len, D_IN), BF16), _sds((t_len, D), BF16), _sds((t_len, D), BF16), _sds((t_len, D), F32),
         _sds((t_len, D_CONF), F32), _sds((t_len, D_SCONV), F32), _sds((t_len, D_POOL), BF16)],
        [pltpu.VMEM((h0 + tm, D_POOL), F32), pltpu.VMEM((h0 + tm, D_CONF), F32),
         pltpu.VMEM((h0 + tm, D_SCONV), F32), pltpu.VMEM((SUBLANES - 1, tm + PHASE_ROWS, D_CONF), F32),
         pltpu.VMEM((len(POOL_WINDOWS), tm, tm + h0), BF16)],
        (x, g1, wint, maps_bd, scale, dww, dwb, lng, lnb, sw, wout, g2), comm)


def _mix_bwd(dx1, y, x, z, ycat, v1, cv, pooled, g1, wint, maps_bd, scale, dww, lng, lnb, sw, wout, g2, comm=None):
    t_len = x.shape[0]
    tm = min(TM_BWD, t_len)
    nt = t_len // tm
    h0 = HALO

    def body(ins, outs, scr, start, wait):
        (dx1_ref, y_ref, x_ref, z_ref, ycat_ref, v1_ref, cv_ref, pooled_ref, g1_ref, wint_ref, maps_ref, scale_ref,
         dww_ref, lng_ref, lnb_ref, sw_ref, wout_ref, g2_ref) = ins
        dx_ref, dwin_ref, dwout_ref, dg1_ref, dg2_ref, dmaps_ref, dscale_ref, ddww_ref, misc_ref = outs
        ebuf, dvbuf, dcbuf, phases, band, dzbuf, acc_in, acc_out, ddacc = scr
        s = pl.program_id(0)
        ti = nt - 1 - s
        finish = _bracket(start, wait, s == 0, s == nt - 1)

        @pl.when(s == 0)
        def _():
            ebuf[tm:tm + h0, :] = jnp.zeros((h0, D_POOL), F32)
            dvbuf[tm:tm + h0, :] = jnp.zeros((h0, D_CONF), F32)
            dcbuf[tm:tm + h0, :] = jnp.zeros((h0, D_SCONV), F32)
            _fill_bands(band, tm, False)
            acc_in[...] = jnp.zeros_like(acc_in)
            acc_out[...] = jnp.zeros_like(acc_out)
            ddacc[...] = jnp.zeros_like(ddacc)
            dg1_ref[...] = jnp.zeros_like(dg1_ref)
            dg2_ref[...] = jnp.zeros_like(dg2_ref)
            dmaps_ref[...] = jnp.zeros_like(dmaps_ref)
            dscale_ref[...] = jnp.zeros_like(dscale_ref)
            ddww_ref[...] = jnp.zeros_like(ddww_ref)
            misc_ref[...] = jnp.zeros_like(misc_ref)

        @pl.when(s > 0)
        def _():
            ebuf[tm:tm + h0, :] = ebuf[0:h0, :]
            dvbuf[tm:tm + h0, :] = dvbuf[0:h0, :]
            dcbuf[tm:tm + h0, :] = dcbuf[0:h0, :]

        yv = y_ref[...].astype(F32)
        _, yh, yr = _rms(yv, g2_ref[...])
        dy, dg2 = _rms_bwd(yh, yr, g2_ref[...], dx1_ref[...])
        dg2_ref[...] += dg2
        dyb = dy.astype(BF16)
        acc_out[...] += _tn(ycat_ref[...], dyb)
        dycat = _nt(dyb, wout_ref[...])
        dya = dycat[:, 0:D_POOL]
        dyb2 = dycat[:, D_POOL:D_POOL + D_CONF]
        dyc = dycat[:, D_POOL + D_CONF:D]

        pooled_v = pooled_ref[...]
        pm = _nn(pooled_v, maps_ref[...])
        dscale_ref[...] += _colsum(dya * pm)
        dq = (dya * scale_ref[...]).astype(BF16)
        dmaps_ref[...] += _tn(pooled_v, dq)
        dpooled = _nt(dq, maps_ref[...])
        ebuf[0:tm, :] = dpooled / _pool_count(ti * tm, tm)
        dzbuf[:, C_P[0]:C_P[1]] = (_window_sums(band, ebuf[...].astype(BF16), D_POOL) - dpooled).astype(BF16)

        vh, rs = _layer_norm_stats(v1_ref[...])
        v2 = vh * lng_ref[...] + lnb_ref[...]
        s2 = _sigmoid(v2)
        dv2 = dyb2 * (s2 * (1.0 + v2 * (1.0 - s2)))
        misc_ref[1:2, :] += _colsum(dv2 * vh)
        misc_ref[2:3, :] += _colsum(dv2)
        dvh = dv2 * lng_ref[...]
        dv1 = rs * (dvh - jnp.mean(dvh, axis=-1, keepdims=True) - vh * jnp.mean(dvh * vh, axis=-1, keepdims=True))
        misc_ref[0:1, :] += _colsum(dv1)
        dvbuf[0:tm, :] = dv1
        _phase_copies(dvbuf, phases, tm)
        for r0 in range(0, tm, TAP_ROWS):
            blk = slice(r0, r0 + TAP_ROWS)
            za = z_ref[blk, C_A[0]:C_A[1]].astype(F32)
            sg = _sigmoid(z_ref[blk, C_G[0]:C_G[1]].astype(F32))
            v0 = za * sg
            dv0 = None
            for k in range(CONF_K):
                j = CONF_K - 1 - k
                dk = _tap(dvbuf, phases, k, TAP_ROWS, r0)
                prod = v0 * dk
                part = prod[0:SUBLANES, :]
                for q in range(SUBLANES, TAP_ROWS, SUBLANES):
                    part = part + prod[q:q + SUBLANES, :]
                ddacc[SUBLANES * j:SUBLANES * (j + 1), :] += part
                term = dww_ref[j:j + 1, :] * dk
                dv0 = term if dv0 is None else dv0 + term
            dzbuf[blk, C_A[0]:C_A[1]] = (dv0 * sg).astype(BF16)
            dzbuf[blk, C_G[0]:C_G[1]] = (dv0 * za * sg * (1.0 - sg)).astype(BF16)

        zb = z_ref[:, C_B[0]:C_B[1]].astype(F32)
        zc = z_ref[:, C_C[0]:C_C[1]].astype(F32)
        zx = z_ref[:, C_X[0]:C_X[1]].astype(F32)
        pv = zc * zx
        dzbuf[:, C_B[0]:C_B[1]] = (dyc * cv_ref[...]).astype(BF16)
        dcbuf[0:tm, :] = dyc * zb
        dp = None
        for k in range(3):
            j = 2 - k
            dk = dcbuf[k:k + tm, :]
            misc_ref[3 + j:4 + j, :] += _colsum(pv * dk)
            term = sw_ref[j:j + 1, :] * dk
            dp = term if dp is None else dp + term
        dzbuf[:, C_C[0]:C_C[1]] = (dp * zx).astype(BF16)
        dzbuf[:, C_X[0]:C_X[1]] = (dp * zc).astype(BF16)

        xv = x_ref[...]
        h, xh, xr = _rms(xv, g1_ref[...])
        dz = dzbuf[...]
        acc_in[...] += _tn(dz, h.astype(BF16))
        dh = _nn(dz, wint_ref[...])
        dxn, dg1 = _rms_bwd(xh, xr, g1_ref[...], dh)
        dg1_ref[...] += dg1
        dx_ref[...] = dx1_ref[...] + dxn

        @pl.when(s == nt - 1)
        def _():
            dwin_ref[...] = acc_in[...].astype(BF16)
            dwout_ref[...] = acc_out[...].astype(BF16)
            for j in range(CONF_K):
                ddww_ref[j:j + 1, :] = _colsum(ddacc[SUBLANES * j:SUBLANES * (j + 1), :])

        finish()

    return _fused_call(
        "mix_bwd", body, (nt,),
        [_rows(tm, D, nt), _rows(tm, D, nt), _rows(tm, D, nt), _rows(tm, D_IN, nt), _rows(tm, D, nt),
         _rows(tm, D_CONF, nt), _rows(tm, D_SCONV, nt), _rows(tm, D_POOL, nt), _const((1, D)), _const((D_IN, D)),
         _const((D_POOL, D_POOL)), _const((1, D_POOL)), _const((CONF_K, D_CONF)), _const((1, D_CONF)),
         _const((1, D_CONF)), _const((3, D_SCONV)), _const((D, D)), _const((1, D))],
        [_rows(tm, D, nt), _acc((D_IN, D)), _acc((D, D)), _acc((1, D)), _acc((1, D)), _acc((D_POOL, D_POOL)),
         _acc((1, D_POOL)), _acc((32, D_CONF)), _acc((8, D_CONF))],
        [_sds((t_len, D), F32), _sds((D_IN, D), BF16), _sds((D, D), BF16), _sds((1, D), F32), _sds((1, D), F32),
         _sds((D_POOL, D_POOL), F32), _sds((1, D_POOL), F32), _sds((32, D_CONF), F32), _sds((8, D_CONF), F32)],
        [pltpu.VMEM((tm + h0, D_POOL), F32), pltpu.VMEM((tm + h0, D_CONF), F32),
         pltpu.VMEM((tm + h0, D_SCONV), F32), pltpu.VMEM((SUBLANES - 1, tm + PHASE_ROWS, D_CONF), F32),
         pltpu.VMEM((len(POOL_WINDOWS), tm, tm + h0), BF16), pltpu.VMEM((tm, D_IN), BF16),
         pltpu.VMEM((D_IN, D), F32), pltpu.VMEM((D, D), F32), pltpu.VMEM((SUBLANES * CONF_K, D_CONF), F32)],
        (dx1, y, x, z, ycat, v1, cv, pooled, g1, wint, maps_bd, scale, dww, lng, lnb, sw, wout, g2), comm)


def _kv_fwd(mem, gmem, wk, wv):
    def body(mem_ref, g_ref, wk_ref, wv_ref, k_ref, v_ref):
        mn, _, _ = _rms(mem_ref[...], g_ref[...])
        mnb = mn.astype(BF16)
        k_ref[...] = _nn(mnb, wk_ref[...]).astype(BF16)
        v_ref[...] = _nn(mnb, wv_ref[...]).astype(BF16)

    n = mem.shape[0]
    return pl.pallas_call(
        body, name="kv_fwd", out_shape=[_sds((n, D), BF16), _sds((n, D), BF16)],
        compiler_params=pltpu.CompilerParams(vmem_limit_bytes=VMEM_LIMIT),
    )(mem, gmem, wk, wv)


def _kv_bwd(mem, gmem, dk, dv, wk, wv):
    def body(mem_ref, g_ref, dk_ref, dv_ref, wk_ref, wv_ref, dwk_ref, dwv_ref, dg_ref):
        mn, mh, _ = _rms(mem_ref[...], g_ref[...])
        mnb = mn.astype(BF16)
        dkb = dk_ref[...].astype(BF16)
        dvb = dv_ref[...].astype(BF16)
        dwk_ref[...] = _tn(mnb, dkb).astype(BF16)
        dwv_ref[...] = _tn(mnb, dvb).astype(BF16)
        dmn = _nt(dkb, wk_ref[...]) + _nt(dvb, wv_ref[...])
        dg_ref[...] = _colsum(dmn * mh)

    return pl.pallas_call(
        body, name="kv_bwd", out_shape=[_sds((D, D), BF16), _sds((D, D), BF16), _sds((1, D), F32)],
        compiler_params=pltpu.CompilerParams(vmem_limit_bytes=VMEM_LIMIT),
    )(mem, gmem, dk, dv, wk, wv)


def _softmax_rows(s):
    e = jnp.exp(s - jnp.max(s, axis=-1, keepdims=True))
    return e / jnp.sum(e, axis=-1, keepdims=True)


def _xattn_fwd(x1, g3, wq, k, v, wo, g4, comm=None):
    t_len = x1.shape[0]
    tm = min(2 * TM_FWD, t_len)
    nt = t_len // tm
    n_mem = k.shape[0]
    sc = HEAD_DIM ** -0.5

    def body(ins, outs, scr, start, wait):
        x_ref, g3_ref, wq_ref, k_ref, v_ref, wo_ref, g4_ref = ins
        q_ref, p_ref, o_ref, y_ref, x2_ref = outs
        i = pl.program_id(0)
        finish = _bracket(start, wait, i == 0, i == nt - 1)
        xv = x_ref[...]
        h, _, _ = _rms(xv, g3_ref[...])
        qb = _nn(h.astype(BF16), wq_ref[...]).astype(BF16)
        q_ref[...] = qb
        for hd in range(HEADS):
            sl = slice(hd * HEAD_DIM, (hd + 1) * HEAD_DIM)
            pb = _softmax_rows(_nt(qb[:, sl], k_ref[:, sl]) * sc).astype(BF16)
            p_ref[:, hd * n_mem:(hd + 1) * n_mem] = pb
            o_ref[:, sl] = _nn(pb, v_ref[:, sl]).astype(BF16)
        yb = _nn(o_ref[...], wo_ref[...]).astype(BF16)
        y_ref[...] = yb
        yn, _, _ = _rms(yb.astype(F32), g4_ref[...])
        x2_ref[...] = xv + yn
        finish()

    return _fused_call(
        "xattn_fwd", body, (nt,),
        [_rows(tm, D), _const((1, D)), _const((D, D)), _const((n_mem, D)), _const((n_mem, D)), _const((D, D)),
         _const((1, D))],
        [_rows(tm, D), _rows(tm, HEADS * n_mem), _rows(tm, D), _rows(tm, D), _rows(tm, D)],
        [_sds((t_len, D), BF16), _sds((t_len, HEADS * n_mem), BF16), _sds((t_len, D), BF16), _sds((t_len, D), BF16),
         _sds((t_len, D), F32)],
        [], (x1, g3, wq, k, v, wo, g4), comm)


def _xattn_bwd(dx2, y, x1, q, p, o, g3, wq, k, v, wo, g4):
    t_len = x1.shape[0]
    tm = min(TM_FWD, t_len)
    nt = t_len // tm
    n_mem = k.shape[0]
    sc = HEAD_DIM ** -0.5

    def body(ins, outs, scr, start, wait):
        dx2_ref, y_ref, x_ref, q_ref, p_ref, o_ref, g3_ref, wq_ref, k_ref, v_ref, wo_ref, g4_ref = ins
        dx_ref, dwq_ref, dwo_ref, dk_ref, dv_ref, dg3_ref, dg4_ref = outs
        dqbuf, acc_q, acc_o = scr
        s = pl.program_id(0)

        @pl.when(s == 0)
        def _():
            acc_q[...] = jnp.zeros_like(acc_q)
            acc_o[...] = jnp.zeros_like(acc_o)
            dk_ref[...] = jnp.zeros_like(dk_ref)
            dv_ref[...] = jnp.zeros_like(dv_ref)
            dg3_ref[...] = jnp.zeros_like(dg3_ref)
            dg4_ref[...] = jnp.zeros_like(dg4_ref)

        yv = y_ref[...].astype(F32)
        _, yh, yr = _rms(yv, g4_ref[...])
        dy, dg4 = _rms_bwd(yh, yr, g4_ref[...], dx2_ref[...])
        dg4_ref[...] += dg4
        dyb = dy.astype(BF16)
        acc_o[...] += _tn(o_ref[...], dyb)
        do = _nt(dyb, wo_ref[...])
        qb = q_ref[...]
        for hd in range(HEADS):
            sl = slice(hd * HEAD_DIM, (hd + 1) * HEAD_DIM)
            pb = p_ref[:, hd * n_mem:(hd + 1) * n_mem]
            p = pb.astype(F32)
            dob = do[:, sl].astype(BF16)
            dp = _nt(dob, v_ref[:, sl])
            dv_ref[:, sl] += _tn(pb, dob)
            ds = (p * (dp - jnp.sum(dp * p, axis=-1, keepdims=True)) * sc).astype(BF16)
            dqbuf[:, sl] = _nn(ds, k_ref[:, sl]).astype(BF16)
            dk_ref[:, sl] += _tn(ds, qb[:, sl])
        xv = x_ref[...]
        h, xh, xr = _rms(xv, g3_ref[...])
        dq = dqbuf[...]
        acc_q[...] += _tn(h.astype(BF16), dq)
        dh = _nt(dq, wq_ref[...])
        dxn, dg3 = _rms_bwd(xh, xr, g3_ref[...], dh)
        dg3_ref[...] += dg3
        dx_ref[...] = dx2_ref[...] + dxn

        @pl.when(s == nt - 1)
        def _():
            dwq_ref[...] = acc_q[...].astype(BF16)
            dwo_ref[...] = acc_o[...].astype(BF16)

    outs, _ = _fused_call(
        "xattn_bwd", body, (nt,),
        [_rows(tm, D), _rows(tm, D), _rows(tm, D), _rows(tm, D), _rows(tm, HEADS * n_mem), _rows(tm, D), _const((1, D)),
         _const((D, D)), _const((n_mem, D)), _const((n_mem, D)), _const((D, D)), _const((1, D))],
        [_rows(tm, D), _acc((D, D)), _acc((D, D)), _acc((n_mem, D)), _acc((n_mem, D)), _acc((1, D)), _acc((1, D))],
        [_sds((t_len, D), F32), _sds((D, D), BF16), _sds((D, D), BF16), _sds((n_mem, D), F32),
         _sds((n_mem, D), F32), _sds((1, D), F32), _sds((1, D), F32)],
        [pltpu.VMEM((tm, D), BF16), pltpu.VMEM((D, D), F32), pltpu.VMEM((D, D), F32)],
        (dx2, y, x1, q, p, o, g3, wq, k, v, wo, g4))
    return outs


def _ffn_cols(half, part):
    c0 = part * D_FF + half * FF_CHUNK
    return c0, c0 + FF_CHUNK


def _fill_shifts(ref, tm, step):
    t = lax.broadcasted_iota(jnp.int32, (tm, tm), 0)
    s = lax.broadcasted_iota(jnp.int32, (tm, tm), 1)
    ref[0:tm, :] = jnp.where(s == t + step, 1.0, 0.0).astype(BF16)
    ref[tm:2 * tm, :] = jnp.where(s == t + 2 * step, 1.0, 0.0).astype(BF16)


def _edge_terms(near, far, edge, inner):
    r = lax.broadcasted_iota(jnp.int32, (16, near.shape[1]), 0)
    return jnp.where(r == edge, near, 0.0), jnp.where(r == edge, far, jnp.where(r == inner, near, 0.0))


def _ffn_fwd(x2, g5, wupt, wc, wdown, g6, comm=None, target=None):
    t_len = x2.shape[0]
    tm = min(TM_BWD, t_len)
    nt = t_len // tm

    def body(ins, outs, scr, start, wait):
        x_ref, g5_ref, wupt_ref, wc_ref, wdown_ref, g6_ref = ins[:6]
        u_ref, c_ref, a_ref, y_ref, x3_ref = outs[:5]
        carry, shift = scr
        i = pl.program_id(0)
        finish = _bracket(start, wait, i == 0, i == nt - 1)

        @pl.when(i == 0)
        def _():
            carry[...] = jnp.zeros_like(carry)
            _fill_shifts(shift, tm, -1)
            if target is not None:
                outs[5][...] = jnp.zeros_like(outs[5])

        xv = x_ref[...]
        h, _, _ = _rms(xv, g5_ref[...])
        hb = h.astype(BF16)
        for half in range(2):
            conv = []
            for part in range(2):
                c0, c1 = _ffn_cols(half, part)
                w0, w1, w2 = wc_ref[0:1, c0:c1], wc_ref[1:2, c0:c1], wc_ref[2:3, c0:c1]
                u = _nt(hb, wupt_ref[c0:c1, :])
                ub = u.astype(BF16)
                u_ref[:, c0:c1] = ub
                sh = _nn(shift[...], ub)
                cb = w0 * sh[tm:2 * tm, :] + w1 * sh[0:tm, :] + w2 * u
                c_ref[:, c0:c1] = cb.astype(BF16)
                m1, m2 = _edge_terms(carry[SUBLANES - 1:SUBLANES, c0:c1], carry[SUBLANES - 2:SUBLANES - 1, c0:c1], 0, 1)
                c_ref[0:16, c0:c1] = (cb[0:16, :] + w1 * m1 + w0 * m2).astype(BF16)
                carry[:, c0:c1] = u[tm - SUBLANES:tm, :].astype(BF16).astype(F32)
                conv.append(c_ref[:, c0:c1].astype(F32))
            a = (conv[0] * _sigmoid(conv[0]) * conv[1]).astype(BF16)
            a_ref[:, half * FF_CHUNK:(half + 1) * FF_CHUNK] = a
        yb = _nn(a_ref[...], wdown_ref[...]).astype(BF16)
        y_ref[...] = yb
        yn, _, _ = _rms(yb.astype(F32), g6_ref[...])
        if target is None:
            x3_ref[...] = xv + yn
        else:
            err = xv + yn - ins[6][...]
            x3_ref[...] = err * (1.0 / D)
            part = 0.5 * _colsum(jnp.mean(err * err, axis=-1, keepdims=True))
            outs[5][...] += jnp.broadcast_to(part, outs[5].shape)
        finish()

    last = target is not None
    return _fused_call(
        "ffn_fwd_loss" if last else "ffn_fwd", body, (nt,),
        [_rows(tm, D), _const((1, D)), _const((2 * D_FF, D)), _const((3, 2 * D_FF)), _const((D_FF, D)),
         _const((1, D))] + ([_rows(tm, D)] if last else []),
        [_rows(tm, 2 * D_FF), _rows(tm, 2 * D_FF), _rows(tm, D_FF), _rows(tm, D), _rows(tm, D)]
        + ([_acc((8, 128))] if last else []),
        [_sds((t_len, 2 * D_FF), BF16), _sds((t_len, 2 * D_FF), BF16), _sds((t_len, D_FF), BF16),
         _sds((t_len, D), BF16), _sds((t_len, D), F32)] + ([_sds((8, 128), F32)] if last else []),
        [pltpu.VMEM((SUBLANES, 2 * D_FF), F32), pltpu.VMEM((2 * tm, tm), BF16)],
        (x2, g5, wupt, wc, wdown, g6) + ((target,) if last else ()), comm)


def _ffn_bwd(dx3, y, x2, u, c, g5, wupt, wc, wdown, g6, comm=None):
    t_len = x2.shape[0]
    tm = min(TM_BWD, t_len)
    nt = t_len // tm

    def body(ins, outs, scr, start, wait):
        dx3_ref, y_ref, x_ref, u_ref, c_ref, g5_ref, wupt_ref, wc_ref, wdown_ref, g6_ref = ins
        dx_ref, du_ref, dyo_ref, h_ref, dg5_ref, dg6_ref, dwc_ref = outs
        carry, shift = scr
        s = pl.program_id(0)
        finish = _bracket(start, wait, s == 0, s == nt - 1)

        @pl.when(s == 0)
        def _():
            carry[...] = jnp.zeros_like(carry)
            _fill_shifts(shift, tm, 1)
            dg5_ref[...] = jnp.zeros_like(dg5_ref)
            dg6_ref[...] = jnp.zeros_like(dg6_ref)
            dwc_ref[...] = jnp.zeros_like(dwc_ref)

        yv = y_ref[...].astype(F32)
        _, yh, yr = _rms(yv, g6_ref[...])
        dy, dg6 = _rms_bwd(yh, yr, g6_ref[...], dx3_ref[...])
        dg6_ref[...] += dg6
        dyb = dy.astype(BF16)
        dyo_ref[...] = dyb
        xv = x_ref[...]
        h, xh, xr = _rms(xv, g5_ref[...])
        h_ref[...] = h.astype(BF16)

        for half in range(2):
            g0, g1 = _ffn_cols(half, 0)
            v0, v1 = _ffn_cols(half, 1)
            gt = c_ref[:, g0:g1].astype(F32)
            vl = c_ref[:, v0:v1].astype(F32)
            sg = _sigmoid(gt)
            sil = gt * sg
            da = _nt(dyb, wdown_ref[half * FF_CHUNK:(half + 1) * FF_CHUNK, :])
            dcs = (da * vl * (sg * (1.0 + gt * (1.0 - sg))), da * sil)
            for part in range(2):
                c0, c1 = _ffn_cols(half, part)
                w0, w1, w2 = wc_ref[0:1, c0:c1], wc_ref[1:2, c0:c1], wc_ref[2:3, c0:c1]
                dc = dcs[part]
                sh = _nn(shift[...], dc.astype(BF16))
                d1, d2 = sh[0:tm, :], sh[tm:2 * tm, :]
                m1, m2 = _edge_terms(carry[0:1, c0:c1], carry[1:2, c0:c1], 15, 14)
                carry[:, c0:c1] = dc[0:SUBLANES, :]
                uu = u_ref[:, c0:c1].astype(F32)
                ut = u_ref[tm - 16:tm, c0:c1].astype(F32)
                dwc_ref[2:3, c0:c1] += _colsum(uu * dc)
                dwc_ref[1:2, c0:c1] += _colsum(uu * d1) + _colsum(ut * m1)
                dwc_ref[0:1, c0:c1] += _colsum(uu * d2) + _colsum(ut * m2)
                du = w2 * dc + w1 * d1 + w0 * d2
                du_ref[:, c0:c1] = du.astype(BF16)
                du_ref[tm - 16:tm, c0:c1] = (du[tm - 16:tm, :] + w1 * m1 + w0 * m2).astype(BF16)
        dh = _nn(du_ref[...], wupt_ref[...])
        dxn, dg5 = _rms_bwd(xh, xr, g5_ref[...], dh)
        dg5_ref[...] += dg5
        dx_ref[...] = dx3_ref[...] + dxn
        finish()

    return _fused_call(
        "ffn_bwd", body, (nt,),
        [_rows(tm, D, nt), _rows(tm, D, nt), _rows(tm, D, nt), _rows(tm, 2 * D_FF, nt), _rows(tm, 2 * D_FF, nt),
         _const((1, D)), _const((2 * D_FF, D)), _const((3, 2 * D_FF)), _const((D_FF, D)), _const((1, D))],
        [_rows(tm, D, nt), _rows(tm, 2 * D_FF, nt), _rows(tm, D, nt), _rows(tm, D, nt), _acc((1, D)), _acc((1, D)),
         _acc((8, 2 * D_FF))],
        [_sds((t_len, D), F32), _sds((t_len, 2 * D_FF), BF16), _sds((t_len, D), BF16), _sds((t_len, D), BF16),
         _sds((1, D), F32), _sds((1, D), F32), _sds((8, 2 * D_FF), F32)],
        [pltpu.VMEM((SUBLANES, 2 * D_FF), F32), pltpu.VMEM((2 * tm, tm), BF16)],
        (dx3, y, x2, u, c, g5, wupt, wc, wdown, g6), comm)


def _tn_matmul(a, b):
    t_len, m = a.shape
    bm = FF_CHUNK
    bt = min(4 * TM_FWD, t_len)
    nt = t_len // bt

    def body(a_ref, b_ref, o_ref, acc):
        t = pl.program_id(1)

        @pl.when(t == 0)
        def _():
            acc[...] = jnp.zeros_like(acc)

        acc[...] += _tn(a_ref[...], b_ref[...])

        @pl.when(t == nt - 1)
        def _():
            o_ref[...] = acc[...].astype(BF16)

    return pl.pallas_call(
        body, grid=(m // bm, nt), name="tn_matmul",
        in_specs=[pl.BlockSpec((bt, bm), lambda i, t: (t, i)), pl.BlockSpec((bt, D), lambda i, t: (t, 0))],
        out_specs=pl.BlockSpec((bm, D), lambda i, t: (i, 0)),
        out_shape=_sds((m, D), BF16),
        scratch_shapes=[pltpu.VMEM((bm, D), F32)],
        compiler_params=pltpu.CompilerParams(dimension_semantics=("parallel", "arbitrary"),
                                             vmem_limit_bytes=VMEM_LIMIT),
    )(a, b)


BLOCK_BYTES = 1 << 20


def _row_block(rows, cols):
    limit = max(16, BLOCK_BYTES // (4 * cols))
    best = None
    for rb in range(16, min(rows, limit) + 1, 16):
        if rows % rb == 0:
            best = rb
    return best or rows


def _elementwise(name, fn, ins, out_dtypes, comm=None):
    rows, cols = ins[0].shape
    rb = _row_block(rows, cols)
    nt = rows // rb

    def body(in_refs, out_refs, scr, start, wait):
        i = pl.program_id(0)
        finish = _bracket(start, wait, i == 0, i == nt - 1)
        res = fn(*[r[...] for r in in_refs])
        for o_ref, r in zip(out_refs, res):
            o_ref[...] = r
        finish()

    spec = pl.BlockSpec((rb, cols), lambda i: (i, 0))
    outs, extra = _fused_call(
        name, body, (nt,), [spec] * len(ins), [spec] * len(out_dtypes),
        [_sds((rows, cols), dt) for dt in out_dtypes], [], tuple(ins), comm,
        sem=("arbitrary",) if comm else ("parallel",))
    return list(outs) + list(extra)


def _cast_bf16(w):
    return _elementwise("cast_bf16", lambda v: (v.astype(BF16),), [w], [BF16])[0]


def _adam_math(w, g, m, v):
    nm = ADAM_B1 * m + (1.0 - ADAM_B1) * g
    nv = ADAM_B2 * v + (1.0 - ADAM_B2) * (g * g)
    m_hat = nm / (1.0 - ADAM_B1 ** ADAM_STEP)
    v_hat = nv / (1.0 - ADAM_B2 ** ADAM_STEP)
    return -ADAM_LR * (m_hat / (jnp.sqrt(v_hat) + ADAM_EPS) + ADAM_WD * w), nm, nv


def _adamw(w, g, m, v, comm=None):
    return _elementwise("adamw", _adam_math, [w, g, m, v], [F32, F32, F32], comm)


def _sum_blocks(parts, slot):
    n, rows, cols = parts.shape
    rb = _row_block(rows, cols)

    def body(slot_ref, p_ref, o_ref):
        acc = p_ref[0].astype(F32)
        for j in range(1, n):
            acc = acc + p_ref[j].astype(F32)
        o_ref[...] = acc

    return pl.pallas_call(
        body, name="sum_blocks", out_shape=_sds((2, rows, cols), F32),
        grid_spec=pltpu.PrefetchScalarGridSpec(
            num_scalar_prefetch=1, grid=(rows // rb,),
            in_specs=[pl.BlockSpec((n, rb, cols), lambda i, s: (0, i, 0))],
            out_specs=pl.BlockSpec((None, rb, cols), lambda i, s: (s[0], i, 0))),
        compiler_params=pltpu.CompilerParams(dimension_semantics=("parallel",), vmem_limit_bytes=VMEM_LIMIT),
    )(slot.reshape(1).astype(jnp.int32), parts)


def _sum_devices(parts):
    n, rows, cols = parts.shape

    def body(p_ref, o_ref):
        acc = p_ref[0]
        for j in range(1, n):
            acc = acc + p_ref[j]
        o_ref[...] = acc

    return pl.pallas_call(
        body, name="sum_devices", out_shape=_sds((rows, cols), F32),
        compiler_params=pltpu.CompilerParams(vmem_limit_bytes=VMEM_LIMIT),
    )(parts)


def _pack(parts):
    flat = jnp.concatenate([p.reshape(-1) for p in parts])
    rows = -(-flat.shape[0] // 1024) * 8
    return jnp.pad(flat, (0, rows * 128 - flat.shape[0])).reshape(rows, 128)


def _unpack(packed, shapes):
    flat = packed.reshape(-1)
    out, off = [], 0
    for shp in shapes:
        size = 1
        for d in shp:
            size *= d
        out.append(flat[off:off + size].reshape(shp))
        off += size
    return out


_BIG = {'w_in': ('wint', True), 'w_out': ('wout', False), 'xattn_wq': ('wq', False), 'xattn_wk': ('wk', False),
        'xattn_wv': ('wv', False), 'xattn_wo': ('wo', False), 'ffn_w_up': ('wupt', True),
        'ffn_w_down': ('wdown', False)}
_KEYS = [key for key, _ in _BIG.values()]
_GATHER_EARLY = ("wint", "wout", "wq", "wk", "wv", "wo")
_GATHER_WITH = {"mix": ("wupt",), "xattn": ("wdown",), "ffn": _GATHER_EARLY}
_SCATTER_LATE = ("wupt", "wdown", "wq", "wk", "wv", "wo")
_SCATTER_NEXT = ("wint", "wout")


def _block_diag(maps):
    out = jnp.zeros((D_POOL, D_POOL), maps.dtype)
    for g in range(len(POOL_WINDOWS)):
        out = lax.dynamic_update_slice(out, maps[g], (g * POOL_GROUP, g * POOL_GROUP))
    return out


def _local_step(x, mem, target, small, big, shards=None):
    distributed = shards is not None
    depth = len(shards) if distributed else len(big)
    big = list(big)
    row = lambda a: a.reshape(1, -1)
    gmem = row(small["mem_norm"])
    saved = []
    for l in range(depth):
        w = big[l]
        nxt = {}

        def plan(stage):
            layer = l + 1 if stage == "ffn" else l
            if distributed and layer < depth:
                return _Gather([shards[layer][key] for key in _GATHER_WITH[stage]])
            return None

        def landed(stage, outs):
            into = nxt if stage == "ffn" else w
            for key, g in zip(_GATHER_WITH[stage], outs):
                into[key] = g.reshape(-1, D)

        sp = dict(
            g1=row(small["mix_pre_norm"][l]), g2=row(small["mix_post_norm"][l]),
            maps=_block_diag(small["pool_maps"][l]).astype(BF16), scale=row(small["pool_scale"][l]),
            dww=small["conf_dw_w"][l], dwb=row(small["conf_dw_b"][l]), lng=row(small["conf_ln_g"][l]),
            lnb=row(small["conf_ln_b"][l]), sw=small["sconv_w"][l],
            g3=row(small["xattn_pre_norm"][l]), g4=row(small["xattn_post_norm"][l]),
            g5=row(small["ffn_pre_norm"][l]), g6=row(small["ffn_post_norm"][l]), wc=small["ffn_conv_w"][l])
        (z, ycat, y1, x1, v1, cv, pooled), got = _mix_fwd(
            x, sp["g1"], w["wint"], sp["maps"], sp["scale"], sp["dww"], sp["dwb"], sp["lng"], sp["lnb"], sp["sw"],
            w["wout"], sp["g2"], plan("mix"))
        landed("mix", got)
        k, v = _kv_fwd(mem, gmem, w["wk"], w["wv"])
        (q, p, o, y2, x2), got = _xattn_fwd(x1, sp["g3"], w["wq"], k, v, w["wo"], sp["g4"], plan("xattn"))
        landed("xattn", got)
        ffn_out, got = _ffn_fwd(x2, sp["g5"], w["wupt"], sp["wc"], w["wdown"], sp["g6"], plan("ffn"),
                                target if l == depth - 1 else None)
        u, c, a, y3, x3 = ffn_out[:5]
        landed("ffn", got)
        if nxt:
            big.append(nxt)
        saved.append(dict(sp=sp, x=x, z=z, ycat=ycat, y1=y1, x1=x1, v1=v1, cv=cv, pooled=pooled, k=k, v=v, q=q, p=p, o=o,
                          y2=y2, x2=x2, u=u, c=c, a=a, y3=y3))
        x = x3

    dx, loss_blk = x, ffn_out[5]
    big_grads = [None] * depth
    sg = {n: [None] * depth for n in ("mix_pre_norm", "mix_post_norm", "pool_maps", "pool_scale", "conf_dw_w",
                                      "conf_dw_b", "conf_ln_g", "conf_ln_b", "sconv_w", "xattn_pre_norm",
                                      "xattn_post_norm", "ffn_pre_norm", "ffn_post_norm", "ffn_conv_w")}
    dgmem = None
    pending = None
    for l in reversed(range(depth)):
        w, s = big[l], saved[l]
        sp = s["sp"]
        comm = _Scatter(pending) if distributed and pending is not None else None
        (dx, du, dy3, h3, dg5, dg6, dwc), got = _ffn_bwd(dx, s["y3"], s["x2"], s["u"], s["c"], sp["g5"], w["wupt"],
                                                        sp["wc"], w["wdown"], sp["g6"], comm)
        if comm is not None:
            big_grads[l + 1] = (big_grads[l + 1], got[0])
        g = dict(wupt=_tn_matmul(du, h3), wdown=_tn_matmul(s["a"], dy3))
        dx, g["wq"], g["wo"], dk, dv, dg3, dg4 = _xattn_bwd(dx, s["y2"], s["x1"], s["q"], s["p"], s["o"], sp["g3"], w["wq"],
                                                            s["k"], s["v"], w["wo"], sp["g4"])
        g["wk"], g["wv"], dgm = _kv_bwd(mem, gmem, dk, dv, w["wk"], w["wv"])
        dgmem = dgm if dgmem is None else dgmem + dgm
        comm = _Scatter([g[key] for key in _SCATTER_LATE]) if distributed else None
        (dx, g["wint"], g["wout"], dg1, dg2, dmaps, dscale, ddww, misc), got = _mix_bwd(
            dx, s["y1"], s["x"], s["z"], s["ycat"], s["v1"], s["cv"], s["pooled"], sp["g1"], w["wint"], sp["maps"],
            sp["scale"], sp["dww"], sp["lng"], sp["lnb"], sp["sw"], w["wout"], sp["g2"], comm)
        if distributed:
            big_grads[l] = got[0]
            pending = [g[key] for key in _SCATTER_NEXT]
        else:
            big_grads[l] = g
        sg["mix_pre_norm"][l] = dg1[0]
        sg["mix_post_norm"][l] = dg2[0]
        sg["pool_maps"][l] = jnp.stack([dmaps[i * 64:(i + 1) * 64, i * 64:(i + 1) * 64] for i in range(4)])
        sg["pool_scale"][l] = dscale[0]
        sg["conf_dw_w"][l] = ddww[0:CONF_K]
        sg["conf_dw_b"][l] = misc[0]
        sg["conf_ln_g"][l] = misc[1]
        sg["conf_ln_b"][l] = misc[2]
        sg["sconv_w"][l] = misc[3:6]
        sg["xattn_pre_norm"][l] = dg3[0]
        sg["xattn_post_norm"][l] = dg4[0]
        sg["ffn_pre_norm"][l] = dg5[0]
        sg["ffn_post_norm"][l] = dg6[0]
        sg["ffn_conv_w"][l] = dwc[0:3]
    if distributed:
        big_grads[0] = (big_grads[0], pending)
    small_grads = {n: jnp.stack(vs) for n, vs in sg.items()}
    small_grads["mem_norm"] = dgmem[0]
    return loss_blk, dx, big_grads, small_grads


_WEIGHTS = ['mem_norm', 'mix_pre_norm', 'mix_post_norm', 'w_in', 'pool_maps', 'pool_scale', 'conf_dw_w', 'conf_dw_b',
            'conf_ln_g', 'conf_ln_b', 'sconv_w', 'w_out', 'xattn_pre_norm', 'xattn_post_norm', 'xattn_wq',
            'xattn_wk', 'xattn_wv', 'xattn_wo', 'ffn_pre_norm', 'ffn_post_norm', 'ffn_w_up', 'ffn_conv_w',
            'ffn_w_down']
_CHANNEL_SHARDED = ('conf_dw_w', 'sconv_w', 'ffn_conv_w')
_SMALL = [n for n in _WEIGHTS if n not in _BIG]


def kernel(x, mem, mem_norm, mix_pre_norm, mix_post_norm, w_in, pool_maps, pool_scale, conf_dw_w, conf_dw_b, conf_ln_g, conf_ln_b, sconv_w, w_out, xattn_pre_norm, xattn_post_norm, xattn_wq, xattn_wk, xattn_wv, xattn_wo, ffn_pre_norm, ffn_post_norm, ffn_w_up, ffn_conv_w, ffn_w_down, loss_target, m_mem_norm, m_mix_pre_norm, m_mix_post_norm, m_w_in, m_pool_maps, m_pool_scale, m_conf_dw_w, m_conf_dw_b, m_conf_ln_g, m_conf_ln_b, m_sconv_w, m_w_out, m_xattn_pre_norm, m_xattn_post_norm, m_xattn_wq, m_xattn_wk, m_xattn_wv, m_xattn_wo, m_ffn_pre_norm, m_ffn_post_norm, m_ffn_w_up, m_ffn_conv_w, m_ffn_w_down, v_mem_norm, v_mix_pre_norm, v_mix_post_norm, v_w_in, v_pool_maps, v_pool_scale, v_conf_dw_w, v_conf_dw_b, v_conf_ln_g, v_conf_ln_b, v_sconv_w, v_w_out, v_xattn_pre_norm, v_xattn_post_norm, v_xattn_wq, v_xattn_wk, v_xattn_wv, v_xattn_wo, v_ffn_pre_norm, v_ffn_post_norm, v_ffn_w_up, v_ffn_conv_w, v_ffn_w_down):
    wts = dict(mem_norm=mem_norm, mix_pre_norm=mix_pre_norm, mix_post_norm=mix_post_norm, w_in=w_in,
               pool_maps=pool_maps, pool_scale=pool_scale, conf_dw_w=conf_dw_w, conf_dw_b=conf_dw_b,
               conf_ln_g=conf_ln_g, conf_ln_b=conf_ln_b, sconv_w=sconv_w, w_out=w_out,
               xattn_pre_norm=xattn_pre_norm, xattn_post_norm=xattn_post_norm, xattn_wq=xattn_wq,
               xattn_wk=xattn_wk, xattn_wv=xattn_wv, xattn_wo=xattn_wo, ffn_pre_norm=ffn_pre_norm,
               ffn_post_norm=ffn_post_norm, ffn_w_up=ffn_w_up, ffn_conv_w=ffn_conv_w, ffn_w_down=ffn_w_down)
    mom_m = dict(mem_norm=m_mem_norm, mix_pre_norm=m_mix_pre_norm, mix_post_norm=m_mix_post_norm, w_in=m_w_in,
                 pool_maps=m_pool_maps, pool_scale=m_pool_scale, conf_dw_w=m_conf_dw_w, conf_dw_b=m_conf_dw_b,
                 conf_ln_g=m_conf_ln_g, conf_ln_b=m_conf_ln_b, sconv_w=m_sconv_w, w_out=m_w_out,
                 xattn_pre_norm=m_xattn_pre_norm, xattn_post_norm=m_xattn_post_norm, xattn_wq=m_xattn_wq,
                 xattn_wk=m_xattn_wk, xattn_wv=m_xattn_wv, xattn_wo=m_xattn_wo, ffn_pre_norm=m_ffn_pre_norm,
                 ffn_post_norm=m_ffn_post_norm, ffn_w_up=m_ffn_w_up, ffn_conv_w=m_ffn_conv_w,
                 ffn_w_down=m_ffn_w_down)
    mom_v = dict(mem_norm=v_mem_norm, mix_pre_norm=v_mix_pre_norm, mix_post_norm=v_mix_post_norm, w_in=v_w_in,
                 pool_maps=v_pool_maps, pool_scale=v_pool_scale, conf_dw_w=v_conf_dw_w, conf_dw_b=v_conf_dw_b,
                 conf_ln_g=v_conf_ln_g, conf_ln_b=v_conf_ln_b, sconv_w=v_sconv_w, w_out=v_w_out,
                 xattn_pre_norm=v_xattn_pre_norm, xattn_post_norm=v_xattn_post_norm, xattn_wq=v_xattn_wq,
                 xattn_wk=v_xattn_wk, xattn_wv=v_xattn_wv, xattn_wo=v_xattn_wo, ffn_pre_norm=v_ffn_pre_norm,
                 ffn_post_norm=v_ffn_post_norm, ffn_w_up=v_ffn_w_up, ffn_conv_w=v_ffn_conv_w,
                 ffn_w_down=v_ffn_w_down)
    depth = w_in.shape[0]
    chip = 2 * lax.axis_index("x") + lax.axis_index("y")

    stacked = {}
    for name, (key, transposed) in _BIG.items():
        w = wts[name]
        wb = _cast_bf16(w.reshape(-1, w.shape[-1])).reshape(w.shape)
        stacked[key] = wb.transpose(0, 2, 1) if transposed else wb
    shards = [{key: stacked[key][l] for key in _KEYS} for l in range(depth)]
    first = _gather_two_level([shards[0][key] for key in _GATHER_EARLY])
    big0 = {key: g.reshape(-1, D) for key, g in zip(_GATHER_EARLY, first)}

    conv_shapes = [wts[n].shape for n in _CHANNEL_SHARDED]
    conv_all = _allgather_devices(_pack([wts[n] for n in _CHANNEL_SHARDED]))
    per_chip = [_unpack(conv_all[2 * j], conv_shapes) for j in range(N_CHIPS)]
    small = {n: wts[n] for n in _SMALL}
    for i, n in enumerate(_CHANNEL_SHARDED):
        small[n] = jnp.concatenate([per_chip[j][i] for j in range(N_CHIPS)], axis=-1)

    loss_blk, dx, landings, small_grads = _local_step(x[0], mem[0], loss_target[0], small, [big0], shards)

    core = lax.axis_index("c")
    grads, delta, new_m, new_v = {}, {}, {}, {}

    def finish_grads(bufs, order):
        paired = _pair_halves([_sum_blocks(buf, core) for buf in bufs])
        off = 0
        for key in order:
            name = next(n for n, (k, _) in _BIG.items() if k == key)
            h = stacked[key].shape[1] // 2
            g = jnp.stack([p[:, off:off + h].reshape(2 * h, D) for p in paired])
            grads[name] = g.transpose(0, 2, 1) if _BIG[name][1] else g
            off += h

    def adam_step(name, comm=None):
        shp = wts[name].shape
        flat = lambda a: a.reshape(-1, shp[-1])
        res = _adamw(flat(wts[name]), flat(grads[name]), flat(mom_m[name]), flat(mom_v[name]), comm)
        delta[name], new_m[name], new_v[name] = [a.reshape(shp) for a in res[:3]]
        return res[3:]

    finish_grads([pair[0] for pair in landings], _SCATTER_LATE)
    first_landing = adam_step("ffn_w_up", _Scatter(landings[0][1]))[0]
    finish_grads([first_landing] + [pair[1] for pair in landings[1:]], _SCATTER_NEXT)
    for name in _BIG:
        if name != "ffn_w_up":
            adam_step(name)

    small_shapes = [(128,)] + [small[n].shape for n in _SMALL]
    partial = _pack([loss_blk[0]] + [small_grads[n] for n in _SMALL])
    total = _unpack(_sum_devices(_allgather_devices(partial)), small_shapes)
    loss = total[0][0]
    for n, g in zip(_SMALL, total[1:]):
        if n in _CHANNEL_SHARDED:
            width = wts[n].shape[-1]
            g = lax.dynamic_slice_in_dim(g, chip * width, width, axis=-1)
        grads[n] = g

    shapes = [wts[n].shape for n in _SMALL]
    d, nm, nv = _adamw(_pack([wts[n] for n in _SMALL]), _pack([grads[n] for n in _SMALL]),
                       _pack([mom_m[n] for n in _SMALL]), _pack([mom_v[n] for n in _SMALL]))
    for out, packed in ((delta, d), (new_m, nm), (new_v, nv)):
        for n, a in zip(_SMALL, _unpack(packed, shapes)):
            out[n] = a

    return (loss, dx[None], *[grads[n] for n in _WEIGHTS], *[delta[n] for n in _WEIGHTS],
            *[new_m[n] for n in _WEIGHTS], *[new_v[n] for n in _WEIGHTS])
```

```python
import jax
import jax.numpy as jnp
from jax import lax
from jax.experimental import pallas as pl
from jax.experimental.pallas import tpu as pltpu

F32 = jnp.float32
BF16 = jnp.bfloat16

EPS = 1e-6
D = 1024
D_POOL, D_CONF, D_SCONV = 256, 384, 384
D_IN = D_POOL + 2 * D_CONF + 3 * D_SCONV
D_FF = 2816
FF_CHUNK = 1408
HEADS, HEAD_DIM = 4, 256
CONF_K = 31
POOL_WINDOWS = (2, 4, 8, 16)
POOL_GROUP = 64
SUBLANES = 8
HALO = 32
PHASE_ROWS = HALO - SUBLANES
TAP_ROWS = 64
FHALO = 16
TM_FWD = 512
TM_BWD = 256
N_CHIPS = 4
N_DEV = 8
MESH_ID = pl.DeviceIdType.MESH
VMEM_LIMIT = 56 << 20

ADAM_LR, ADAM_B1, ADAM_B2, ADAM_EPS, ADAM_WD, ADAM_STEP = 0.001, 0.9, 0.999, 1e-08, 0.01, 10

C_P = (0, 256)
C_A = (256, 640)
C_G = (640, 1024)
C_B = (1024, 1408)
C_C = (1408, 1792)
C_X = (1792, 2176)


def _nn(a, b):
    return jnp.dot(a, b, preferred_element_type=F32)


def _nt(a, b):
    return lax.dot_general(a, b, (((1,), (1,)), ((), ())), preferred_element_type=F32)


def _tn(a, b):
    return lax.dot_general(a, b, (((0,), (0,)), ((), ())), preferred_element_type=F32)


def _sigmoid(v):
    return 1.0 / (1.0 + jnp.exp(-v))


def _rms(v, g):
    r = lax.rsqrt(jnp.mean(v * v, axis=-1, keepdims=True) + EPS)
    vh = v * r
    return vh * g, vh, r


def _rms_bwd(vh, r, g, dy):
    dvh = dy * g
    dv = r * (dvh - vh * jnp.mean(dvh * vh, axis=-1, keepdims=True))
    return dv, jnp.sum(dy * vh, axis=0, keepdims=True)


def _colsum(v):
    return jnp.sum(v, axis=0, keepdims=True)


def _rows(tm, n, nt=None):
    if nt is None:
        return pl.BlockSpec((tm, n), lambda i: (i, 0))
    return pl.BlockSpec((tm, n), lambda i: (nt - 1 - i, 0))


def _const(shape):
    nd = len(shape)
    return pl.BlockSpec(shape, lambda i: (0,) * nd, pipeline_mode=pl.Buffered(1))


def _acc(shape):
    nd = len(shape)
    return pl.BlockSpec(shape, lambda i: (0,) * nd)


def _sds(shape, dtype):
    return jax.ShapeDtypeStruct(shape, dtype)


_HBM = pl.BlockSpec(memory_space=pltpu.HBM)


def _mesh_pos():
    return lax.axis_index("x"), lax.axis_index("y"), lax.axis_index("c")


def _chip_peers():
    x, y, c = _mesh_pos()
    flips = [(1 - x, y), (x, 1 - y), (1 - x, 1 - y)]
    return 2 * x + y, [((px, py, c), 2 * px + py) for px, py in flips]


def _remote(src, dst, send_sems, recv_sems, idx, dev):
    return pltpu.make_async_remote_copy(src_ref=src, dst_ref=dst, send_sem=send_sems.at[idx],
                                        recv_sem=recv_sems.at[idx], device_id=dev, device_id_type=MESH_ID)


class _Plan:
    def __init__(self, arrays):
        self.arrays = list(arrays)

    def scratch(self):
        n = len(self.arrays) * self.n_peers
        return [pltpu.SemaphoreType.DMA((n,)), pltpu.SemaphoreType.DMA((n,)),
                pltpu.SemaphoreType.DMA((len(self.arrays),))]

    def _copies(self, ins, outs, sems):
        send_sems, recv_sems, local_sems = sems
        me, peers = self.peers()
        own, sends, recvs = [], [], []
        for k in range(len(ins)):
            own.append(pltpu.make_async_copy(self.src(ins, k, me), self.dst(outs, k, me), local_sems.at[k]))
            for j, (dev, who) in enumerate(peers):
                idx = self.n_peers * k + j
                sends.append(_remote(self.src(ins, k, who), self.dst(outs, k, me), send_sems, recv_sems, idx, dev))
                recvs.append(_remote(self.src(ins, k, me), self.dst(outs, k, who), send_sems, recv_sems, idx, dev))
        return own, sends, recvs

    def start(self, ins, outs, sems):
        own, sends, _ = self._copies(ins, outs, sems)
        for cp in own + sends:
            cp.start()

    def wait(self, ins, outs, sems):
        own, sends, recvs = self._copies(ins, outs, sems)
        for cp in recvs:
            cp.wait_recv()
        for cp in sends:
            cp.wait_send()
        for cp in own:
            cp.wait()


class _Gather(_Plan):
    tag = "gather"
    n_peers = N_CHIPS - 1

    def peers(self):
        return _chip_peers()

    def out_shapes(self):
        return [_sds((N_CHIPS,) + a.shape, a.dtype) for a in self.arrays]

    def src(self, ins, k, chip):
        return ins[k]

    def dst(self, outs, k, chip):
        return outs[k].at[chip]


class _Scatter(_Plan):
    tag = "scatter"
    n_peers = N_DEV - 1

    def __init__(self, arrays):
        super().__init__(arrays)
        self.rows = [a.shape[0] // N_DEV for a in self.arrays]
        self.offs = [sum(self.rows[:k]) for k in range(len(self.rows))]

    def peers(self):
        x, y, c = _mesh_pos()
        flip = lambda v, on: 1 - v if on else v
        others = [(flip(x, m & 4), flip(y, m & 2), flip(c, m & 1)) for m in range(1, N_DEV)]
        return 4 * x + 2 * y + c, [(dev, 4 * dev[0] + 2 * dev[1] + dev[2]) for dev in others]

    def out_shapes(self):
        a = self.arrays[0]
        return [_sds((N_DEV, sum(self.rows), a.shape[1]), a.dtype)]

    def src(self, ins, k, dev):
        r = self.rows[k]
        return ins[k].at[pl.ds(pl.multiple_of(dev * r, 16), r), :]

    def dst(self, outs, k, dev):
        return outs[0].at[dev, pl.ds(self.offs[k], self.rows[k]), :]


def _fused_call(name, body, grid, in_specs, out_specs, out_shape, scratch, args, comm=None, sem=("arbitrary",)):
    n_in, n_out, n_scr = len(in_specs), len(out_specs), len(scratch)
    c_in = comm.arrays if comm else []
    c_out = comm.out_shapes() if comm else []
    c_scr = comm.scratch() if comm else []

    def kernel_fn(*refs):
        ins, cins = refs[:n_in], refs[n_in:n_in + len(c_in)]
        o0 = n_in + len(c_in)
        outs, couts = refs[o0:o0 + n_out], refs[o0 + n_out:o0 + n_out + len(c_out)]
        s0 = o0 + n_out + len(c_out)
        scr, csems = refs[s0:s0 + n_scr], refs[s0 + n_scr:]
        if comm:
            body(ins, outs, scr, lambda: comm.start(cins, couts, csems), lambda: comm.wait(cins, couts, csems))
        else:
            body(ins, outs, scr, None, None)

    res = pl.pallas_call(
        kernel_fn, grid=grid, name=name + ("_" + comm.tag if comm else ""),
        in_specs=list(in_specs) + [_HBM] * len(c_in), out_specs=list(out_specs) + [_HBM] * len(c_out),
        out_shape=list(out_shape) + c_out, scratch_shapes=list(scratch) + c_scr,
        compiler_params=pltpu.CompilerParams(dimension_semantics=sem, vmem_limit_bytes=VMEM_LIMIT),
    )(*args, *c_in)
    return res[:n_out], res[n_out:]


def _bracket(start, wait, first, last):
    if start is not None:
        pl.when(first)(start)

    def finish():
        if wait is not None:
            pl.when(last)(wait)
    return finish


def _gather_two_level(shards):
    n = len(shards)
    per = 2 * N_CHIPS - 1

    def body(*refs):
        ins, outs = refs[:n], refs[n:2 * n]
        send_sems, recv_sems = refs[2 * n:]
        x, y, c = _mesh_pos()
        me, peers = _chip_peers()
        sibling = (x, y, 1 - c)

        def half(ref, k, core):
            h = shards[k].shape[0] // 2
            return ref.at[pl.ds(pl.multiple_of(core * h, 16), h), :]

        first, passed = [], []
        for k in range(n):
            first.append(_remote(ins[k], outs[k].at[me], send_sems, recv_sems, per * k + 3, sibling))
            for j, (dev, chip) in enumerate(peers):
                first.append(_remote(half(ins[k], k, c), half(outs[k].at[me], k, c), send_sems, recv_sems,
                                     per * k + j, dev))
        for cp in first:
            cp.start()
        for k in range(n):
            for j, (dev, chip) in enumerate(peers):
                landed = half(outs[k].at[chip], k, c)
                _remote(landed, landed, send_sems, recv_sems, per * k + j, dev).wait_recv()
                fwd = _remote(landed, landed, send_sems, recv_sems, per * k + 4 + j, sibling)
                fwd.start()
                passed.append(fwd)
        for k in range(n):
            _remote(ins[k], outs[k].at[me], send_sems, recv_sems, per * k + 3, sibling).wait_recv()
            for j, (dev, chip) in enumerate(peers):
                other = half(outs[k].at[chip], k, 1 - c)
                _remote(other, other, send_sems, recv_sems, per * k + 4 + j, sibling).wait_recv()
        for cp in first + passed:
            cp.wait_send()

    return pl.pallas_call(
        body, name="gather_two_level", in_specs=[_HBM] * n, out_specs=[_HBM] * n,
        out_shape=[_sds((N_CHIPS,) + s.shape, s.dtype) for s in shards],
        scratch_shapes=[pltpu.SemaphoreType.DMA((per * n,)), pltpu.SemaphoreType.DMA((per * n,))],
    )(*shards)


def _pair_halves(bufs):
    n = len(bufs)

    def body(*refs):
        outs = refs[n:2 * n]
        send_sems, recv_sems = refs[2 * n:]
        x, y, c = _mesh_pos()
        sibling = (x, y, 1 - c)
        sends = [_remote(outs[k].at[c], outs[k].at[c], send_sems, recv_sems, k, sibling) for k in range(n)]
        for cp in sends:
            cp.start()
        for k in range(n):
            _remote(outs[k].at[c], outs[k].at[1 - c], send_sems, recv_sems, k, sibling).wait_recv()
        for cp in sends:
            cp.wait_send()

    return pl.pallas_call(
        body, name="pair_halves", in_specs=[_HBM] * n, out_specs=[_HBM] * n,
        out_shape=[_sds(a.shape, a.dtype) for a in bufs], input_output_aliases={k: k for k in range(n)},
        scratch_shapes=[pltpu.SemaphoreType.DMA((n,)), pltpu.SemaphoreType.DMA((n,))],
    )(*bufs)


def _allgather_devices(v):
    m_per, n = v.shape

    def body(v_ref, out_ref, send_sems, recv_sems, local_sem):
        x, y, c = _mesh_pos()
        me, sibling = (x, y, c), (x, y, 1 - c)
        chips = [(1 - x, y), (x, 1 - y), (1 - x, 1 - y)]

        def rows(px, py, pc):
            return out_ref.at[4 * px + 2 * py + pc]

        def copy(k, block, to, src=None):
            return _remote(rows(*block) if src is None else src, rows(*block), send_sems, recv_sems, k, to)

        mine = pltpu.make_async_copy(v_ref, rows(*me), local_sem)
        mine.start()
        first = [copy(0, me, sibling, src=v_ref)]
        first += [copy(1 + j, me, (*chip, c), src=v_ref) for j, chip in enumerate(chips)]
        for cp in first:
            cp.start()
        passed = [copy(4 + j, (*chip, c), sibling) for j, chip in enumerate(chips)]
        for j, chip in enumerate(chips):
            copy(1 + j, (*chip, c), me).wait_recv()
            passed[j].start()
        copy(0, sibling, me).wait_recv()
        for j, chip in enumerate(chips):
            copy(4 + j, (*chip, 1 - c), me).wait_recv()
        for cp in first + passed:
            cp.wait_send()
        mine.wait()

    return pl.pallas_call(
        body, name="allgather_devices", out_shape=_sds((N_DEV, m_per, n), v.dtype),
        in_specs=[pl.BlockSpec(memory_space=pltpu.VMEM)], out_specs=pl.BlockSpec(memory_space=pltpu.VMEM),
        scratch_shapes=[pltpu.SemaphoreType.DMA((7,)), pltpu.SemaphoreType.DMA((7,)), pltpu.SemaphoreType.DMA],
        compiler_params=pltpu.CompilerParams(vmem_limit_bytes=VMEM_LIMIT),
    )(v)


def _pool_lane():
    return lax.broadcasted_iota(jnp.int32, (1, D_POOL), 1)


def _pool_count(t0, tm):
    lane = _pool_lane()
    w = jnp.where(lane < 64, 2, jnp.where(lane < 128, 4, jnp.where(lane < 192, 8, 16)))
    pos1 = lax.broadcasted_iota(jnp.int32, (tm, D_POOL), 0) + (t0 + 1)
    return jnp.minimum(pos1, w).astype(F32)


def _fill_bands(band_ref, tm, causal):
    r = lax.broadcasted_iota(jnp.int32, (tm, tm + HALO), 0)
    s = lax.broadcasted_iota(jnp.int32, (tm, tm + HALO), 1)
    d = (r + HALO - s) if causal else (s - r)
    for g, w in enumerate(POOL_WINDOWS):
        band_ref[g] = jnp.where((d >= 0) & (d < w), 1.0, 0.0).astype(BF16)


def _window_sums(band_ref, operand, width):
    lane = _pool_lane()
    res = None
    for g in range(len(POOL_WINDOWS)):
        r = _nn(band_ref[g], operand)
        acc = r[:, 0:width]
        for c0 in range(width, r.shape[1], width):
            acc = acc + r[:, c0:c0 + width]
        res = acc if res is None else jnp.where(lane >= POOL_GROUP * g, acc, res)
    return res


def _phase_copies(src, phases, tm):
    for b in range(1, SUBLANES):
        phases[b - 1] = src[b:b + tm + PHASE_ROWS, :]


def _tap(src, phases, off, rows, r0=0):
    a, b = divmod(off, SUBLANES)
    lo = SUBLANES * a + r0
    if b == 0:
        return src[lo:lo + rows, :]
    return phases[b - 1, lo:lo + rows, :]


def _layer_norm_stats(v1):
    mu = jnp.mean(v1, axis=-1, keepdims=True)
    xc = v1 - mu
    rs = lax.rsqrt(jnp.mean(xc * xc, axis=-1, keepdims=True) + EPS)
    return xc * rs, rs


def _mix_fwd(x, g1, wint, maps_bd, scale, dww, dwb, lng, lnb, sw, wout, g2, comm=None):
    t_len = x.shape[0]
    tm = min(TM_FWD, t_len)
    nt = t_len // tm
    h0 = HALO

    def body(ins, outs, scr, start, wait):
        (x_ref, g1_ref, wint_ref, maps_ref, scale_ref, dww_ref, dwb_ref, lng_ref, lnb_ref, sw_ref, wout_ref,
         g2_ref) = ins
        z_ref, ycat_ref, y_ref, x1_ref, v1_ref, cv_ref, pooled_ref = outs
        pbuf, vbuf, sbuf, phases, band = scr
        i = pl.program_id(0)
        finish = _bracket(start, wait, i == 0, i == nt - 1)

        @pl.when(i == 0)
        def _():
            pbuf[0:h0, :] = jnp.zeros((h0, D_POOL), F32)
            vbuf[0:h0, :] = jnp.zeros((h0, D_CONF), F32)
            sbuf[0:h0, :] = jnp.zeros((h0, D_SCONV), F32)
            _fill_bands(band, tm, True)

        @pl.when(i > 0)
        def _():
            pbuf[0:h0, :] = pbuf[tm:tm + h0, :]
            vbuf[0:h0, :] = vbuf[tm:tm + h0, :]
            sbuf[0:h0, :] = sbuf[tm:tm + h0, :]

        xv = x_ref[...]
        h, _, _ = _rms(xv, g1_ref[...])
        z = _nt(h.astype(BF16), wint_ref[...])
        z_ref[...] = z.astype(BF16)
        zp = z[:, C_P[0]:C_P[1]]
        pbuf[h0:h0 + tm, :] = zp
        vbuf[h0:h0 + tm, :] = z[:, C_A[0]:C_A[1]] * _sigmoid(z[:, C_G[0]:C_G[1]])
        sbuf[h0:h0 + tm, :] = z[:, C_C[0]:C_C[1]] * z[:, C_X[0]:C_X[1]]

        pv = pbuf[...]
        hi = pv.astype(BF16)
        lo = (pv - hi.astype(F32)).astype(BF16)
        sums = _window_sums(band, jnp.concatenate([hi, lo], axis=1), D_POOL)
        pooled = (sums / _pool_count(i * tm, tm) - zp).astype(BF16)
        pooled_ref[...] = pooled
        ycat_ref[:, 0:D_POOL] = (_nn(pooled, maps_ref[...]) * scale_ref[...]).astype(BF16)

        _phase_copies(vbuf, phases, tm)
        base = h0 - (CONF_K - 1)
        for r0 in range(0, tm, TAP_ROWS):
            v1 = dww_ref[0:1, :] * _tap(vbuf, phases, base, TAP_ROWS, r0)
            for j in range(1, CONF_K):
                v1 = v1 + dww_ref[j:j + 1, :] * _tap(vbuf, phases, base + j, TAP_ROWS, r0)
            v1 = v1 + dwb_ref[...]
            v1_ref[r0:r0 + TAP_ROWS, :] = v1
            vh, _ = _layer_norm_stats(v1)
            v2 = vh * lng_ref[...] + lnb_ref[...]
            ycat_ref[r0:r0 + TAP_ROWS, D_POOL:D_POOL + D_CONF] = (v2 * _sigmoid(v2)).astype(BF16)

        cv = (sw_ref[0:1, :] * sbuf[h0 - 2:h0 - 2 + tm, :] + sw_ref[1:2, :] * sbuf[h0 - 1:h0 - 1 + tm, :]
              + sw_ref[2:3, :] * sbuf[h0:h0 + tm, :])
        cv_ref[...] = cv
        ycat_ref[:, D_POOL + D_CONF:D] = (z[:, C_B[0]:C_B[1]] * cv).astype(BF16)

        yb = _nn(ycat_ref[...], wout_ref[...]).astype(BF16)
        y_ref[...] = yb
        yn, _, _ = _rms(yb.astype(F32), g2_ref[...])
        x1_ref[...] = xv + yn
        finish()

    return _fused_call(
        "mix_fwd", body, (nt,),
        [_rows(tm, D), _const((1, D)), _const((D_IN, D)), _const((D_POOL, D_POOL)), _const((1, D_POOL)),
         _const((CONF_K, D_CONF)), _const((1, D_CONF)), _const((1, D_CONF)), _const((1, D_CONF)),
         _const((3, D_SCONV)), _const((D, D)), _const((1, D))],
        [_rows(tm, D_IN), _rows(tm, D), _rows(tm, D), _rows(tm, D), _rows(tm, D_CONF), _rows(tm, D_SCONV),
         _rows(tm, D_POOL)],
        [_sds((t_len, D_IN), BF16), _sds((t_len, D), BF16), _sds((t_len, D), BF16), _sds((t_len, D), F32),
         _sds((t_len, D_CONF), F32), _sds((t_len, D_SCONV), F32), _sds((t_len, D_POOL), BF16)],
        [pltpu.VMEM((h0 + tm, D_POOL), F32), pltpu.VMEM((h0 + tm, D_CONF), F32),
         pltpu.VMEM((h0 + tm, D_SCONV), F32), pltpu.VMEM((SUBLANES - 1, tm + PHASE_ROWS, D_CONF), F32),
         pltpu.VMEM((len(POOL_WINDOWS), tm, tm + h0), BF16)],
        (x, g1, wint, maps_bd, scale, dww, dwb, lng, lnb, sw, wout, g2), comm)


def _mix_bwd(dx1, y, x, z, ycat, v1, cv, pooled, g1, wint, maps_bd, scale, dww, lng, lnb, sw, wout, g2, comm=None):
    t_len = x.shape[0]
    tm = min(TM_BWD, t_len)
    nt = t_len // tm
    h0 = HALO

    def body(ins, outs, scr, start, wait):
        (dx1_ref, y_ref, x_ref, z_ref, ycat_ref, v1_ref, cv_ref, pooled_ref, g1_ref, wint_ref, maps_ref, scale_ref,
         dww_ref, lng_ref, lnb_ref, sw_ref, wout_ref, g2_ref) = ins
        dx_ref, dwin_ref, dwout_ref, dg1_ref, dg2_ref, dmaps_ref, dscale_ref, ddww_ref, misc_ref = outs
        ebuf, dvbuf, dcbuf, phases, band, dzbuf, acc_in, acc_out, ddacc = scr
        s = pl.program_id(0)
        ti = nt - 1 - s
        finish = _bracket(start, wait, s == 0, s == nt - 1)

        @pl.when(s == 0)
        def _():
            ebuf[tm:tm + h0, :] = jnp.zeros((h0, D_POOL), F32)
            dvbuf[tm:tm + h0, :] = jnp.zeros((h0, D_CONF), F32)
            dcbuf[tm:tm + h0, :] = jnp.zeros((h0, D_SCONV), F32)
            _fill_bands(band, tm, False)
            acc_in[...] = jnp.zeros_like(acc_in)
            acc_out[...] = jnp.zeros_like(acc_out)
            ddacc[...] = jnp.zeros_like(ddacc)
            dg1_ref[...] = jnp.zeros_like(dg1_ref)
            dg2_ref[...] = jnp.zeros_like(dg2_ref)
            dmaps_ref[...] = jnp.zeros_like(dmaps_ref)
            dscale_ref[...] = jnp.zeros_like(dscale_ref)
            ddww_ref[...] = jnp.zeros_like(ddww_ref)
            misc_ref[...] = jnp.zeros_like(misc_ref)

        @pl.when(s > 0)
        def _():
            ebuf[tm:tm + h0, :] = ebuf[0:h0, :]
            dvbuf[tm:tm + h0, :] = dvbuf[0:h0, :]
            dcbuf[tm:tm + h0, :] = dcbuf[0:h0, :]

        yv = y_ref[...].astype(F32)
        _, yh, yr = _rms(yv, g2_ref[...])
        dy, dg2 = _rms_bwd(yh, yr, g2_ref[...], dx1_ref[...])
        dg2_ref[...] += dg2
        dyb = dy.astype(BF16)
        acc_out[...] += _tn(ycat_ref[...], dyb)
        dycat = _nt(dyb, wout_ref[...])
        dya = dycat[:, 0:D_POOL]
        dyb2 = dycat[:, D_POOL:D_POOL + D_CONF]
        dyc = dycat[:, D_POOL + D_CONF:D]

        pooled_v = pooled_ref[...]
        pm = _nn(pooled_v, maps_ref[...])
        dscale_ref[...] += _colsum(dya * pm)
        dq = (dya * scale_ref[...]).astype(BF16)
        dmaps_ref[...] += _tn(pooled_v, dq)
        dpooled = _nt(dq, maps_ref[...])
        ebuf[0:tm, :] = dpooled / _pool_count(ti * tm, tm)
        dzbuf[:, C_P[0]:C_P[1]] = (_window_sums(band, ebuf[...].astype(BF16), D_POOL) - dpooled).astype(BF16)

        vh, rs = _layer_norm_stats(v1_ref[...])
        v2 = vh * lng_ref[...] + lnb_ref[...]
        s2 = _sigmoid(v2)
        dv2 = dyb2 * (s2 * (1.0 + v2 * (1.0 - s2)))
        misc_ref[1:2, :] += _colsum(dv2 * vh)
        misc_ref[2:3, :] += _colsum(dv2)
        dvh = dv2 * lng_ref[...]
        dv1 = rs * (dvh - jnp.mean(dvh, axis=-1, keepdims=True) - vh * jnp.mean(dvh * vh, axis=-1, keepdims=True))
        misc_ref[0:1, :] += _colsum(dv1)
        dvbuf[0:tm, :] = dv1
        _phase_copies(dvbuf, phases, tm)
        for r0 in range(0, tm, TAP_ROWS):
            blk = slice(r0, r0 + TAP_ROWS)
            za = z_ref[blk, C_A[0]:C_A[1]].astype(F32)
            sg = _sigmoid(z_ref[blk, C_G[0]:C_G[1]].astype(F32))
            v0 = za * sg
            dv0 = None
            for k in range(CONF_K):
                j = CONF_K - 1 - k
                dk = _tap(dvbuf, phases, k, TAP_ROWS, r0)
                prod = v0 * dk
                part = prod[0:SUBLANES, :]
                for q in range(SUBLANES, TAP_ROWS, SUBLANES):
                    part = part + prod[q:q + SUBLANES, :]
                ddacc[SUBLANES * j:SUBLANES * (j + 1), :] += part
                term = dww_ref[j:j + 1, :] * dk
                dv0 = term if dv0 is None else dv0 + term
            dzbuf[blk, C_A[0]:C_A[1]] = (dv0 * sg).astype(BF16)
            dzbuf[blk, C_G[0]:C_G[1]] = (dv0 * za * sg * (1.0 - sg)).astype(BF16)

        zb = z_ref[:, C_B[0]:C_B[1]].astype(F32)
        zc = z_ref[:, C_C[0]:C_C[1]].astype(F32)
        zx = z_ref[:, C_X[0]:C_X[1]].astype(F32)
        pv = zc * zx
        dzbuf[:, C_B[0]:C_B[1]] = (dyc * cv_ref[...]).astype(BF16)
        dcbuf[0:tm, :] = dyc * zb
        dp = None
        for k in range(3):
            j = 2 - k
            dk = dcbuf[k:k + tm, :]
            misc_ref[3 + j:4 + j, :] += _colsum(pv * dk)
            term = sw_ref[j:j + 1, :] * dk
            dp = term if dp is None else dp + term
        dzbuf[:, C_C[0]:C_C[1]] = (dp * zx).astype(BF16)
        dzbuf[:, C_X[0]:C_X[1]] = (dp * zc).astype(BF16)

        xv = x_ref[...]
        h, xh, xr = _rms(xv, g1_ref[...])
        dz = dzbuf[...]
        acc_in[...] += _tn(dz, h.astype(BF16))
        dh = _nn(dz, wint_ref[...])
        dxn, dg1 = _rms_bwd(xh, xr, g1_ref[...], dh)
        dg1_ref[...] += dg1
        dx_ref[...] = dx1_ref[...] + dxn

        @pl.when(s == nt - 1)
        def _():
            dwin_ref[...] = acc_in[...].astype(BF16)
            dwout_ref[...] = acc_out[...].astype(BF16)
            for j in range(CONF_K):
                ddww_ref[j:j + 1, :] = _colsum(ddacc[SUBLANES * j:SUBLANES * (j + 1), :])

        finish()

    return _fused_call(
        "mix_bwd", body, (nt,),
        [_rows(tm, D, nt), _rows(tm, D, nt), _rows(tm, D, nt), _rows(tm, D_IN, nt), _rows(tm, D, nt),
         _rows(tm, D_CONF, nt), _rows(tm, D_SCONV, nt), _rows(tm, D_POOL, nt), _const((1, D)), _const((D_IN, D)),
         _const((D_POOL, D_POOL)), _const((1, D_POOL)), _const((CONF_K, D_CONF)), _const((1, D_CONF)),
         _const((1, D_CONF)), _const((3, D_SCONV)), _const((D, D)), _const((1, D))],
        [_rows(tm, D, nt), _acc((D_IN, D)), _acc((D, D)), _acc((1, D)), _acc((1, D)), _acc((D_POOL, D_POOL)),
         _acc((1, D_POOL)), _acc((32, D_CONF)), _acc((8, D_CONF))],
        [_sds((t_len, D), F32), _sds((D_IN, D), BF16), _sds((D, D), BF16), _sds((1, D), F32), _sds((1, D), F32),
         _sds((D_POOL, D_POOL), F32), _sds((1, D_POOL), F32), _sds((32, D_CONF), F32), _sds((8, D_CONF), F32)],
        [pltpu.VMEM((tm + h0, D_POOL), F32), pltpu.VMEM((tm + h0, D_CONF), F32),
         pltpu.VMEM((tm + h0, D_SCONV), F32), pltpu.VMEM((SUBLANES - 1, tm + PHASE_ROWS, D_CONF), F32),
         pltpu.VMEM((len(POOL_WINDOWS), tm, tm + h0), BF16), pltpu.VMEM((tm, D_IN), BF16),
         pltpu.VMEM((D_IN, D), F32), pltpu.VMEM((D, D), F32), pltpu.VMEM((SUBLANES * CONF_K, D_CONF), F32)],
        (dx1, y, x, z, ycat, v1, cv, pooled, g1, wint, maps_bd, scale, dww, lng, lnb, sw, wout, g2), comm)


def _kv_fwd(mem, gmem, wk, wv):
    def body(mem_ref, g_ref, wk_ref, wv_ref, k_ref, v_ref):
        mn, _, _ = _rms(mem_ref[...], g_ref[...])
        mnb = mn.astype(BF16)
        k_ref[...] = _nn(mnb, wk_ref[...]).astype(BF16)
        v_ref[...] = _nn(mnb, wv_ref[...]).astype(BF16)

    n = mem.shape[0]
    return pl.pallas_call(
        body, name="kv_fwd", out_shape=[_sds((n, D), BF16), _sds((n, D), BF16)],
        compiler_params=pltpu.CompilerParams(vmem_limit_bytes=VMEM_LIMIT),
    )(mem, gmem, wk, wv)


def _kv_bwd(mem, gmem, dk, dv, wk, wv):
    def body(mem_ref, g_ref, dk_ref, dv_ref, wk_ref, wv_ref, dwk_ref, dwv_ref, dg_ref):
        mn, mh, _ = _rms(mem_ref[...], g_ref[...])
        mnb = mn.astype(BF16)
        dkb = dk_ref[...].astype(BF16)
        dvb = dv_ref[...].astype(BF16)
        dwk_ref[...] = _tn(mnb, dkb).astype(BF16)
        dwv_ref[...] = _tn(mnb, dvb).astype(BF16)
        dmn = _nt(dkb, wk_ref[...]) + _nt(dvb, wv_ref[...])
        dg_ref[...] = _colsum(dmn * mh)

    return pl.pallas_call(
        body, name="kv_bwd", out_shape=[_sds((D, D), BF16), _sds((D, D), BF16), _sds((1, D), F32)],
        compiler_params=pltpu.CompilerParams(vmem_limit_bytes=VMEM_LIMIT),
    )(mem, gmem, dk, dv, wk, wv)


def _softmax_rows(s):
    e = jnp.exp(s - jnp.max(s, axis=-1, keepdims=True))
    return e / jnp.sum(e, axis=-1, keepdims=True)


def _xattn_fwd(x1, g3, wq, k, v, wo, g4, comm=None):
    t_len = x1.shape[0]
    tm = min(2 * TM_FWD, t_len)
    nt = t_len // tm
    n_mem = k.shape[0]
    sc = HEAD_DIM ** -0.5

    def body(ins, outs, scr, start, wait):
        x_ref, g3_ref, wq_ref, k_ref, v_ref, wo_ref, g4_ref = ins
        q_ref, p_ref, o_ref, y_ref, x2_ref = outs
        i = pl.program_id(0)
        finish = _bracket(start, wait, i == 0, i == nt - 1)
        xv = x_ref[...]
        h, _, _ = _rms(xv, g3_ref[...])
        qb = _nn(h.astype(BF16), wq_ref[...]).astype(BF16)
        q_ref[...] = qb
        for hd in range(HEADS):
            sl = slice(hd * HEAD_DIM, (hd + 1) * HEAD_DIM)
            pb = _softmax_rows(_nt(qb[:, sl], k_ref[:, sl]) * sc).astype(BF16)
            p_ref[:, hd * n_mem:(hd + 1) * n_mem] = pb
            o_ref[:, sl] = _nn(pb, v_ref[:, sl]).astype(BF16)
        yb = _nn(o_ref[...], wo_ref[...]).astype(BF16)
        y_ref[...] = yb
        yn, _, _ = _rms(yb.astype(F32), g4_ref[...])
        x2_ref[...] = xv + yn
        finish()

    return _fused_call(
        "xattn_fwd", body, (nt,),
        [_rows(tm, D), _const((1, D)), _const((D, D)), _const((n_mem, D)), _const((n_mem, D)), _const((D, D)),
         _const((1, D))],
        [_rows(tm, D), _rows(tm, HEADS * n_mem), _rows(tm, D), _rows(tm, D), _rows(tm, D)],
        [_sds((t_len, D), BF16), _sds((t_len, HEADS * n_mem), BF16), _sds((t_len, D), BF16), _sds((t_len, D), BF16),
         _sds((t_len, D), F32)],
        [], (x1, g3, wq, k, v, wo, g4), comm)


def _xattn_bwd(dx2, y, x1, q, p, o, g3, wq, k, v, wo, g4):
    t_len = x1.shape[0]
    tm = min(TM_FWD, t_len)
    nt = t_len // tm
    n_mem = k.shape[0]
    sc = HEAD_DIM ** -0.5

    def body(ins, outs, scr, start, wait):
        dx2_ref, y_ref, x_ref, q_ref, p_ref, o_ref, g3_ref, wq_ref, k_ref, v_ref, wo_ref, g4_ref = ins
        dx_ref, dwq_ref, dwo_ref, dk_ref, dv_ref, dg3_ref, dg4_ref = outs
        dqbuf, acc_q, acc_o = scr
        s = pl.program_id(0)

        @pl.when(s == 0)
        def _():
            acc_q[...] = jnp.zeros_like(acc_q)
            acc_o[...] = jnp.zeros_like(acc_o)
            dk_ref[...] = jnp.zeros_like(dk_ref)
            dv_ref[...] = jnp.zeros_like(dv_ref)
            dg3_ref[...] = jnp.zeros_like(dg3_ref)
            dg4_ref[...] = jnp.zeros_like(dg4_ref)

        yv = y_ref[...].astype(F32)
        _, yh, yr = _rms(yv, g4_ref[...])
        dy, dg4 = _rms_bwd(yh, yr, g4_ref[...], dx2_ref[...])
        dg4_ref[...] += dg4
        dyb = dy.astype(BF16)
        acc_o[...] += _tn(o_ref[...], dyb)
        do = _nt(dyb, wo_ref[...])
        qb = q_ref[...]
        for hd in range(HEADS):
            sl = slice(hd * HEAD_DIM, (hd + 1) * HEAD_DIM)
            pb = p_ref[:, hd * n_mem:(hd + 1) * n_mem]
            p = pb.astype(F32)
            dob = do[:, sl].astype(BF16)
            dp = _nt(dob, v_ref[:, sl])
            dv_ref[:, sl] += _tn(pb, dob)
            ds = (p * (dp - jnp.sum(dp * p, axis=-1, keepdims=True)) * sc).astype(BF16)
            dqbuf[:, sl] = _nn(ds, k_ref[:, sl]).astype(BF16)
            dk_ref[:, sl] += _tn(ds, qb[:, sl])
        xv = x_ref[...]
        h, xh, xr = _rms(xv, g3_ref[...])
        dq = dqbuf[...]
        acc_q[...] += _tn(h.astype(BF16), dq)
        dh = _nt(dq, wq_ref[...])
        dxn, dg3 = _rms_bwd(xh, xr, g3_ref[...], dh)
        dg3_ref[...] += dg3
        dx_ref[...] = dx2_ref[...] + dxn

        @pl.when(s == nt - 1)
        def _():
            dwq_ref[...] = acc_q[...].astype(BF16)
            dwo_ref[...] = acc_o[...].astype(BF16)

    outs, _ = _fused_call(
        "xattn_bwd", body, (nt,),
        [_rows(tm, D), _rows(tm, D), _rows(tm, D), _rows(tm, D), _rows(tm, HEADS * n_mem), _rows(tm, D), _const((1, D)),
         _const((D, D)), _const((n_mem, D)), _const((n_mem, D)), _const((D, D)), _const((1, D))],
        [_rows(tm, D), _acc((D, D)), _acc((D, D)), _acc((n_mem, D)), _acc((n_mem, D)), _acc((1, D)), _acc((1, D))],
        [_sds((t_len, D), F32), _sds((D, D), BF16), _sds((D, D), BF16), _sds((n_mem, D), F32),
         _sds((n_mem, D), F32), _sds((1, D), F32), _sds((1, D), F32)],
        [pltpu.VMEM((tm, D), BF16), pltpu.VMEM((D, D), F32), pltpu.VMEM((D, D), F32)],
        (dx2, y, x1, q, p, o, g3, wq, k, v, wo, g4))
    return outs


def _ffn_cols(half, part):
    c0 = part * D_FF + half * FF_CHUNK
    return c0, c0 + FF_CHUNK


def _fill_shifts(ref, tm, step):
    t = lax.broadcasted_iota(jnp.int32, (tm, tm), 0)
    s = lax.broadcasted_iota(jnp.int32, (tm, tm), 1)
    ref[0:tm, :] = jnp.where(s == t + step, 1.0, 0.0).astype(BF16)
    ref[tm:2 * tm, :] = jnp.where(s == t + 2 * step, 1.0, 0.0).astype(BF16)


def _edge_terms(near, far, edge, inner):
    r = lax.broadcasted_iota(jnp.int32, (16, near.shape[1]), 0)
    return jnp.where(r == edge, near, 0.0), jnp.where(r == edge, far, jnp.where(r == inner, near, 0.0))


def _ffn_fwd(x2, g5, wupt, wc, wdown, g6, comm=None, target=None):
    t_len = x2.shape[0]
    tm = min(TM_BWD, t_len)
    nt = t_len // tm

    def body(ins, outs, scr, start, wait):
        x_ref, g5_ref, wupt_ref, wc_ref, wdown_ref, g6_ref = ins[:6]
        u_ref, c_ref, a_ref, y_ref, x3_ref = outs[:5]
        carry, shift = scr
        i = pl.program_id(0)
        finish = _bracket(start, wait, i == 0, i == nt - 1)

        @pl.when(i == 0)
        def _():
            carry[...] = jnp.zeros_like(carry)
            _fill_shifts(shift, tm, -1)
            if target is not None:
                outs[5][...] = jnp.zeros_like(outs[5])

        xv = x_ref[...]
        h, _, _ = _rms(xv, g5_ref[...])
        hb = h.astype(BF16)
        for half in range(2):
            conv = []
            for part in range(2):
                c0, c1 = _ffn_cols(half, part)
                w0, w1, w2 = wc_ref[0:1, c0:c1], wc_ref[1:2, c0:c1], wc_ref[2:3, c0:c1]
                u = _nt(hb, wupt_ref[c0:c1, :])
                ub = u.astype(BF16)
                u_ref[:, c0:c1] = ub
                sh = _nn(shift[...], ub)
                cb = w0 * sh[tm:2 * tm, :] + w1 * sh[0:tm, :] + w2 * u
                c_ref[:, c0:c1] = cb.astype(BF16)
                m1, m2 = _edge_terms(carry[SUBLANES - 1:SUBLANES, c0:c1], carry[SUBLANES - 2:SUBLANES - 1, c0:c1], 0, 1)
                c_ref[0:16, c0:c1] = (cb[0:16, :] + w1 * m1 + w0 * m2).astype(BF16)
                carry[:, c0:c1] = u[tm - SUBLANES:tm, :].astype(BF16).astype(F32)
                conv.append(c_ref[:, c0:c1].astype(F32))
            a = (conv[0] * _sigmoid(conv[0]) * conv[1]).astype(BF16)
            a_ref[:, half * FF_CHUNK:(half + 1) * FF_CHUNK] = a
        yb = _nn(a_ref[...], wdown_ref[...]).astype(BF16)
        y_ref[...] = yb
        yn, _, _ = _rms(yb.astype(F32), g6_ref[...])
        if target is None:
            x3_ref[...] = xv + yn
        else:
            err = xv + yn - ins[6][...]
            x3_ref[...] = err * (1.0 / D)
            part = 0.5 * _colsum(jnp.mean(err * err, axis=-1, keepdims=True))
            outs[5][...] += jnp.broadcast_to(part, outs[5].shape)
        finish()

    last = target is not None
    return _fused_call(
        "ffn_fwd_loss" if last else "ffn_fwd", body, (nt,),
        [_rows(tm, D), _const((1, D)), _const((2 * D_FF, D)), _const((3, 2 * D_FF)), _const((D_FF, D)),
         _const((1, D))] + ([_rows(tm, D)] if last else []),
        [_rows(tm, 2 * D_FF), _rows(tm, 2 * D_FF), _rows(tm, D_FF), _rows(tm, D), _rows(tm, D)]
        + ([_acc((8, 128))] if last else []),
        [_sds((t_len, 2 * D_FF), BF16), _sds((t_len, 2 * D_FF), BF16), _sds((t_len, D_FF), BF16),
         _sds((t_len, D), BF16), _sds((t_len, D), F32)] + ([_sds((8, 128), F32)] if last else []),
        [pltpu.VMEM((SUBLANES, 2 * D_FF), F32), pltpu.VMEM((2 * tm, tm), BF16)],
        (x2, g5, wupt, wc, wdown, g6) + ((target,) if last else ()), comm)


def _ffn_bwd(dx3, y, x2, u, c, g5, wupt, wc, wdown, g6, comm=None):
    t_len = x2.shape[0]
    tm = min(TM_BWD, t_len)
    nt = t_len // tm

    def body(ins, outs, scr, start, wait):
        dx3_ref, y_ref, x_ref, u_ref, c_ref, g5_ref, wupt_ref, wc_ref, wdown_ref, g6_ref = ins
        dx_ref, du_ref, dyo_ref, h_ref, dg5_ref, dg6_ref, dwc_ref = outs
        carry, shift = scr
        s = pl.program_id(0)
        finish = _bracket(start, wait, s == 0, s == nt - 1)

        @pl.when(s == 0)
        def _():
            carry[...] = jnp.zeros_like(carry)
            _fill_shifts(shift, tm, 1)
            dg5_ref[...] = jnp.zeros_like(dg5_ref)
            dg6_ref[...] = jnp.zeros_like(dg6_ref)
            dwc_ref[...] = jnp.zeros_like(dwc_ref)

        yv = y_ref[...].astype(F32)
        _, yh, yr = _rms(yv, g6_ref[...])
        dy, dg6 = _rms_bwd(yh, yr, g6_ref[...], dx3_ref[...])
        dg6_ref[...] += dg6
        dyb = dy.astype(BF16)
        dyo_ref[...] = dyb
        xv = x_ref[...]
        h, xh, xr = _rms(xv, g5_ref[...])
        h_ref[...] = h.astype(BF16)

        for half in range(2):
            g0, g1 = _ffn_cols(half, 0)
            v0, v1 = _ffn_cols(half, 1)
            gt = c_ref[:, g0:g1].astype(F32)
            vl = c_ref[:, v0:v1].astype(F32)
            sg = _sigmoid(gt)
            sil = gt * sg
            da = _nt(dyb, wdown_ref[half * FF_CHUNK:(half + 1) * FF_CHUNK, :])
            dcs = (da * vl * (sg + sil * (1.0 - sg)), da * sil)
            for part in range(2):
                c0, c1 = _ffn_cols(half, part)
                w0, w1, w2 = wc_ref[0:1, c0:c1], wc_ref[1:2, c0:c1], wc_ref[2:3, c0:c1]
                dc = dcs[part]
                sh = _nn(shift[...], dc.astype(BF16))
                d1, d2 = sh[0:tm, :], sh[tm:2 * tm, :]
                m1, m2 = _edge_terms(carry[0:1, c0:c1], carry[1:2, c0:c1], 15, 14)
                carry[:, c0:c1] = dc[0:SUBLANES, :]
                uu = u_ref[:, c0:c1].astype(F32)
                ut = u_ref[tm - 16:tm, c0:c1].astype(F32)
                dwc_ref[2:3, c0:c1] += _colsum(uu * dc)
                dwc_ref[1:2, c0:c1] += _colsum(uu * d1) + _colsum(ut * m1)
                dwc_ref[0:1, c0:c1] += _colsum(uu * d2) + _colsum(ut * m2)
                du = w2 * dc + w1 * d1 + w0 * d2
                du_ref[:, c0:c1] = du.astype(BF16)
                du_ref[tm - 16:tm, c0:c1] = (du[tm - 16:tm, :] + w1 * m1 + w0 * m2).astype(BF16)
        dh = _nn(du_ref[...], wupt_ref[...])
        dxn, dg5 = _rms_bwd(xh, xr, g5_ref[...], dh)
        dg5_ref[...] += dg5
        dx_ref[...] = dx3_ref[...] + dxn
        finish()

    return _fused_call(
        "ffn_bwd", body, (nt,),
        [_rows(tm, D, nt), _rows(tm, D, nt), _rows(tm, D, nt), _rows(tm, 2 * D_FF, nt), _rows(tm, 2 * D_FF, nt),
         _const((1, D)), _const((2 * D_FF, D)), _const((3, 2 * D_FF)), _const((D_FF, D)), _const((1, D))],
        [_rows(tm, D, nt), _rows(tm, 2 * D_FF, nt), _rows(tm, D, nt), _rows(tm, D, nt), _acc((1, D)), _acc((1, D)),
         _acc((8, 2 * D_FF))],
        [_sds((t_len, D), F32), _sds((t_len, 2 * D_FF), BF16), _sds((t_len, D), BF16), _sds((t_len, D), BF16),
         _sds((1, D), F32), _sds((1, D), F32), _sds((8, 2 * D_FF), F32)],
        [pltpu.VMEM((SUBLANES, 2 * D_FF), F32), pltpu.VMEM((2 * tm, tm), BF16)],
        (dx3, y, x2, u, c, g5, wupt, wc, wdown, g6), comm)


def _tn_matmul(a, b):
    t_len, m = a.shape
    bm = FF_CHUNK
    bt = min(4 * TM_FWD, t_len)
    nt = t_len // bt

    def body(a_ref, b_ref, o_ref, acc):
        t = pl.program_id(1)

        @pl.when(t == 0)
        def _():
            acc[...] = jnp.zeros_like(acc)

        acc[...] += _tn(a_ref[...], b_ref[...])

        @pl.when(t == nt - 1)
        def _():
            o_ref[...] = acc[...].astype(BF16)

    return pl.pallas_call(
        body, grid=(m // bm, nt), name="tn_matmul",
        in_specs=[pl.BlockSpec((bt, bm), lambda i, t: (t, i)), pl.BlockSpec((bt, D), lambda i, t: (t, 0))],
        out_specs=pl.BlockSpec((bm, D), lambda i, t: (i, 0)),
        out_shape=_sds((m, D), BF16),
        scratch_shapes=[pltpu.VMEM((bm, D), F32)],
        compiler_params=pltpu.CompilerParams(dimension_semantics=("parallel", "arbitrary"),
                                             vmem_limit_bytes=VMEM_LIMIT),
    )(a, b)


BLOCK_BYTES = 2 << 20


def _row_block(rows, cols):
    limit = max(16, BLOCK_BYTES // (4 * cols))
    best = None
    for rb in range(16, min(rows, limit) + 1, 16):
        if rows % rb == 0:
            best = rb
    return best or rows


def _elementwise(name, fn, ins, out_dtypes, comm=None):
    rows, cols = ins[0].shape
    rb = _row_block(rows, cols)
    nt = rows // rb

    def body(in_refs, out_refs, scr, start, wait):
        i = pl.program_id(0)
        finish = _bracket(start, wait, i == 0, i == nt - 1)
        res = fn(*[r[...] for r in in_refs])
        for o_ref, r in zip(out_refs, res):
            o_ref[...] = r
        finish()

    spec = pl.BlockSpec((rb, cols), lambda i: (i, 0))
    outs, extra = _fused_call(
        name, body, (nt,), [spec] * len(ins), [spec] * len(out_dtypes),
        [_sds((rows, cols), dt) for dt in out_dtypes], [], tuple(ins), comm,
        sem=("arbitrary",) if comm else ("parallel",))
    return list(outs) + list(extra)


def _cast_bf16(w):
    return _elementwise("cast_bf16", lambda v: (v.astype(BF16),), [w], [BF16])[0]


def _adam_math(w, g, m, v):
    nm = ADAM_B1 * m + (1.0 - ADAM_B1) * g
    nv = ADAM_B2 * v + (1.0 - ADAM_B2) * (g * g)
    m_hat = nm / (1.0 - ADAM_B1 ** ADAM_STEP)
    v_hat = nv / (1.0 - ADAM_B2 ** ADAM_STEP)
    return -ADAM_LR * (m_hat / (jnp.sqrt(v_hat) + ADAM_EPS) + ADAM_WD * w), nm, nv


def _adamw(w, g, m, v, comm=None):
    return _elementwise("adamw", _adam_math, [w, g, m, v], [F32, F32, F32], comm)


def _sum_blocks(parts, slot):
    n, rows, cols = parts.shape
    rb = _row_block(rows, cols)

    def body(slot_ref, p_ref, o_ref):
        acc = p_ref[0].astype(F32)
        for j in range(1, n):
            acc = acc + p_ref[j].astype(F32)
        o_ref[...] = acc

    return pl.pallas_call(
        body, name="sum_blocks", out_shape=_sds((2, rows, cols), F32),
        grid_spec=pltpu.PrefetchScalarGridSpec(
            num_scalar_prefetch=1, grid=(rows // rb,),
            in_specs=[pl.BlockSpec((n, rb, cols), lambda i, s: (0, i, 0))],
            out_specs=pl.BlockSpec((None, rb, cols), lambda i, s: (s[0], i, 0))),
        compiler_params=pltpu.CompilerParams(dimension_semantics=("parallel",), vmem_limit_bytes=VMEM_LIMIT),
    )(slot.reshape(1).astype(jnp.int32), parts)


def _sum_devices(parts):
    n, rows, cols = parts.shape

    def body(p_ref, o_ref):
        acc = p_ref[0]
        for j in range(1, n):
            acc = acc + p_ref[j]
        o_ref[...] = acc

    return pl.pallas_call(
        body, name="sum_devices", out_shape=_sds((rows, cols), F32),
        compiler_params=pltpu.CompilerParams(vmem_limit_bytes=VMEM_LIMIT),
    )(parts)


def _pack(parts):
    flat = jnp.concatenate([p.reshape(-1) for p in parts])
    rows = -(-flat.shape[0] // 1024) * 8
    return jnp.pad(flat, (0, rows * 128 - flat.shape[0])).reshape(rows, 128)


def _unpack(packed, shapes):
    flat = packed.reshape(-1)
    out, off = [], 0
    for shp in shapes:
        size = 1
        for d in shp:
            size *= d
        out.append(flat[off:off + size].reshape(shp))
        off += size
    return out


_BIG = {'w_in': ('wint', True), 'w_out': ('wout', False), 'xattn_wq': ('wq', False), 'xattn_wk': ('wk', False),
        'xattn_wv': ('wv', False), 'xattn_wo': ('wo', False), 'ffn_w_up': ('wupt', True),
        'ffn_w_down': ('wdown', False)}
_KEYS = [key for key, _ in _BIG.values()]
_GATHER_EARLY = ("wint", "wout", "wq", "wk", "wv", "wo")
_GATHER_WITH = {"mix": ("wupt",), "xattn": ("wdown",), "ffn": _GATHER_EARLY}
_SCATTER_LATE = ("wupt", "wdown", "wq", "wk", "wv", "wo")
_SCATTER_NEXT = ("wint", "wout")


def _block_diag(maps):
    out = jnp.zeros((D_POOL, D_POOL), maps.dtype)
    for g in range(len(POOL_WINDOWS)):
        out = lax.dynamic_update_slice(out, maps[g], (g * POOL_GROUP, g * POOL_GROUP))
    return out


def _local_step(x, mem, target, small, big, shards=None):
    distributed = shards is not None
    depth = len(shards) if distributed else len(big)
    big = list(big)
    row = lambda a: a.reshape(1, -1)
    gmem = row(small["mem_norm"])
    saved = []
    for l in range(depth):
        w = big[l]
        nxt = {}

        def plan(stage):
            layer = l + 1 if stage == "ffn" else l
            if distributed and layer < depth:
                return _Gather([shards[layer][key] for key in _GATHER_WITH[stage]])
            return None

        def landed(stage, outs):
            into = nxt if stage == "ffn" else w
            for key, g in zip(_GATHER_WITH[stage], outs):
                into[key] = g.reshape(-1, D)

        sp = dict(
            g1=row(small["mix_pre_norm"][l]), g2=row(small["mix_post_norm"][l]),
            maps=_block_diag(small["pool_maps"][l]).astype(BF16), scale=row(small["pool_scale"][l]),
            dww=small["conf_dw_w"][l], dwb=row(small["conf_dw_b"][l]), lng=row(small["conf_ln_g"][l]),
            lnb=row(small["conf_ln_b"][l]), sw=small["sconv_w"][l],
            g3=row(small["xattn_pre_norm"][l]), g4=row(small["xattn_post_norm"][l]),
            g5=row(small["ffn_pre_norm"][l]), g6=row(small["ffn_post_norm"][l]), wc=small["ffn_conv_w"][l])
        (z, ycat, y1, x1, v1, cv, pooled), got = _mix_fwd(
            x, sp["g1"], w["wint"], sp["maps"], sp["scale"], sp["dww"], sp["dwb"], sp["lng"], sp["lnb"], sp["sw"],
            w["wout"], sp["g2"], plan("mix"))
        landed("mix", got)
        k, v = _kv_fwd(mem, gmem, w["wk"], w["wv"])
        (q, p, o, y2, x2), got = _xattn_fwd(x1, sp["g3"], w["wq"], k, v, w["wo"], sp["g4"], plan("xattn"))
        landed("xattn", got)
        ffn_out, got = _ffn_fwd(x2, sp["g5"], w["wupt"], sp["wc"], w["wdown"], sp["g6"], plan("ffn"),
                                target if l == depth - 1 else None)
        u, c, a, y3, x3 = ffn_out[:5]
        landed("ffn", got)
        if nxt:
            big.append(nxt)
        saved.append(dict(sp=sp, x=x, z=z, ycat=ycat, y1=y1, x1=x1, v1=v1, cv=cv, pooled=pooled, k=k, v=v, q=q, p=p, o=o,
                          y2=y2, x2=x2, u=u, c=c, a=a, y3=y3))
        x = x3

    dx, loss_blk = x, ffn_out[5]
    big_grads = [None] * depth
    sg = {n: [None] * depth for n in ("mix_pre_norm", "mix_post_norm", "pool_maps", "pool_scale", "conf_dw_w",
                                      "conf_dw_b", "conf_ln_g", "conf_ln_b", "sconv_w", "xattn_pre_norm",
                                      "xattn_post_norm", "ffn_pre_norm", "ffn_post_norm", "ffn_conv_w")}
    dgmem = None
    pending = None
    for l in reversed(range(depth)):
        w, s = big[l], saved[l]
        sp = s["sp"]
        comm = _Scatter(pending) if distributed and pending is not None else None
        (dx, du, dy3, h3, dg5, dg6, dwc), got = _ffn_bwd(dx, s["y3"], s["x2"], s["u"], s["c"], sp["g5"], w["wupt"],
                                                        sp["wc"], w["wdown"], sp["g6"], comm)
        if comm is not None:
            big_grads[l + 1] = (big_grads[l + 1], got[0])
        g = dict(wupt=_tn_matmul(du, h3), wdown=_tn_matmul(s["a"], dy3))
        dx, g["wq"], g["wo"], dk, dv, dg3, dg4 = _xattn_bwd(dx, s["y2"], s["x1"], s["q"], s["p"], s["o"], sp["g3"], w["wq"],
                                                            s["k"], s["v"], w["wo"], sp["g4"])
        g["wk"], g["wv"], dgm = _kv_bwd(mem, gmem, dk, dv, w["wk"], w["wv"])
        dgmem = dgm if dgmem is None else dgmem + dgm
        comm = _Scatter([g[key] for key in _SCATTER_LATE]) if distributed else None
        (dx, g["wint"], g["wout"], dg1, dg2, dmaps, dscale, ddww, misc), got = _mix_bwd(
            dx, s["y1"], s["x"], s["z"], s["ycat"], s["v1"], s["cv"], s["pooled"], sp["g1"], w["wint"], sp["maps"],
            sp["scale"], sp["dww"], sp["lng"], sp["lnb"], sp["sw"], w["wout"], sp["g2"], comm)
        if distributed:
            big_grads[l] = got[0]
            pending = [g[key] for key in _SCATTER_NEXT]
        else:
            big_grads[l] = g
        sg["mix_pre_norm"][l] = dg1[0]
        sg["mix_post_norm"][l] = dg2[0]
        sg["pool_maps"][l] = jnp.stack([dmaps[i * 64:(i + 1) * 64, i * 64:(i + 1) * 64] for i in range(4)])
        sg["pool_scale"][l] = dscale[0]
        sg["conf_dw_w"][l] = ddww[0:CONF_K]
        sg["conf_dw_b"][l] = misc[0]
        sg["conf_ln_g"][l] = misc[1]
        sg["conf_ln_b"][l] = misc[2]
        sg["sconv_w"][l] = misc[3:6]
        sg["xattn_pre_norm"][l] = dg3[0]
        sg["xattn_post_norm"][l] = dg4[0]
        sg["ffn_pre_norm"][l] = dg5[0]
        sg["ffn_post_norm"][l] = dg6[0]
        sg["ffn_conv_w"][l] = dwc[0:3]
    if distributed:
        big_grads[0] = (big_grads[0], pending)
    small_grads = {n: jnp.stack(vs) for n, vs in sg.items()}
    small_grads["mem_norm"] = dgmem[0]
    return loss_blk, dx, big_grads, small_grads


_WEIGHTS = ['mem_norm', 'mix_pre_norm', 'mix_post_norm', 'w_in', 'pool_maps', 'pool_scale', 'conf_dw_w', 'conf_dw_b',
            'conf_ln_g', 'conf_ln_b', 'sconv_w', 'w_out', 'xattn_pre_norm', 'xattn_post_norm', 'xattn_wq',
            'xattn_wk', 'xattn_wv', 'xattn_wo', 'ffn_pre_norm', 'ffn_post_norm', 'ffn_w_up', 'ffn_conv_w',
            'ffn_w_down']
_CHANNEL_SHARDED = ('conf_dw_w', 'sconv_w', 'ffn_conv_w')
_SMALL = [n for n in _WEIGHTS if n not in _BIG]


def kernel(x, mem, mem_norm, mix_pre_norm, mix_post_norm, w_in, pool_maps, pool_scale, conf_dw_w, conf_dw_b, conf_ln_g, conf_ln_b, sconv_w, w_out, xattn_pre_norm, xattn_post_norm, xattn_wq, xattn_wk, xattn_wv, xattn_wo, ffn_pre_norm, ffn_post_norm, ffn_w_up, ffn_conv_w, ffn_w_down, loss_target, m_mem_norm, m_mix_pre_norm, m_mix_post_norm, m_w_in, m_pool_maps, m_pool_scale, m_conf_dw_w, m_conf_dw_b, m_conf_ln_g, m_conf_ln_b, m_sconv_w, m_w_out, m_xattn_pre_norm, m_xattn_post_norm, m_xattn_wq, m_xattn_wk, m_xattn_wv, m_xattn_wo, m_ffn_pre_norm, m_ffn_post_norm, m_ffn_w_up, m_ffn_conv_w, m_ffn_w_down, v_mem_norm, v_mix_pre_norm, v_mix_post_norm, v_w_in, v_pool_maps, v_pool_scale, v_conf_dw_w, v_conf_dw_b, v_conf_ln_g, v_conf_ln_b, v_sconv_w, v_w_out, v_xattn_pre_norm, v_xattn_post_norm, v_xattn_wq, v_xattn_wk, v_xattn_wv, v_xattn_wo, v_ffn_pre_norm, v_ffn_post_norm, v_ffn_w_up, v_ffn_conv_w, v_ffn_w_down):
    wts = dict(mem_norm=mem_norm, mix_pre_norm=mix_pre_norm, mix_post_norm=mix_post_norm, w_in=w_in,
               pool_maps=pool_maps, pool_scale=pool_scale, conf_dw_w=conf_dw_w, conf_dw_b=conf_dw_b,
               conf_ln_g=conf_ln_g, conf_ln_b=conf_ln_b, sconv_w=sconv_w, w_out=w_out,
               xattn_pre_norm=xattn_pre_norm, xattn_post_norm=xattn_post_norm, xattn_wq=xattn_wq,
               xattn_wk=xattn_wk, xattn_wv=xattn_wv, xattn_wo=xattn_wo, ffn_pre_norm=ffn_pre_norm,
               ffn_post_norm=ffn_post_norm, ffn_w_up=ffn_w_up, ffn_conv_w=ffn_conv_w, ffn_w_down=ffn_w_down)
    mom_m = dict(mem_norm=m_mem_norm, mix_pre_norm=m_mix_pre_norm, mix_post_norm=m_mix_post_norm, w_in=m_w_in,
                 pool_maps=m_pool_maps, pool_scale=m_pool_scale, conf_dw_w=m_conf_dw_w, conf_dw_b=m_conf_dw_b,
                 conf_ln_g=m_conf_ln_g, conf_ln_b=m_conf_ln_b, sconv_w=m_sconv_w, w_out=m_w_out,
                 xattn_pre_norm=m_xattn_pre_norm, xattn_post_norm=m_xattn_post_norm, xattn_wq=m_xattn_wq,
                 xattn_wk=m_xattn_wk, xattn_wv=m_xattn_wv, xattn_wo=m_xattn_wo, ffn_pre_norm=m_ffn_pre_norm,
                 ffn_post_norm=m_ffn_post_norm, ffn_w_up=m_ffn_w_up, ffn_conv_w=m_ffn_conv_w,
                 ffn_w_down=m_ffn_w_down)
    mom_v = dict(mem_norm=v_mem_norm, mix_pre_norm=v_mix_pre_norm, mix_post_norm=v_mix_post_norm, w_in=v_w_in,
                 pool_maps=v_pool_maps, pool_scale=v_pool_scale, conf_dw_w=v_conf_dw_w, conf_dw_b=v_conf_dw_b,
                 conf_ln_g=v_conf_ln_g, conf_ln_b=v_conf_ln_b, sconv_w=v_sconv_w, w_out=v_w_out,
                 xattn_pre_norm=v_xattn_pre_norm, xattn_post_norm=v_xattn_post_norm, xattn_wq=v_xattn_wq,
                 xattn_wk=v_xattn_wk, xattn_wv=v_xattn_wv, xattn_wo=v_xattn_wo, ffn_pre_norm=v_ffn_pre_norm,
                 ffn_post_norm=v_ffn_post_norm, ffn_w_up=v_ffn_w_up, ffn_conv_w=v_ffn_conv_w,
                 ffn_w_down=v_ffn_w_down)
    depth = w_in.shape[0]
    chip = 2 * lax.axis_index("x") + lax.axis_index("y")

    stacked = {}
    for name, (key, transposed) in _BIG.items():
        w = wts[name]
        wb = _cast_bf16(w.reshape(-1, w.shape[-1])).reshape(w.shape)
        stacked[key] = wb.transpose(0, 2, 1) if transposed else wb
    shards = [{key: stacked[key][l] for key in _KEYS} for l in range(depth)]
    first = _gather_two_level([shards[0][key] for key in _GATHER_EARLY])
    big0 = {key: g.reshape(-1, D) for key, g in zip(_GATHER_EARLY, first)}

    conv_shapes = [wts[n].shape for n in _CHANNEL_SHARDED]
    conv_all = _allgather_devices(_pack([wts[n] for n in _CHANNEL_SHARDED]))
    per_chip = [_unpack(conv_all[2 * j], conv_shapes) for j in range(N_CHIPS)]
    small = {n: wts[n] for n in _SMALL}
    for i, n in enumerate(_CHANNEL_SHARDED):
        small[n] = jnp.concatenate([per_chip[j][i] for j in range(N_CHIPS)], axis=-1)

    loss_blk, dx, landings, small_grads = _local_step(x[0], mem[0], loss_target[0], small, [big0], shards)

    core = lax.axis_index("c")
    grads, delta, new_m, new_v = {}, {}, {}, {}

    def finish_grads(bufs, order):
        paired = _pair_halves([_sum_blocks(buf, core) for buf in bufs])
        off = 0
        for key in order:
            name = next(n for n, (k, _) in _BIG.items() if k == key)
            h = stacked[key].shape[1] // 2
            g = jnp.stack([p[:, off:off + h].reshape(2 * h, D) for p in paired])
            grads[name] = g.transpose(0, 2, 1) if _BIG[name][1] else g
            off += h

    def adam_step(name, comm=None):
        shp = wts[name].shape
        flat = lambda a: a.reshape(-1, shp[-1])
        res = _adamw(flat(wts[name]), flat(grads[name]), flat(mom_m[name]), flat(mom_v[name]), comm)
        delta[name], new_m[name], new_v[name] = [a.reshape(shp) for a in res[:3]]
        return res[3:]

    finish_grads([pair[0] for pair in landings], _SCATTER_LATE)
    first_landing = adam_step("ffn_w_up", _Scatter(landings[0][1]))[0]
    finish_grads([first_landing] + [pair[1] for pair in landings[1:]], _SCATTER_NEXT)
    for name in _BIG:
        if name != "ffn_w_up":
            adam_step(name)

    small_shapes = [(128,)] + [small[n].shape for n in _SMALL]
    partial = _pack([loss_blk[0]] + [small_grads[n] for n in _SMALL])
    total = _unpack(_sum_devices(_allgather_devices(partial)), small_shapes)
    loss = total[0][0]
    for n, g in zip(_SMALL, total[1:]):
        if n in _CHANNEL_SHARDED:
            width = wts[n].shape[-1]
            g = lax.dynamic_slice_in_dim(g, chip * width, width, axis=-1)
        grads[n] = g

    shapes = [wts[n].shape for n in _SMALL]
    d, nm, nv = _adamw(_pack([wts[n] for n in _SMALL]), _pack([grads[n] for n in _SMALL]),
                       _pack([mom_m[n] for n in _SMALL]), _pack([mom_v[n] for n in _SMALL]))
    for out, packed in ((delta, d), (new_m, nm), (new_v, nv)):
        for n, a in zip(_SMALL, _unpack(packed, shapes)):
            out[n] = a

    return (loss, dx[None], *[grads[n] for n in _WEIGHTS], *[delta[n] for n in _WEIGHTS],
            *[new_m[n] for n in _WEIGHTS], *[new_v[n] for n in _WEIGHTS])
```
